```python
import math
import jax, jax.numpy as jnp
from jax import lax
import numpy as np

D_MODEL = 4096
BATCH = 4
SEQ = 4096
DEPTH = 1

CHUNK = 64
N_META = 16
Q_BLOCK = 128
NEG_INF = -1e30
RMS_EPS = 1e-6

MIX_WIDTH = D_MODEL
DIFF_WIDTH = MIX_WIDTH // 2
MLA_WIDTH = MIX_WIDTH - DIFF_WIDTH

DIFF_HEAD_DIM = 128
DIFF_V_DIM = 2 * DIFF_HEAD_DIM
DIFF_HEADS = DIFF_WIDTH // DIFF_V_DIM
DIFF_QK_WIDTH = DIFF_HEADS * 2 * DIFF_HEAD_DIM

MLA_NOPE = 128
MLA_ROPE = 64
MLA_V_DIM = 128
MLA_HEADS = MLA_WIDTH // MLA_V_DIM
Q_LORA = 1536
KV_LORA = 512
ROPE_THETA = 10000.0

IN_SIZES = (DIFF_QK_WIDTH, DIFF_QK_WIDTH, DIFF_WIDTH, DIFF_WIDTH,
            Q_LORA, KV_LORA, MLA_ROPE, MLA_WIDTH)
IN_WIDTH = sum(IN_SIZES)

kernel_name = 'hybrid_diffattn_mla_meta_chunk_causal'


def _rms_norm(x, g):
    xf = x.astype(jnp.float32)
    y = xf * lax.rsqrt(jnp.mean(xf * xf, axis=-1, keepdims=True) + RMS_EPS)
    return (y * g.astype(jnp.float32)).astype(x.dtype)


def _rope(x, cos, sin):
    half = x.shape[-1] // 2
    x1, x2 = x[..., :half], x[..., half:]
    c = cos.astype(x.dtype)
    s = sin.astype(x.dtype)
    return jnp.concatenate([x1 * c - x2 * s, x1 * s + x2 * c], axis=-1)


def _alibi_slopes(n_heads):
    return 2.0 ** (-8.0 * jnp.arange(1, n_heads + 1, dtype=jnp.float32) / n_heads)


def _chunk_id(pos):
    return jnp.where(pos < N_META, 0, (pos - N_META) // CHUNK + 1)


def _chunk_causal_mask(q_pos, k_pos):
    return _chunk_id(k_pos)[None, :] <= _chunk_id(q_pos)[:, None]


def _sweep(attend, q_arrays, pos):
    meta_out = attend(tuple(q[:, :N_META] for q in q_arrays), pos[:N_META], N_META)
    total = pos.shape[0]
    n_blk = (total - N_META) // Q_BLOCK

    def to_blocks(q):
        r = q[:, N_META:]
        r = r.reshape((r.shape[0], n_blk, Q_BLOCK) + r.shape[2:])
        return jnp.moveaxis(r, 1, 0)

    blocks = tuple(to_blocks(q) for q in q_arrays)
    pos_blocks = pos[N_META:].reshape(n_blk, Q_BLOCK)
    out = lax.map(lambda xs: attend(xs[:-1], xs[-1], total), blocks + (pos_blocks,))
    out = jnp.moveaxis(out, 0, 1)
    out = out.reshape((out.shape[0], n_blk * Q_BLOCK) + out.shape[3:])
    return jnp.concatenate([meta_out, out], axis=1)


def _diff_attention(q, k, v, lam, pos, slopes):
    scale = DIFF_HEAD_DIM ** -0.5

    def attend(qb, q_pos, n_keys):
        (qb,) = qb
        kb = k[:, :n_keys]
        vb = v[:, :n_keys]
        k_pos = pos[:n_keys]
        dist = jnp.abs(q_pos[:, None] - k_pos[None, :]).astype(jnp.float32)
        bias = -slopes[:, None, None] * dist[None]
        s = jnp.einsum('bqhmd,bkhmd->bmhqk', qb, kb,
                       preferred_element_type=jnp.float32) * scale + bias
        s = jnp.where(_chunk_causal_mask(q_pos, k_pos), s, NEG_INF)
        p = jax.nn.softmax(s, axis=-1)
        w = p[:, 0] - lam * p[:, 1]
        return jnp.einsum('bhqk,bkhe->bqhe', w.astype(vb.dtype), vb)

    return _sweep(attend, (q,), pos)


def _mla_attention(q_nope, q_rope, k_nope, k_rope, v, pos):
    scale = (MLA_NOPE + MLA_ROPE) ** -0.5

    def attend(qb, q_pos, n_keys):
        qn, qr = qb
        k_pos = pos[:n_keys]
        s = (jnp.einsum('bqhd,bkhd->bhqk', qn, k_nope[:, :n_keys], preferred_element_type=jnp.float32)
             + jnp.einsum('bqhr,bkr->bhqk', qr, k_rope[:, :n_keys], preferred_element_type=jnp.float32)) * scale
        s = jnp.where(_chunk_causal_mask(q_pos, k_pos), s, NEG_INF)
        p = jax.nn.softmax(s, axis=-1)
        vb = v[:, :n_keys]
        return jnp.einsum('bhqk,bkhe->bqhe', p.astype(vb.dtype), vb)

    return _sweep(attend, (q_nope, q_rope), pos)


def _hybrid_layer(h, pos, cos, sin, slopes, lambda_init, norm_pre, w_in,
                  lq1, lk1, lq2, lk2, subln, g_cq, g_ckv, w_uq, w_ukv, w_out, norm_post):
    b, l, _ = h.shape
    u = _rms_norm(h, norm_pre)
    z = jnp.einsum('bld,de->ble', u, w_in)
    points = []
    acc = 0
    for size in IN_SIZES[:-1]:
        acc += size
        points.append(acc)
    dq, dk, dv, dg, cq, ckv, kr, mg = jnp.split(z, points, axis=-1)

    dq = dq.reshape(b, l, DIFF_HEADS, 2, DIFF_HEAD_DIM)
    dk = dk.reshape(b, l, DIFF_HEADS, 2, DIFF_HEAD_DIM)
    dv = dv.reshape(b, l, DIFF_HEADS, DIFF_V_DIM)
    f32 = jnp.float32
    lam = (jnp.exp(jnp.sum(lq1.astype(f32) * lk1.astype(f32)))
           - jnp.exp(jnp.sum(lq2.astype(f32) * lk2.astype(f32))) + lambda_init)
    oa = _diff_attention(dq, dk, dv, lam, pos, slopes)
    oa = _rms_norm(oa, subln) * (1.0 - lambda_init)
    out_a = oa.reshape(b, l, DIFF_WIDTH) * jax.nn.silu(dg)

    cq = _rms_norm(cq, g_cq)
    q = jnp.einsum('blr,re->ble', cq, w_uq).reshape(b, l, MLA_HEADS, MLA_NOPE + MLA_ROPE)
    q_nope, q_rope = q[..., :MLA_NOPE], q[..., MLA_NOPE:]
    q_rope = _rope(q_rope, cos[:, None, :], sin[:, None, :])
    ckv = _rms_norm(ckv, g_ckv)
    kv = jnp.einsum('blr,re->ble', ckv, w_ukv).reshape(b, l, MLA_HEADS, MLA_NOPE + MLA_V_DIM)
    k_nope, v = kv[..., :MLA_NOPE], kv[..., MLA_NOPE:]
    k_rope = _rope(kr, cos, sin)
    ob = _mla_attention(q_nope, q_rope, k_nope, k_rope, v, pos)
    out_b = ob.reshape(b, l, MLA_WIDTH) * jax.nn.silu(mg)

    mix = jnp.concatenate([out_a, out_b], axis=-1)
    y = jnp.einsum('ble,ed->bld', mix, w_out)
    return h + _rms_norm(y, norm_post)


def setup_inputs(seed: int = 0) -> dict:
    key = jax.random.key(seed)
    ks = jax.random.split(key, 16)
    f32 = jnp.float32
    nrm = lambda k, shape, s: jax.random.normal(k, shape, f32) * s
    gain = lambda k, shape: 1.0 + 0.02 * jax.random.normal(k, shape, f32)
    return {
        'x': nrm(ks[0], (BATCH, SEQ, D_MODEL), 1.0),
        'meta_tokens': nrm(ks[1], (N_META, D_MODEL), 1.0),
        'norm_pre': gain(ks[2], (DEPTH, D_MODEL)),
        'w_in': nrm(ks[3], (DEPTH, D_MODEL, IN_WIDTH), D_MODEL ** -0.5),
        'diff_lambda_q1': nrm(ks[4], (DEPTH, DIFF_HEAD_DIM), 0.1),
        'diff_lambda_k1': nrm(ks[5], (DEPTH, DIFF_HEAD_DIM), 0.1),
        'diff_lambda_q2': nrm(ks[6], (DEPTH, DIFF_HEAD_DIM), 0.1),
        'diff_lambda_k2': nrm(ks[7], (DEPTH, DIFF_HEAD_DIM), 0.1),
        'diff_subln': gain(ks[8], (DEPTH, DIFF_V_DIM)),
        'mla_norm_q': gain(ks[9], (DEPTH, Q_LORA)),
        'mla_norm_kv': gain(ks[10], (DEPTH, KV_LORA)),
        'w_uq': nrm(ks[11], (DEPTH, Q_LORA, MLA_HEADS * (MLA_NOPE + MLA_ROPE)), Q_LORA ** -0.5),
        'w_ukv': nrm(ks[12], (DEPTH, KV_LORA, MLA_HEADS * (MLA_NOPE + MLA_V_DIM)), KV_LORA ** -0.5),
        'w_out': nrm(ks[13], (DEPTH, MIX_WIDTH, D_MODEL), MIX_WIDTH ** -0.5),
        'norm_post': gain(ks[14], (DEPTH, D_MODEL)),
    }


def reference(x, meta_tokens, norm_pre, w_in, diff_lambda_q1, diff_lambda_k1, diff_lambda_q2,
              diff_lambda_k2, diff_subln, mla_norm_q, mla_norm_kv, w_uq, w_ukv, w_out, norm_post):
    b = x.shape[0]
    meta = jnp.broadcast_to(meta_tokens[None].astype(x.dtype), (b, N_META, x.shape[-1]))
    h = jnp.concatenate([meta, x], axis=1)
    total = h.shape[1]
    pos = jnp.arange(total, dtype=jnp.int32)
    inv_freq = 1.0 / (ROPE_THETA ** (jnp.arange(0, MLA_ROPE, 2, dtype=jnp.float32) / MLA_ROPE))
    ang = pos.astype(jnp.float32)[:, None] * inv_freq[None, :]
    cos, sin = jnp.cos(ang), jnp.sin(ang)
    slopes = _alibi_slopes(DIFF_HEADS)
    for layer in range(DEPTH):
        lambda_init = 0.8 - 0.6 * math.exp(-0.3 * layer)
        h = _hybrid_layer(h, pos, cos, sin, slopes, lambda_init, norm_pre[layer], w_in[layer],
                          diff_lambda_q1[layer], diff_lambda_k1[layer], diff_lambda_q2[layer],
                          diff_lambda_k2[layer], diff_subln[layer], mla_norm_q[layer],
                          mla_norm_kv[layer], w_uq[layer], w_ukv[layer], w_out[layer], norm_post[layer])
    return h[:, N_META:]
```

```python
import functools
import math

import jax
import jax.numpy as jnp
import numpy as np
from jax import lax
from jax.experimental import pallas as pl
from jax.experimental.pallas import tpu as pltpu

CHUNK = 64
RMS_EPS = 1e-6
MASK_VALUE = -1e30
DIFF_HEAD_DIM = 128
DIFF_V_DIM = 2 * DIFF_HEAD_DIM
MLA_NOPE = 128
MLA_ROPE = 64
MLA_V_DIM = 128
ROPE_THETA = 10000.0

LANES = 128
HEAD_TILE = 2 * LANES
VMEM_LIMIT = 56 * 1024 * 1024

ATT_TILE = 256
M_INIT = -1e38

F32 = jnp.float32
BF16 = jnp.bfloat16
NP_BF16 = np.dtype(jnp.bfloat16)


def _params(*sem):
    return pltpu.CompilerParams(dimension_semantics=sem, vmem_limit_bytes=VMEM_LIMIT)


def _tile(dim, target, mult=LANES):
    if dim <= target:
        return dim
    t = (target // mult) * mult
    while t > mult and dim % t:
        t -= mult
    assert dim % t == 0, (dim, target)
    return t


def _dot(a, b):
    return jnp.dot(a, b, preferred_element_type=F32)


def _dot_nt(a, b):
    return lax.dot_general(a, b, (((1,), (1,)), ((), ())), preferred_element_type=F32)


def _prenorm_kernel(x_ref, g_ref, o_ref):
    x = x_ref[...]
    ms = jnp.mean(x * x, axis=-1, keepdims=True)
    o_ref[...] = (x * lax.rsqrt(ms + RMS_EPS) * g_ref[...]).astype(o_ref.dtype)


def _prenorm(x2d, g):
    m, d = x2d.shape
    tm = _tile(m, 256, 8)
    return pl.pallas_call(
        _prenorm_kernel,
        out_shape=jax.ShapeDtypeStruct((m, d), BF16),
        grid=(m // tm,),
        in_specs=[pl.BlockSpec((tm, d), lambda i: (i, 0)),
                  pl.BlockSpec((1, d), lambda i: (0, 0))],
        out_specs=pl.BlockSpec((tm, d), lambda i: (i, 0)),
        compiler_params=_params("arbitrary"),
        name="prenorm",
    )(x2d, g)


def _inproj_kernel(u_ref, w_ref, s_ref, wkr_ref, z_ref, kr_ref):
    u = u_ref[...]
    z_ref[...] = (_dot(u, w_ref[...]) * s_ref[...]).astype(z_ref.dtype)

    @pl.when(pl.program_id(1) == 0)
    def _():
        kr_ref[...] = _dot(u, wkr_ref[...])


def _inproj(u, w, colscale, wkr):
    m, d = u.shape
    n = w.shape[1]
    tm = _tile(m, 1024, 16)
    tn = _tile(n, 1024)
    return pl.pallas_call(
        _inproj_kernel,
        out_shape=(jax.ShapeDtypeStruct((m, n), BF16),
                   jax.ShapeDtypeStruct((m, LANES), F32)),
        grid=(m // tm, n // tn),
        in_specs=[pl.BlockSpec((tm, d), lambda i, j: (i, 0)),
                  pl.BlockSpec((d, tn), lambda i, j: (0, j)),
                  pl.BlockSpec((1, tn), lambda i, j: (0, j)),
                  pl.BlockSpec((d, LANES), lambda i, j: (0, 0))],
        out_specs=(pl.BlockSpec((tm, tn), lambda i, j: (i, j)),
                   pl.BlockSpec((tm, LANES), lambda i, j: (i, 0))),
        compiler_params=_params("arbitrary", "arbitrary"),
        name="inproj",
    )(u, w, colscale, wkr)


def _vt_kernel(w_ref, u_ref, o_ref):
    o_ref[0, 0] = _dot_nt(w_ref[...], u_ref[...]).astype(o_ref.dtype)


def _v_transposed(w_t, u, batch, t):
    n, d = w_t.shape
    m = u.shape[0]
    nt = m // (batch * t)
    return pl.pallas_call(
        _vt_kernel,
        out_shape=jax.ShapeDtypeStruct((batch, nt, n, t), BF16),
        grid=(m // t,),
        in_specs=[pl.BlockSpec((n, d), lambda i: (0, 0)),
                  pl.BlockSpec((t, d), lambda i: (i, 0))],
        out_specs=pl.BlockSpec((1, 1, n, t), lambda i: (i // nt, i % nt, 0, 0)),
        compiler_params=_params("arbitrary"),
        name="inproj_vt",
    )(w_t, u)


def _rope_combine(t):
    r = t + pltpu.roll(t, MLA_ROPE, axis=1)
    lane = lax.broadcasted_iota(jnp.int32, r.shape, 1)
    return jnp.where(lane < MLA_ROPE, r, 0.0), lane


def _mla_q_kernel(cq_ref, g_ref, w_ref, tab_ref, o_ref, cqn_ref, *, heads_per_step, scale):
    @pl.when(pl.program_id(1) == 0)
    def _():
        c = cq_ref[...].astype(F32)
        ms = jnp.mean(c * c, axis=-1, keepdims=True)
        cqn_ref[...] = (c * lax.rsqrt(ms + RMS_EPS) * g_ref[...]).astype(cqn_ref.dtype)

    y = _dot(cqn_ref[...], w_ref[...])
    tab = tab_ref[...]
    for hh in range(heads_per_step):
        base = hh * HEAD_TILE
        nope = y[:, base:base + MLA_NOPE] * scale
        r, lane = _rope_combine(y[:, base + MLA_NOPE:base + HEAD_TILE] * tab)
        r = jnp.where(lane == MLA_ROPE, 1.0, r * scale)
        o_ref[:, base:base + MLA_NOPE] = nope.astype(o_ref.dtype)
        o_ref[:, base + MLA_NOPE:base + HEAD_TILE] = r.astype(o_ref.dtype)


def _mla_q(z, g, w, tab, n_heads, q_lora, seq):
    m = z.shape[0]
    tm = _tile(m, 1024, 16)
    tm = math.gcd(tm, seq)
    hp = _tile(n_heads, 4, 1)
    nseq = seq // tm
    kern = functools.partial(_mla_q_kernel, heads_per_step=hp,
                             scale=float((MLA_NOPE + MLA_ROPE) ** -0.5))
    return pl.pallas_call(
        kern,
        out_shape=jax.ShapeDtypeStruct((m, n_heads * HEAD_TILE), BF16),
        grid=(m // tm, n_heads // hp),
        in_specs=[pl.BlockSpec((tm, q_lora), lambda i, j: (i, 0)),
                  pl.BlockSpec((1, q_lora), lambda i, j: (0, 0)),
                  pl.BlockSpec((q_lora, hp * HEAD_TILE), lambda i, j: (0, j)),
                  pl.BlockSpec((tm, LANES), lambda i, j: (i % nseq, 0))],
        out_specs=pl.BlockSpec((tm, hp * HEAD_TILE), lambda i, j: (i, j)),
        scratch_shapes=[pltpu.VMEM((tm, q_lora), BF16)],
        compiler_params=_params("arbitrary", "arbitrary"),
        name="mla_q",
    )(z, g, w, tab)


def _mla_kv_kernel(ckv_ref, kr_ref, g_ref, wk_ref, wvt_ref, tab_ref, k_ref, vt_ref, *, n_heads, t):
    c = ckv_ref[...].astype(F32)
    ms = jnp.mean(c * c, axis=-1, keepdims=True)
    cn = (c * lax.rsqrt(ms + RMS_EPS) * g_ref[...]).astype(BF16)
    kn = _dot(cn, wk_ref[...])
    kr, _ = _rope_combine(kr_ref[...] * tab_ref[...])
    kr = kr.astype(k_ref.dtype)
    for h in range(n_heads):
        k_ref[:, h * HEAD_TILE:h * HEAD_TILE + MLA_NOPE] = (
            kn[:, h * MLA_NOPE:(h + 1) * MLA_NOPE].astype(k_ref.dtype))
        k_ref[:, h * HEAD_TILE + MLA_NOPE:(h + 1) * HEAD_TILE] = kr
    vt = _dot_nt(wvt_ref[...], cn).astype(vt_ref.dtype)
    for tt in range(vt_ref.shape[1]):
        vt_ref[0, tt] = vt[:, tt * t:(tt + 1) * t]


def _mla_kv(z, kr, g, wk, wvt, tab, n_heads, kv_lora, ckv_block, batch, seq, t):
    m = z.shape[0]
    tm = math.gcd(_tile(m, 1024, 16), seq)
    t = min(t, tm)
    nseq = seq // tm
    kern = functools.partial(_mla_kv_kernel, n_heads=n_heads, t=t)
    return pl.pallas_call(
        kern,
        out_shape=(jax.ShapeDtypeStruct((m, n_heads * HEAD_TILE), BF16),
                   jax.ShapeDtypeStruct((batch, seq // t, n_heads * MLA_V_DIM, t), BF16)),
        grid=(m // tm,),
        in_specs=[pl.BlockSpec((tm, kv_lora), lambda i: (i, ckv_block)),
                  pl.BlockSpec((tm, LANES), lambda i: (i, 0)),
                  pl.BlockSpec((1, kv_lora), lambda i: (0, 0)),
                  pl.BlockSpec((kv_lora, n_heads * MLA_NOPE), lambda i: (0, 0)),
                  pl.BlockSpec((n_heads * MLA_V_DIM, kv_lora), lambda i: (0, 0)),
                  pl.BlockSpec((tm, LANES), lambda i: (i % nseq, 0))],
        out_specs=(pl.BlockSpec((tm, n_heads * HEAD_TILE), lambda i: (i, 0)),
                   pl.BlockSpec((1, tm // t, n_heads * MLA_V_DIM, t),
                                lambda i: (i // nseq, i % nseq, 0, 0))),
        compiler_params=_params("arbitrary"),
        name="mla_kv",
    )(z, kr, g, wk, wvt, tab)


def _softmax_block(s_t, vt_blk, m_ref, l_ref, acc_ref, idx):
    m_prev = m_ref[idx]
    m_new = jnp.maximum(m_prev, jnp.max(s_t, axis=0, keepdims=True))
    alpha = jnp.exp(m_prev - m_new)
    p = jnp.exp(s_t - m_new)
    l_ref[idx] = alpha * l_ref[idx] + jnp.sum(p, axis=0, keepdims=True)
    acc_ref[idx] = alpha * acc_ref[idx] + _dot(vt_blk, p.astype(BF16))
    m_ref[idx] = m_new


def _init_stats(m_ref, l_ref, acc_ref):
    m_ref[...] = jnp.full(m_ref.shape, M_INIT, F32)
    l_ref[...] = jnp.zeros(l_ref.shape, F32)
    acc_ref[...] = jnp.zeros(acc_ref.shape, F32)


def _silu(g):
    return g * jax.nn.sigmoid(g)


def _diff_attn_kernel(q_ref, k_ref, vt_ref, g_ref, kmeta_ref, vtmeta_ref, eq_ref, ek_ref,
                      ekmeta_ref, dadd_ref, lq1_ref, lk1_ref, lq2_ref, lk2_ref, subln_ref,
                      o_ref, kaug_ref, kmaug_ref, vtm_ref, m_ref, l_ref, acc_ref,
                      *, lambda_init, t, n_meta):
    qi = pl.program_id(2)
    d = DIFF_HEAD_DIM

    @pl.when(qi == 0)
    def _build():
        kmaug_ref[...] = jnp.zeros(kmaug_ref.shape, BF16)
        vtm_ref[...] = jnp.zeros(vtm_ref.shape, BF16)
        vtm_ref[:, 0:n_meta] = vtmeta_ref[...]
        for mm in range(2):
            kaug_ref[mm, :, 0:d] = k_ref[0, :, mm * d:(mm + 1) * d]
            kaug_ref[mm, :, d:2 * d] = ek_ref[0]
            kmaug_ref[mm, 0:n_meta, 0:d] = kmeta_ref[:, mm * d:(mm + 1) * d]
            kmaug_ref[mm, :, d:2 * d] = ekmeta_ref[0]

    q = q_ref[0]
    eq = eq_ref[0]
    qa = [jnp.concatenate([q[:, mm * d:(mm + 1) * d], eq], axis=1) for mm in range(2)]
    _init_stats(m_ref, l_ref, acc_ref)

    vtm = vtm_ref[...]
    for mm in range(2):
        _softmax_block(_dot_nt(kmaug_ref[mm], qa[mm]), vtm, m_ref, l_ref, acc_ref, mm)

    def body(kb, carry):
        off = pl.multiple_of(kb * t, t)
        vblk = vt_ref[0, kb]
        for mm in range(2):
            s_t = _dot_nt(kaug_ref[mm, pl.ds(off, t), :], qa[mm])
            _softmax_block(s_t, vblk, m_ref, l_ref, acc_ref, mm)
        return carry

    lax.fori_loop(0, qi, body, 0)

    off = pl.multiple_of(qi * t, t)
    vblk = vt_ref[0, qi]
    dadd = dadd_ref[0]
    for mm in range(2):
        s_t = _dot_nt(kaug_ref[mm, pl.ds(off, t), :], qa[mm]) + dadd
        _softmax_block(s_t, vblk, m_ref, l_ref, acc_ref, mm)

    lam = (jnp.exp(jnp.sum(lq1_ref[...] * lk1_ref[...], axis=-1, keepdims=True))
           - jnp.exp(jnp.sum(lq2_ref[...] * lk2_ref[...], axis=-1, keepdims=True))
           + lambda_init)
    o_t = acc_ref[0] / l_ref[0] - lam * (acc_ref[1] / l_ref[1])
    ms = jnp.mean(o_t * o_t, axis=0, keepdims=True)
    o = (o_t * lax.rsqrt(ms + RMS_EPS)).T
    o = (o * subln_ref[...]) * (1.0 - lambda_init)
    o_ref[0] = (o * _silu(g_ref[0].astype(F32))).astype(o_ref.dtype)


def _diff_attn(z3, vt, z_meta, vt_meta, eq_tab, ek_tab, ekmeta_tab, dadd_tab, lams, subln,
               n_heads, blk, lambda_init, t):
    b, s, _ = z3.shape
    n_meta = z_meta.shape[0]
    nq = s // t
    q0, k0, g0 = blk
    vec = pl.BlockSpec((1, DIFF_HEAD_DIM), lambda bb, h, qi: (0, 0))
    kern = functools.partial(_diff_attn_kernel, lambda_init=lambda_init, t=t, n_meta=n_meta)
    return pl.pallas_call(
        kern,
        out_shape=jax.ShapeDtypeStruct((b, s, n_heads * DIFF_V_DIM), BF16),
        grid=(b, n_heads, nq),
        in_specs=[
            pl.BlockSpec((1, t, HEAD_TILE), lambda bb, h, qi: (bb, qi, q0 + h)),
            pl.BlockSpec((1, s, HEAD_TILE), lambda bb, h, qi: (bb, 0, k0 + h)),
            pl.BlockSpec((1, nq, DIFF_V_DIM, t), lambda bb, h, qi: (bb, 0, h, 0)),
            pl.BlockSpec((1, t, DIFF_V_DIM), lambda bb, h, qi: (bb, qi, g0 + h)),
            pl.BlockSpec((n_meta, HEAD_TILE), lambda bb, h, qi: (0, k0 + h)),
            pl.BlockSpec((DIFF_V_DIM, n_meta), lambda bb, h, qi: (h, 0)),
            pl.BlockSpec((1, t, LANES), lambda bb, h, qi: (h, qi, 0)),
            pl.BlockSpec((1, s, LANES), lambda bb, h, qi: (h, 0, 0)),
            pl.BlockSpec((1, LANES, LANES), lambda bb, h, qi: (h, 0, 0)),
            pl.BlockSpec((1, t, t), lambda bb, h, qi: (h, 0, 0)),
            vec, vec, vec, vec,
            pl.BlockSpec((1, DIFF_V_DIM), lambda bb, h, qi: (0, 0)),
        ],
        out_specs=pl.BlockSpec((1, t, DIFF_V_DIM), lambda bb, h, qi: (bb, qi, h)),
        scratch_shapes=[
            pltpu.VMEM((2, s, HEAD_TILE), BF16),
            pltpu.VMEM((2, LANES, HEAD_TILE), BF16),
            pltpu.VMEM((DIFF_V_DIM, LANES), BF16),
            pltpu.VMEM((2, 1, t), F32),
            pltpu.VMEM((2, 1, t), F32),
            pltpu.VMEM((2, DIFF_V_DIM, t), F32),
        ],
        compiler_params=_params("arbitrary", "arbitrary", "arbitrary"),
        name="diff_attn",
    )(z3, z3, vt, z3, z_meta, vt_meta, eq_tab, ek_tab, ekmeta_tab, dadd_tab, *lams, subln)


def _mla_attn_kernel(q_ref, k_ref, vt_ref, g_ref, kmeta_ref, vtmeta_ref, padk_ref, madd_ref,
                     o_ref, kmaug_ref, vtm_ref, m_ref, l_ref, acc_ref, *, t, n_meta):
    qi = pl.program_id(2)

    @pl.when(qi == 0)
    def _build():
        kmaug_ref[...] = padk_ref[...]
        kmaug_ref[0:n_meta, :] = kmeta_ref[...]
        vtm_ref[...] = jnp.zeros(vtm_ref.shape, BF16)
        vtm_ref[:, 0:n_meta] = vtmeta_ref[...]

    q = q_ref[0]
    _init_stats(m_ref, l_ref, acc_ref)
    _softmax_block(_dot_nt(kmaug_ref[...], q), vtm_ref[...], m_ref, l_ref, acc_ref, 0)

    def body(kb, carry):
        off = pl.multiple_of(kb * t, t)
        s_t = _dot_nt(k_ref[0, pl.ds(off, t), :], q)
        _softmax_block(s_t, vt_ref[0, kb], m_ref, l_ref, acc_ref, 0)
        return carry

    lax.fori_loop(0, qi, body, 0)

    off = pl.multiple_of(qi * t, t)
    s_t = _dot_nt(k_ref[0, pl.ds(off, t), :], q) + madd_ref[...]
    _softmax_block(s_t, vt_ref[0, qi], m_ref, l_ref, acc_ref, 0)

    o = (acc_ref[0] / l_ref[0]).T
    o_ref[0] = (o * _silu(g_ref[0].astype(F32))).astype(o_ref.dtype)


def _mla_attn(q3, k3, vt, z3, k_meta, vt_meta, padk, madd, n_heads, g0, t):
    b, s, _ = q3.shape
    n_meta = k_meta.shape[0]
    nq = s // t
    kern = functools.partial(_mla_attn_kernel, t=t, n_meta=n_meta)
    return pl.pallas_call(
        kern,
        out_shape=jax.ShapeDtypeStruct((b, s, n_heads * MLA_V_DIM), BF16),
        grid=(b, n_heads, nq),
        in_specs=[
            pl.BlockSpec((1, t, HEAD_TILE), lambda bb, h, qi: (bb, qi, h)),
            pl.BlockSpec((1, s, HEAD_TILE), lambda bb, h, qi: (bb, 0, h)),
            pl.BlockSpec((1, nq, MLA_V_DIM, t), lambda bb, h, qi: (bb, 0, h, 0)),
            pl.BlockSpec((1, t, MLA_V_DIM), lambda bb, h, qi: (bb, qi, g0 + h)),
            pl.BlockSpec((n_meta, HEAD_TILE), lambda bb, h, qi: (0, h)),
            pl.BlockSpec((MLA_V_DIM, n_meta), lambda bb, h, qi: (h, 0)),
            pl.BlockSpec((LANES, HEAD_TILE), lambda bb, h, qi: (0, 0)),
            pl.BlockSpec((t, t), lambda bb, h, qi: (0, 0)),
        ],
        out_specs=pl.BlockSpec((1, t, MLA_V_DIM), lambda bb, h, qi: (bb, qi, h)),
        scratch_shapes=[
            pltpu.VMEM((LANES, HEAD_TILE), BF16),
            pltpu.VMEM((MLA_V_DIM, LANES), BF16),
            pltpu.VMEM((1, 1, t), F32),
            pltpu.VMEM((1, 1, t), F32),
            pltpu.VMEM((1, MLA_V_DIM, t), F32),
        ],
        compiler_params=_params("arbitrary", "arbitrary", "arbitrary"),
        name="mla_attn",
    )(q3, k3, vt, z3, k_meta, vt_meta, padk, madd)


def _outproj_kernel(a_ref, b_ref, wa_ref, wb_ref, y_ref):
    y_ref[...] = _dot(a_ref[...], wa_ref[...]) + _dot(b_ref[...], wb_ref[...])


def _outproj(mix_a, mix_b, wa, wb):
    m, ka = mix_a.shape
    kb = mix_b.shape[1]
    n = wa.shape[1]
    tm = _tile(m, 1024, 16)
    tn = _tile(n, 1024)
    return pl.pallas_call(
        _outproj_kernel,
        out_shape=jax.ShapeDtypeStruct((m, n), F32),
        grid=(m // tm, n // tn),
        in_specs=[pl.BlockSpec((tm, ka), lambda i, j: (i, 0)),
                  pl.BlockSpec((tm, kb), lambda i, j: (i, 0)),
                  pl.BlockSpec((ka, tn), lambda i, j: (0, j)),
                  pl.BlockSpec((kb, tn), lambda i, j: (0, j))],
        out_specs=pl.BlockSpec((tm, tn), lambda i, j: (i, j)),
        compiler_params=_params("arbitrary", "arbitrary"),
        name="outproj",
    )(mix_a, mix_b, wa, wb)


def _postnorm_kernel(x_ref, y_ref, g_ref, o_ref):
    y = y_ref[...]
    ms = jnp.mean(y * y, axis=-1, keepdims=True)
    o_ref[...] = x_ref[...] + y * lax.rsqrt(ms + RMS_EPS) * g_ref[...]


def _postnorm(x2d, y, g):
    m, d = x2d.shape
    tm = _tile(m, 256, 8)
    return pl.pallas_call(
        _postnorm_kernel,
        out_shape=jax.ShapeDtypeStruct((m, d), F32),
        grid=(m // tm,),
        in_specs=[pl.BlockSpec((tm, d), lambda i: (i, 0)),
                  pl.BlockSpec((tm, d), lambda i: (i, 0)),
                  pl.BlockSpec((1, d), lambda i: (0, 0))],
        out_specs=pl.BlockSpec((tm, d), lambda i: (i, 0)),
        compiler_params=_params("arbitrary"),
        name="postnorm",
    )(x2d, y, g)


def _split3(v):
    v = np.asarray(v, np.float32)
    hi = v.astype(NP_BF16)
    r = v - hi.astype(np.float32)
    mid = r.astype(NP_BF16)
    lo = (r - mid.astype(np.float32)).astype(NP_BF16)
    return hi, mid, lo


def _rope_table(pos):
    inv_freq = 1.0 / (ROPE_THETA ** (jnp.arange(0, MLA_ROPE, 2, dtype=F32) / MLA_ROPE))
    ang = pos.astype(F32)[:, None] * inv_freq[None, :]
    cos, sin = jnp.cos(ang), jnp.sin(ang)
    return jnp.concatenate([cos, cos, sin, sin], axis=1)


def _alibi_tables(n_heads, n_meta, seq, t):
    f32 = np.float32
    slopes = (2.0 ** (-8.0 * np.arange(1, n_heads + 1, dtype=f32) / n_heads)).astype(f32)
    pos_q = ((np.arange(seq, dtype=f32) + n_meta)[None, :] * slopes[:, None]).astype(f32)
    pos_m = (np.arange(LANES, dtype=f32)[None, :] * slopes[:, None]).astype(f32)
    ones_q = np.ones((n_heads, seq), NP_BF16)
    zeros_q = np.zeros((n_heads, seq), NP_BF16)
    qh, qm, ql = _split3(-pos_q)
    kh, km, kl = _split3(pos_q)
    pad = np.zeros((n_heads, seq, LANES - 7), NP_BF16)
    eq = np.concatenate([np.stack([qh, qm, ql, ones_q, ones_q, ones_q, ones_q], -1), pad], -1)
    ek = np.concatenate([np.stack([ones_q, ones_q, ones_q, zeros_q, kh, km, kl], -1), pad], -1)
    mh, mmid, ml = _split3(pos_m)
    valid = (np.arange(LANES) < n_meta)[None, :]
    padmask = np.broadcast_to(np.where(valid, 0.0, MASK_VALUE).astype(f32),
                              (n_heads, LANES)).astype(NP_BF16)
    ones_m = np.ones((n_heads, LANES), NP_BF16)
    ekm = np.concatenate(
        [np.stack([ones_m, ones_m, ones_m, padmask, mh, mmid, ml], -1),
         np.zeros((n_heads, LANES, LANES - 7), NP_BF16)], -1)
    j = np.arange(t)[:, None]
    i = np.arange(t)[None, :]
    visible = (j // CHUNK) <= (i // CHUNK)
    fut = (2.0 * np.maximum(j - i, 0)).astype(f32)
    dadd = np.where(visible[None], -slopes[:, None, None] * fut[None], MASK_VALUE).astype(f32)
    madd = np.where(visible, 0.0, MASK_VALUE).astype(f32)
    return tuple(jnp.asarray(a) for a in (eq, ek, ekm, dadd, madd))


def kernel(x, meta_tokens, norm_pre, w_in, diff_lambda_q1, diff_lambda_k1, diff_lambda_q2,
           diff_lambda_k2, diff_subln, mla_norm_q, mla_norm_kv, w_uq, w_ukv, w_out, norm_post):
    assert norm_pre.shape[0] == 1, "single-layer block"
    b, s, d = x.shape
    n_meta = meta_tokens.shape[0]
    mix = w_out.shape[1]
    diff_w = mix // 2
    mla_w = mix - diff_w
    ha = diff_w // DIFF_V_DIM
    hb = mla_w // MLA_V_DIM
    q_lora = mla_norm_q.shape[-1]
    kv_lora = mla_norm_kv.shape[-1]
    t = min(ATT_TILE, s)
    assert s % t == 0 and t % CHUNK == 0 and n_meta <= 16
    lambda_init = 0.8 - 0.6 * math.exp(-0.3 * 0)

    w = w_in[0]
    sizes = (diff_w, diff_w, diff_w, diff_w, q_lora, kv_lora, MLA_ROPE, mla_w)
    offs = np.cumsum((0,) + sizes)
    w_dq, w_dk, w_dv, w_dg, w_cq, w_ckv, w_kr, w_mg = (w[:, offs[i]:offs[i + 1]] for i in range(8))
    w_main = jnp.concatenate([w_cq, w_ckv, w_dq, w_dk, w_dg, w_mg], axis=1).astype(BF16)
    n_main = w_main.shape[1]
    assert q_lora % kv_lora == 0 and (q_lora + kv_lora) % HEAD_TILE == 0
    q_blk0 = (q_lora + kv_lora) // HEAD_TILE
    k_blk0 = q_blk0 + diff_w // HEAD_TILE
    g_blk0 = k_blk0 + diff_w // HEAD_TILE
    mg_blk0 = (q_lora + kv_lora + 3 * diff_w) // MLA_V_DIM
    colscale = jnp.ones((1, n_main), F32).at[:, q_lora + kv_lora:q_lora + kv_lora + diff_w].set(
        DIFF_HEAD_DIM ** -0.5)
    half = MLA_ROPE // 2

    def rot(wr):
        return jnp.concatenate([-wr[:, half:], wr[:, :half]], axis=1)

    w_kr2 = jnp.concatenate([w_kr, rot(w_kr)], axis=1).astype(BF16)
    w_dvt = w_dv.T.astype(BF16)

    wq = w_uq[0].reshape(q_lora, hb, MLA_NOPE + MLA_ROPE)
    wq_r = wq[:, :, MLA_NOPE:]
    wq_full = jnp.concatenate(
        [wq[:, :, :MLA_NOPE], wq_r, jnp.concatenate([-wq_r[:, :, half:], wq_r[:, :, :half]], -1)],
        axis=-1).reshape(q_lora, hb * HEAD_TILE).astype(BF16)
    wkv = w_ukv[0].reshape(kv_lora, hb, MLA_NOPE + MLA_V_DIM)
    w_kn = wkv[:, :, :MLA_NOPE].reshape(kv_lora, hb * MLA_NOPE).astype(BF16)
    w_vt = wkv[:, :, MLA_NOPE:].reshape(kv_lora, hb * MLA_V_DIM).T.astype(BF16)
    wo = w_out[0].astype(BF16)
    wo_a, wo_b = wo[:diff_w], wo[diff_w:]

    pos = jnp.arange(n_meta + s, dtype=jnp.int32)
    rope_tab = _rope_table(pos)
    tab_meta, tab_seq = rope_tab[:n_meta], rope_tab[n_meta:]
    eq_tab, ek_tab, ekm_tab, dadd_tab, madd_tab = _alibi_tables(ha, n_meta, s, t)
    lane = jnp.arange(HEAD_TILE)
    row = jnp.arange(LANES)
    padk = jnp.where((lane[None, :] == MLA_NOPE + MLA_ROPE) & (row[:, None] >= n_meta),
                     MASK_VALUE, 0.0).astype(BF16)

    x2d = x.reshape(b * s, d)
    u = _prenorm(x2d, norm_pre)
    u_meta = _prenorm(meta_tokens.astype(x.dtype), norm_pre)
    z, kr = _inproj(u, w_main, colscale, w_kr2)
    z_meta, kr_meta = _inproj(u_meta, w_main, colscale, w_kr2)
    vt_a = _v_transposed(w_dvt, u, b, t)
    vt_a_meta = _v_transposed(w_dvt, u_meta, 1, n_meta)[0, 0]

    g_cq = mla_norm_q.astype(F32)
    g_ckv = mla_norm_kv.astype(F32)
    ckv_block = q_lora // kv_lora
    q_b = _mla_q(z, g_cq, wq_full, tab_seq, hb, q_lora, s)
    k_b, vt_b = _mla_kv(z, kr, g_ckv, w_kn, w_vt, tab_seq, hb, kv_lora, ckv_block, b, s, t)
    k_b_meta, vt_b_meta = _mla_kv(z_meta, kr_meta, g_ckv, w_kn, w_vt, tab_meta, hb, kv_lora,
                                  ckv_block, 1, n_meta, n_meta)
    vt_b_meta = vt_b_meta[0, 0]

    z3 = z.reshape(b, s, n_main)
    lams = [v.astype(F32) for v in (diff_lambda_q1, diff_lambda_k1, diff_lambda_q2, diff_lambda_k2)]
    mix_a = _diff_attn(z3, vt_a, z_meta, vt_a_meta, eq_tab, ek_tab, ekm_tab, dadd_tab, lams,
                       diff_subln.astype(F32), ha, (q_blk0, k_blk0, g_blk0), lambda_init, t)
    mix_b = _mla_attn(q_b.reshape(b, s, hb * HEAD_TILE), k_b.reshape(b, s, hb * HEAD_TILE), vt_b,
                      z3, k_b_meta, vt_b_meta, padk, madd_tab, hb, mg_blk0, t)

    y = _outproj(mix_a.reshape(b * s, diff_w), mix_b.reshape(b * s, mla_w), wo_a, wo_b)
    out = _postnorm(x2d, y, norm_post.astype(F32))
    return out.reshape(b, s, d)
```

```python
import functools
import math

import jax
import jax.numpy as jnp
import numpy as np
from jax import lax
from jax.experimental import pallas as pl
from jax.experimental.pallas import tpu as pltpu

CHUNK = 64
RMS_EPS = 1e-6
MASK_VALUE = -1e30
DIFF_HEAD_DIM = 128
DIFF_V_DIM = 2 * DIFF_HEAD_DIM
MLA_NOPE = 128
MLA_ROPE = 64
MLA_V_DIM = 128
ROPE_THETA = 10000.0

LANES = 128
HEAD_TILE = 2 * LANES
VMEM_LIMIT = 56 * 1024 * 1024

KEY_TILE = 256
QUERY_TILE = 512
M_INIT = -1e38

F32 = jnp.float32
BF16 = jnp.bfloat16
NP_BF16 = np.dtype(jnp.bfloat16)


def _params(*sem):
    return pltpu.CompilerParams(dimension_semantics=sem, vmem_limit_bytes=VMEM_LIMIT)


def _tile(dim, target, mult=LANES):
    if dim <= target:
        return dim
    t = (target // mult) * mult
    while t > mult and dim % t:
        t -= mult
    assert dim % t == 0, (dim, target)
    return t


def _dot(a, b):
    return jnp.dot(a, b, preferred_element_type=F32)


def _dot_nt(a, b):
    return lax.dot_general(a, b, (((1,), (1,)), ((), ())), preferred_element_type=F32)


def _prenorm_kernel(x_ref, g_ref, o_ref):
    x = x_ref[...]
    ms = jnp.mean(x * x, axis=-1, keepdims=True)
    o_ref[...] = (x * lax.rsqrt(ms + RMS_EPS) * g_ref[...]).astype(o_ref.dtype)


def _prenorm(x2d, g):
    m, d = x2d.shape
    tm = _tile(m, 256, 8)
    return pl.pallas_call(
        _prenorm_kernel,
        out_shape=jax.ShapeDtypeStruct((m, d), BF16),
        grid=(m // tm,),
        in_specs=[pl.BlockSpec((tm, d), lambda i: (i, 0)),
                  pl.BlockSpec((1, d), lambda i: (0, 0))],
        out_specs=pl.BlockSpec((tm, d), lambda i: (i, 0)),
        compiler_params=_params("arbitrary"),
        name="prenorm",
    )(x2d, g)


def _inproj_kernel(u_ref, w_ref, s_ref, wkr_ref, z_ref, kr_ref):
    u = u_ref[...]
    z_ref[...] = (_dot(u, w_ref[...]) * s_ref[...]).astype(z_ref.dtype)

    @pl.when(pl.program_id(1) == 0)
    def _():
        kr_ref[...] = _dot(u, wkr_ref[...])


def _inproj(u, w, colscale, wkr):
    m, d = u.shape
    n = w.shape[1]
    tm = _tile(m, 1024, 16)
    tn = _tile(n, 1024)
    return pl.pallas_call(
        _inproj_kernel,
        out_shape=(jax.ShapeDtypeStruct((m, n), BF16),
                   jax.ShapeDtypeStruct((m, LANES), F32)),
        grid=(m // tm, n // tn),
        in_specs=[pl.BlockSpec((tm, d), lambda i, j: (i, 0)),
                  pl.BlockSpec((d, tn), lambda i, j: (0, j)),
                  pl.BlockSpec((1, tn), lambda i, j: (0, j)),
                  pl.BlockSpec((d, LANES), lambda i, j: (0, 0))],
        out_specs=(pl.BlockSpec((tm, tn), lambda i, j: (i, j)),
                   pl.BlockSpec((tm, LANES), lambda i, j: (i, 0))),
        compiler_params=_params("arbitrary", "arbitrary"),
        name="inproj",
    )(u, w, colscale, wkr)


def _vt_kernel(w_ref, u_ref, o_ref):
    o_ref[0, 0] = _dot_nt(w_ref[...], u_ref[...]).astype(o_ref.dtype)


def _v_transposed(w_t, u, batch, t):
    n, d = w_t.shape
    m = u.shape[0]
    nt = m // (batch * t)
    return pl.pallas_call(
        _vt_kernel,
        out_shape=jax.ShapeDtypeStruct((batch, nt, n, t), BF16),
        grid=(m // t,),
        in_specs=[pl.BlockSpec((n, d), lambda i: (0, 0)),
                  pl.BlockSpec((t, d), lambda i: (i, 0))],
        out_specs=pl.BlockSpec((1, 1, n, t), lambda i: (i // nt, i % nt, 0, 0)),
        compiler_params=_params("arbitrary"),
        name="inproj_vt",
    )(w_t, u)


def _rope_combine(t):
    r = t + pltpu.roll(t, MLA_ROPE, axis=1)
    lane = lax.broadcasted_iota(jnp.int32, r.shape, 1)
    return jnp.where(lane < MLA_ROPE, r, 0.0), lane


def _mla_q_kernel(cq_ref, g_ref, w_ref, tab_ref, o_ref, cqn_ref, *, heads_per_step, scale):
    @pl.when(pl.program_id(1) == 0)
    def _():
        c = cq_ref[...].astype(F32)
        ms = jnp.mean(c * c, axis=-1, keepdims=True)
        cqn_ref[...] = (c * lax.rsqrt(ms + RMS_EPS) * g_ref[...]).astype(cqn_ref.dtype)

    y = _dot(cqn_ref[...], w_ref[...])
    tab = tab_ref[...]
    for hh in range(heads_per_step):
        base = hh * HEAD_TILE
        nope = y[:, base:base + MLA_NOPE] * scale
        r, lane = _rope_combine(y[:, base + MLA_NOPE:base + HEAD_TILE] * tab)
        r = jnp.where(lane == MLA_ROPE, 1.0, r * scale)
        o_ref[:, base:base + MLA_NOPE] = nope.astype(o_ref.dtype)
        o_ref[:, base + MLA_NOPE:base + HEAD_TILE] = r.astype(o_ref.dtype)


def _mla_q(z, g, w, tab, n_heads, q_lora, seq):
    m = z.shape[0]
    tm = _tile(m, 1024, 16)
    tm = math.gcd(tm, seq)
    hp = _tile(n_heads, 4, 1)
    nseq = seq // tm
    kern = functools.partial(_mla_q_kernel, heads_per_step=hp,
                             scale=float((MLA_NOPE + MLA_ROPE) ** -0.5))
    return pl.pallas_call(
        kern,
        out_shape=jax.ShapeDtypeStruct((m, n_heads * HEAD_TILE), BF16),
        grid=(m // tm, n_heads // hp),
        in_specs=[pl.BlockSpec((tm, q_lora), lambda i, j: (i, 0)),
                  pl.BlockSpec((1, q_lora), lambda i, j: (0, 0)),
                  pl.BlockSpec((q_lora, hp * HEAD_TILE), lambda i, j: (0, j)),
                  pl.BlockSpec((tm, LANES), lambda i, j: (i % nseq, 0))],
        out_specs=pl.BlockSpec((tm, hp * HEAD_TILE), lambda i, j: (i, j)),
        scratch_shapes=[pltpu.VMEM((tm, q_lora), BF16)],
        compiler_params=_params("arbitrary", "arbitrary"),
        name="mla_q",
    )(z, g, w, tab)


def _mla_kv_kernel(ckv_ref, kr_ref, g_ref, wk_ref, wvt_ref, tab_ref, k_ref, vt_ref, *, n_heads, t):
    c = ckv_ref[...].astype(F32)
    ms = jnp.mean(c * c, axis=-1, keepdims=True)
    cn = (c * lax.rsqrt(ms + RMS_EPS) * g_ref[...]).astype(BF16)
    kn = _dot(cn, wk_ref[...])
    kr, _ = _rope_combine(kr_ref[...] * tab_ref[...])
    kr = kr.astype(k_ref.dtype)
    for h in range(n_heads):
        k_ref[:, h * HEAD_TILE:h * HEAD_TILE + MLA_NOPE] = (
            kn[:, h * MLA_NOPE:(h + 1) * MLA_NOPE].astype(k_ref.dtype))
        k_ref[:, h * HEAD_TILE + MLA_NOPE:(h + 1) * HEAD_TILE] = kr
    vt = _dot_nt(wvt_ref[...], cn).astype(vt_ref.dtype)
    for tt in range(vt_ref.shape[1]):
        vt_ref[0, tt] = vt[:, tt * t:(tt + 1) * t]


def _mla_kv(z, kr, g, wk, wvt, tab, n_heads, kv_lora, ckv_block, batch, seq, t):
    m = z.shape[0]
    tm = math.gcd(_tile(m, 1024, 16), seq)
    t = min(t, tm)
    nseq = seq // tm
    kern = functools.partial(_mla_kv_kernel, n_heads=n_heads, t=t)
    return pl.pallas_call(
        kern,
        out_shape=(jax.ShapeDtypeStruct((m, n_heads * HEAD_TILE), BF16),
                   jax.ShapeDtypeStruct((batch, seq // t, n_heads * MLA_V_DIM, t), BF16)),
        grid=(m // tm,),
        in_specs=[pl.BlockSpec((tm, kv_lora), lambda i: (i, ckv_block)),
                  pl.BlockSpec((tm, LANES), lambda i: (i, 0)),
                  pl.BlockSpec((1, kv_lora), lambda i: (0, 0)),
                  pl.BlockSpec((kv_lora, n_heads * MLA_NOPE), lambda i: (0, 0)),
                  pl.BlockSpec((n_heads * MLA_V_DIM, kv_lora), lambda i: (0, 0)),
                  pl.BlockSpec((tm, LANES), lambda i: (i % nseq, 0))],
        out_specs=(pl.BlockSpec((tm, n_heads * HEAD_TILE), lambda i: (i, 0)),
                   pl.BlockSpec((1, tm // t, n_heads * MLA_V_DIM, t),
                                lambda i: (i // nseq, i % nseq, 0, 0))),
        compiler_params=_params("arbitrary"),
        name="mla_kv",
    )(z, kr, g, wk, wvt, tab)


def _softmax_block(s_t, vt_blk, m_ref, l_ref, acc_ref, idx):
    m_prev = m_ref[idx]
    m_new = jnp.maximum(m_prev, jnp.max(s_t, axis=0, keepdims=True))
    alpha = jnp.exp(m_prev - m_new)
    p = jnp.exp(s_t - m_new)
    l_ref[idx] = alpha * l_ref[idx] + jnp.sum(p, axis=0, keepdims=True)
    acc_ref[idx] = alpha * acc_ref[idx] + _dot(vt_blk, p.astype(BF16))
    m_ref[idx] = m_new


def _init_stats(m_ref, l_ref, acc_ref):
    m_ref[...] = jnp.full(m_ref.shape, M_INIT, F32)
    l_ref[...] = jnp.zeros(l_ref.shape, F32)
    acc_ref[...] = jnp.zeros(acc_ref.shape, F32)


def _silu(g):
    return g * jax.nn.sigmoid(g)


def _attention_sweep(n_maps, qk, v_blk, meta_scores, vtm, diag_add, sa_ref, sb_ref,
                     m_ref, l_ref, acc_ref, n_full, n_sub):
    maps = range(n_maps)
    for mm in maps:
        sa_ref[mm] = qk(mm, 0)
    for mm in maps:
        _softmax_block(meta_scores(mm), vtm, m_ref, l_ref, acc_ref, mm)

    def body(i, carry):
        j = 2 * i
        for mm in maps:
            sb_ref[mm] = qk(mm, j + 1)
        vb = v_blk(j)
        for mm in maps:
            _softmax_block(sa_ref[mm], vb, m_ref, l_ref, acc_ref, mm)
        for mm in maps:
            sa_ref[mm] = qk(mm, j + 2)
        vb = v_blk(j + 1)
        for mm in maps:
            _softmax_block(sb_ref[mm], vb, m_ref, l_ref, acc_ref, mm)
        return carry

    lax.fori_loop(0, n_full // 2, body, 0)

    for kd in range(n_sub):
        vb = v_blk(n_full + kd)
        for mm in maps:
            s_t = sa_ref[mm] if kd == 0 else qk(mm, n_full + kd)
            _softmax_block(s_t + diag_add(kd), vb, m_ref, l_ref, acc_ref, mm)


def _diff_attn_kernel(q_ref, k_ref, vt_ref, g_ref, kmeta_ref, vtmeta_ref, eq_ref, ek_ref,
                      ekmeta_ref, dadd_ref, lq1_ref, lk1_ref, lq2_ref, lk2_ref, subln_ref,
                      o_ref, kaug_ref, kmaug_ref, vtm_ref, sa_ref, sb_ref, m_ref, l_ref, acc_ref,
                      *, lambda_init, t, n_sub, n_meta):
    qi = pl.program_id(2)
    d = DIFF_HEAD_DIM

    @pl.when(qi == 0)
    def _build():
        kmaug_ref[...] = jnp.zeros(kmaug_ref.shape, BF16)
        vtm_ref[...] = jnp.zeros(vtm_ref.shape, BF16)
        vtm_ref[:, 0:n_meta] = vtmeta_ref[...]
        for mm in range(2):
            kaug_ref[mm, :, 0:d] = k_ref[0, :, mm * d:(mm + 1) * d]
            kaug_ref[mm, :, d:2 * d] = ek_ref[0]
            kmaug_ref[mm, 0:n_meta, 0:d] = kmeta_ref[:, mm * d:(mm + 1) * d]
            kmaug_ref[mm, :, d:2 * d] = ekmeta_ref[0]

    q = q_ref[0]
    eq = eq_ref[0]
    qa = [jnp.concatenate([q[:, mm * d:(mm + 1) * d], eq], axis=1) for mm in range(2)]
    _init_stats(m_ref, l_ref, acc_ref)

    def qk(mm, blk):
        off = pl.multiple_of(blk * t, t)
        return _dot_nt(kaug_ref[mm, pl.ds(off, t), :], qa[mm])

    _attention_sweep(2, qk, lambda blk: vt_ref[0, blk],
                     lambda mm: _dot_nt(kmaug_ref[mm], qa[mm]), vtm_ref[...],
                     lambda kd: dadd_ref[0, kd], sa_ref, sb_ref, m_ref, l_ref, acc_ref,
                     qi * n_sub, n_sub)

    lam = (jnp.exp(jnp.sum(lq1_ref[...] * lk1_ref[...], axis=-1, keepdims=True))
           - jnp.exp(jnp.sum(lq2_ref[...] * lk2_ref[...], axis=-1, keepdims=True))
           + lambda_init)
    o_t = acc_ref[0] / l_ref[0] - lam * (acc_ref[1] / l_ref[1])
    ms = jnp.mean(o_t * o_t, axis=0, keepdims=True)
    o = (o_t * lax.rsqrt(ms + RMS_EPS)).T
    o = (o * subln_ref[...]) * (1.0 - lambda_init)
    o_ref[0] = (o * _silu(g_ref[0].astype(F32))).astype(o_ref.dtype)


def _diff_attn(z3, vt, z_meta, vt_meta, eq_tab, ek_tab, ekmeta_tab, dadd_tab, lams, subln,
               n_heads, blk, lambda_init, t, tq):
    b, s, _ = z3.shape
    n_meta = z_meta.shape[0]
    nq = s // tq
    n_sub = tq // t
    q0, k0, g0 = blk
    vec = pl.BlockSpec((1, DIFF_HEAD_DIM), lambda bb, h, qi: (0, 0))
    kern = functools.partial(_diff_attn_kernel, lambda_init=lambda_init, t=t, n_sub=n_sub,
                             n_meta=n_meta)
    return pl.pallas_call(
        kern,
        out_shape=jax.ShapeDtypeStruct((b, s, n_heads * DIFF_V_DIM), BF16),
        grid=(b, n_heads, nq),
        in_specs=[
            pl.BlockSpec((1, tq, HEAD_TILE), lambda bb, h, qi: (bb, qi, q0 + h)),
            pl.BlockSpec((1, s, HEAD_TILE), lambda bb, h, qi: (bb, 0, k0 + h)),
            pl.BlockSpec((1, s // t, DIFF_V_DIM, t), lambda bb, h, qi: (bb, 0, h, 0)),
            pl.BlockSpec((1, tq, DIFF_V_DIM), lambda bb, h, qi: (bb, qi, g0 + h)),
            pl.BlockSpec((n_meta, HEAD_TILE), lambda bb, h, qi: (0, k0 + h)),
            pl.BlockSpec((DIFF_V_DIM, n_meta), lambda bb, h, qi: (h, 0)),
            pl.BlockSpec((1, tq, LANES), lambda bb, h, qi: (h, qi, 0)),
            pl.BlockSpec((1, s, LANES), lambda bb, h, qi: (h, 0, 0)),
            pl.BlockSpec((1, LANES, LANES), lambda bb, h, qi: (h, 0, 0)),
            pl.BlockSpec((1, n_sub, t, tq), lambda bb, h, qi: (h, 0, 0, 0)),
            vec, vec, vec, vec,
            pl.BlockSpec((1, DIFF_V_DIM), lambda bb, h, qi: (0, 0)),
        ],
        out_specs=pl.BlockSpec((1, tq, DIFF_V_DIM), lambda bb, h, qi: (bb, qi, h)),
        scratch_shapes=[
            pltpu.VMEM((2, s, HEAD_TILE), BF16),
            pltpu.VMEM((2, LANES, HEAD_TILE), BF16),
            pltpu.VMEM((DIFF_V_DIM, LANES), BF16),
            pltpu.VMEM((2, t, tq), F32),
            pltpu.VMEM((2, t, tq), F32),
            pltpu.VMEM((2, 1, tq), F32),
            pltpu.VMEM((2, 1, tq), F32),
            pltpu.VMEM((2, DIFF_V_DIM, tq), F32),
        ],
        compiler_params=_params("arbitrary", "arbitrary", "arbitrary"),
        name="diff_attn",
    )(z3, z3, vt, z3, z_meta, vt_meta, eq_tab, ek_tab, ekmeta_tab, dadd_tab, *lams, subln)


def _mla_attn_kernel(q_ref, k_ref, vt_ref, g_ref, kmeta_ref, vtmeta_ref, padk_ref, madd_ref,
                     o_ref, kmaug_ref, vtm_ref, sa_ref, sb_ref, m_ref, l_ref, acc_ref,
                     *, t, n_sub, n_meta):
    qi = pl.program_id(2)

    @pl.when(qi == 0)
    def _build():
        kmaug_ref[...] = padk_ref[...]
        kmaug_ref[0:n_meta, :] = kmeta_ref[...]
        vtm_ref[...] = jnp.zeros(vtm_ref.shape, BF16)
        vtm_ref[:, 0:n_meta] = vtmeta_ref[...]

    q = q_ref[0]
    _init_stats(m_ref, l_ref, acc_ref)

    def qk(mm, blk):
        off = pl.multiple_of(blk * t, t)
        return _dot_nt(k_ref[0, pl.ds(off, t), :], q)

    _attention_sweep(1, qk, lambda blk: vt_ref[0, blk],
                     lambda mm: _dot_nt(kmaug_ref[...], q), vtm_ref[...],
                     lambda kd: madd_ref[kd], sa_ref, sb_ref, m_ref, l_ref, acc_ref,
                     qi * n_sub, n_sub)

    o = (acc_ref[0] / l_ref[0]).T
    o_ref[0] = (o * _silu(g_ref[0].astype(F32))).astype(o_ref.dtype)


def _mla_attn(q3, k3, vt, z3, k_meta, vt_meta, padk, madd, n_heads, g0, t, tq):
    b, s, _ = q3.shape
    n_meta = k_meta.shape[0]
    nq = s // tq
    n_sub = tq // t
    kern = functools.partial(_mla_attn_kernel, t=t, n_sub=n_sub, n_meta=n_meta)
    return pl.pallas_call(
        kern,
        out_shape=jax.ShapeDtypeStruct((b, s, n_heads * MLA_V_DIM), BF16),
        grid=(b, n_heads, nq),
        in_specs=[
            pl.BlockSpec((1, tq, HEAD_TILE), lambda bb, h, qi: (bb, qi, h)),
            pl.BlockSpec((1, s, HEAD_TILE), lambda bb, h, qi: (bb, 0, h)),
            pl.BlockSpec((1, s // t, MLA_V_DIM, t), lambda bb, h, qi: (bb, 0, h, 0)),
            pl.BlockSpec((1, tq, MLA_V_DIM), lambda bb, h, qi: (bb, qi, g0 + h)),
            pl.BlockSpec((n_meta, HEAD_TILE), lambda bb, h, qi: (0, h)),
            pl.BlockSpec((MLA_V_DIM, n_meta), lambda bb, h, qi: (h, 0)),
            pl.BlockSpec((LANES, HEAD_TILE), lambda bb, h, qi: (0, 0)),
            pl.BlockSpec((n_sub, t, tq), lambda bb, h, qi: (0, 0, 0)),
        ],
        out_specs=pl.BlockSpec((1, tq, MLA_V_DIM), lambda bb, h, qi: (bb, qi, h)),
        scratch_shapes=[
            pltpu.VMEM((LANES, HEAD_TILE), BF16),
            pltpu.VMEM((MLA_V_DIM, LANES), BF16),
            pltpu.VMEM((1, t, tq), F32),
            pltpu.VMEM((1, t, tq), F32),
            pltpu.VMEM((1, 1, tq), F32),
            pltpu.VMEM((1, 1, tq), F32),
            pltpu.VMEM((1, MLA_V_DIM, tq), F32),
        ],
        compiler_params=_params("arbitrary", "arbitrary", "arbitrary"),
        name="mla_attn",
    )(q3, k3, vt, z3, k_meta, vt_meta, padk, madd)


def _outproj_kernel(a_ref, b_ref, wa_ref, wb_ref, y_ref):
    y_ref[...] = _dot(a_ref[...], wa_ref[...]) + _dot(b_ref[...], wb_ref[...])


def _outproj(mix_a, mix_b, wa, wb):
    m, ka = mix_a.shape
    kb = mix_b.shape[1]
    n = wa.shape[1]
    tm = _tile(m, 1024, 16)
    tn = _tile(n, 1024)
    return pl.pallas_call(
        _outproj_kernel,
        out_shape=jax.ShapeDtypeStruct((m, n), F32),
        grid=(m // tm, n // tn),
        in_specs=[pl.BlockSpec((tm, ka), lambda i, j: (i, 0)),
                  pl.BlockSpec((tm, kb), lambda i, j: (i, 0)),
                  pl.BlockSpec((ka, tn), lambda i, j: (0, j)),
                  pl.BlockSpec((kb, tn), lambda i, j: (0, j))],
        out_specs=pl.BlockSpec((tm, tn), lambda i, j: (i, j)),
        compiler_params=_params("arbitrary", "arbitrary"),
        name="outproj",
    )(mix_a, mix_b, wa, wb)


def _postnorm_kernel(x_ref, y_ref, g_ref, o_ref):
    y = y_ref[...]
    ms = jnp.mean(y * y, axis=-1, keepdims=True)
    o_ref[...] = x_ref[...] + y * lax.rsqrt(ms + RMS_EPS) * g_ref[...]


def _postnorm(x2d, y, g):
    m, d = x2d.shape
    tm = _tile(m, 256, 8)
    return pl.pallas_call(
        _postnorm_kernel,
        out_shape=jax.ShapeDtypeStruct((m, d), F32),
        grid=(m // tm,),
        in_specs=[pl.BlockSpec((tm, d), lambda i: (i, 0)),
                  pl.BlockSpec((tm, d), lambda i: (i, 0)),
                  pl.BlockSpec((1, d), lambda i: (0, 0))],
        out_specs=pl.BlockSpec((tm, d), lambda i: (i, 0)),
        compiler_params=_params("arbitrary"),
        name="postnorm",
    )(x2d, y, g)


def _split3(v):
    v = np.asarray(v, np.float32)
    hi = v.astype(NP_BF16)
    r = v - hi.astype(np.float32)
    mid = r.astype(NP_BF16)
    lo = (r - mid.astype(np.float32)).astype(NP_BF16)
    return hi, mid, lo


def _rope_table(pos):
    inv_freq = 1.0 / (ROPE_THETA ** (jnp.arange(0, MLA_ROPE, 2, dtype=F32) / MLA_ROPE))
    ang = pos.astype(F32)[:, None] * inv_freq[None, :]
    cos, sin = jnp.cos(ang), jnp.sin(ang)
    return jnp.concatenate([cos, cos, sin, sin], axis=1)


def _alibi_tables(n_heads, n_meta, seq, t, tq):
    f32 = np.float32
    slopes = (2.0 ** (-8.0 * np.arange(1, n_heads + 1, dtype=f32) / n_heads)).astype(f32)
    pos_q = ((np.arange(seq, dtype=f32) + n_meta)[None, :] * slopes[:, None]).astype(f32)
    pos_m = (np.arange(LANES, dtype=f32)[None, :] * slopes[:, None]).astype(f32)
    ones_q = np.ones((n_heads, seq), NP_BF16)
    zeros_q = np.zeros((n_heads, seq), NP_BF16)
    qh, qm, ql = _split3(-pos_q)
    kh, km, kl = _split3(pos_q)
    pad = np.zeros((n_heads, seq, LANES - 7), NP_BF16)
    eq = np.concatenate([np.stack([qh, qm, ql, ones_q, ones_q, ones_q, ones_q], -1), pad], -1)
    ek = np.concatenate([np.stack([ones_q, ones_q, ones_q, zeros_q, kh, km, kl], -1), pad], -1)
    mh, mmid, ml = _split3(pos_m)
    valid = (np.arange(LANES) < n_meta)[None, :]
    padmask = np.broadcast_to(np.where(valid, 0.0, MASK_VALUE).astype(f32),
                              (n_heads, LANES)).astype(NP_BF16)
    ones_m = np.ones((n_heads, LANES), NP_BF16)
    ekm = np.concatenate(
        [np.stack([ones_m, ones_m, ones_m, padmask, mh, mmid, ml], -1),
         np.zeros((n_heads, LANES, LANES - 7), NP_BF16)], -1)
    j = np.arange(tq).reshape(tq // t, t, 1)
    i = np.arange(tq)[None, None, :]
    visible = (j // CHUNK) <= (i // CHUNK)
    fut = (2.0 * np.maximum(j - i, 0)).astype(f32)
    dadd = np.where(visible[None], -slopes[:, None, None, None] * fut[None], MASK_VALUE).astype(f32)
    madd = np.where(visible, 0.0, MASK_VALUE).astype(f32)
    return tuple(jnp.asarray(a) for a in (eq, ek, ekm, dadd, madd))


def kernel(x, meta_tokens, norm_pre, w_in, diff_lambda_q1, diff_lambda_k1, diff_lambda_q2,
           diff_lambda_k2, diff_subln, mla_norm_q, mla_norm_kv, w_uq, w_ukv, w_out, norm_post):
    assert norm_pre.shape[0] == 1, "single-layer block"
    b, s, d = x.shape
    n_meta = meta_tokens.shape[0]
    mix = w_out.shape[1]
    diff_w = mix // 2
    mla_w = mix - diff_w
    ha = diff_w // DIFF_V_DIM
    hb = mla_w // MLA_V_DIM
    q_lora = mla_norm_q.shape[-1]
    kv_lora = mla_norm_kv.shape[-1]
    t = min(KEY_TILE, s)
    tq = min(QUERY_TILE, s)
    assert s % tq == 0 and tq % (2 * t) == 0 and t % CHUNK == 0 and n_meta <= 16
    lambda_init = 0.8 - 0.6 * math.exp(-0.3 * 0)

    w = w_in[0]
    sizes = (diff_w, diff_w, diff_w, diff_w, q_lora, kv_lora, MLA_ROPE, mla_w)
    offs = np.cumsum((0,) + sizes)
    w_dq, w_dk, w_dv, w_dg, w_cq, w_ckv, w_kr, w_mg = (w[:, offs[i]:offs[i + 1]] for i in range(8))
    w_main = jnp.concatenate([w_cq, w_ckv, w_dq, w_dk, w_dg, w_mg], axis=1).astype(BF16)
    n_main = w_main.shape[1]
    assert q_lora % kv_lora == 0 and (q_lora + kv_lora) % HEAD_TILE == 0
    q_blk0 = (q_lora + kv_lora) // HEAD_TILE
    k_blk0 = q_blk0 + diff_w // HEAD_TILE
    g_blk0 = k_blk0 + diff_w // HEAD_TILE
    mg_blk0 = (q_lora + kv_lora + 3 * diff_w) // MLA_V_DIM
    colscale = jnp.ones((1, n_main), F32).at[:, q_lora + kv_lora:q_lora + kv_lora + diff_w].set(
        DIFF_HEAD_DIM ** -0.5)
    half = MLA_ROPE // 2

    def rot(wr):
        return jnp.concatenate([-wr[:, half:], wr[:, :half]], axis=1)

    w_kr2 = jnp.concatenate([w_kr, rot(w_kr)], axis=1).astype(BF16)
    w_dvt = w_dv.T.astype(BF16)

    wq = w_uq[0].reshape(q_lora, hb, MLA_NOPE + MLA_ROPE)
    wq_r = wq[:, :, MLA_NOPE:]
    wq_full = jnp.concatenate(
        [wq[:, :, :MLA_NOPE], wq_r, jnp.concatenate([-wq_r[:, :, half:], wq_r[:, :, :half]], -1)],
        axis=-1).reshape(q_lora, hb * HEAD_TILE).astype(BF16)
    wkv = w_ukv[0].reshape(kv_lora, hb, MLA_NOPE + MLA_V_DIM)
    w_kn = wkv[:, :, :MLA_NOPE].reshape(kv_lora, hb * MLA_NOPE).astype(BF16)
    w_vt = wkv[:, :, MLA_NOPE:].reshape(kv_lora, hb * MLA_V_DIM).T.astype(BF16)
    wo = w_out[0].astype(BF16)
    wo_a, wo_b = wo[:diff_w], wo[diff_w:]

    pos = jnp.arange(n_meta + s, dtype=jnp.int32)
    rope_tab = _rope_table(pos)
    tab_meta, tab_seq = rope_tab[:n_meta], rope_tab[n_meta:]
    eq_tab, ek_tab, ekm_tab, dadd_tab, madd_tab = _alibi_tables(ha, n_meta, s, t, tq)
    lane = jnp.arange(HEAD_TILE)
    row = jnp.arange(LANES)
    padk = jnp.where((lane[None, :] == MLA_NOPE + MLA_ROPE) & (row[:, None] >= n_meta),
                     MASK_VALUE, 0.0).astype(BF16)

    x2d = x.reshape(b * s, d)
    u = _prenorm(x2d, norm_pre)
    u_meta = _prenorm(meta_tokens.astype(x.dtype), norm_pre)
    z, kr = _inproj(u, w_main, colscale, w_kr2)
    z_meta, kr_meta = _inproj(u_meta, w_main, colscale, w_kr2)
    vt_a = _v_transposed(w_dvt, u, b, t)
    vt_a_meta = _v_transposed(w_dvt, u_meta, 1, n_meta)[0, 0]

    g_cq = mla_norm_q.astype(F32)
    g_ckv = mla_norm_kv.astype(F32)
    ckv_block = q_lora // kv_lora
    q_b = _mla_q(z, g_cq, wq_full, tab_seq, hb, q_lora, s)
    k_b, vt_b = _mla_kv(z, kr, g_ckv, w_kn, w_vt, tab_seq, hb, kv_lora, ckv_block, b, s, t)
    k_b_meta, vt_b_meta = _mla_kv(z_meta, kr_meta, g_ckv, w_kn, w_vt, tab_meta, hb, kv_lora,
                                  ckv_block, 1, n_meta, n_meta)
    vt_b_meta = vt_b_meta[0, 0]

    z3 = z.reshape(b, s, n_main)
    lams = [v.astype(F32) for v in (diff_lambda_q1, diff_lambda_k1, diff_lambda_q2, diff_lambda_k2)]
    mix_a = _diff_attn(z3, vt_a, z_meta, vt_a_meta, eq_tab, ek_tab, ekm_tab, dadd_tab, lams,
                       diff_subln.astype(F32), ha, (q_blk0, k_blk0, g_blk0), lambda_init, t, tq)
    mix_b = _mla_attn(q_b.reshape(b, s, hb * HEAD_TILE), k_b.reshape(b, s, hb * HEAD_TILE), vt_b,
                      z3, k_b_meta, vt_b_meta, padk, madd_tab, hb, mg_blk0, t, tq)

    y = _outproj(mix_a.reshape(b * s, diff_w), mix_b.reshape(b * s, mla_w), wo_a, wo_b)
    out = _postnorm(x2d, y, norm_post.astype(F32))
    return out.reshape(b, s, d)
```

```python
import functools
import math

import jax
import jax.numpy as jnp
import numpy as np
from jax import lax
from jax.experimental import pallas as pl
from jax.experimental.pallas import tpu as pltpu

CHUNK = 64
RMS_EPS = 1e-6
MASK_VALUE = -1e30
DIFF_HEAD_DIM = 128
DIFF_V_DIM = 2 * DIFF_HEAD_DIM
MLA_NOPE = 128
MLA_ROPE = 64
MLA_V_DIM = 128
ROPE_THETA = 10000.0

LANES = 128
HEAD_TILE = 2 * LANES
VMEM_LIMIT = 56 * 1024 * 1024

KEY_TILE = 512
QUERY_TILE = 1024
M_INIT = -1e38
ONES_ROWS = 16
LOG2E = math.log2(math.e)

F32 = jnp.float32
BF16 = jnp.bfloat16
NP_BF16 = np.dtype(jnp.bfloat16)


def _params(*sem):
    return pltpu.CompilerParams(dimension_semantics=sem, vmem_limit_bytes=VMEM_LIMIT)


def _tile(dim, target, mult=LANES):
    if dim <= target:
        return dim
    t = (target // mult) * mult
    while t > mult and dim % t:
        t -= mult
    assert dim % t == 0, (dim, target)
    return t


def _dot(a, b):
    return jnp.dot(a, b, preferred_element_type=F32)


def _dot_nt(a, b):
    return lax.dot_general(a, b, (((1,), (1,)), ((), ())), preferred_element_type=F32)


def _prenorm_kernel(x_ref, g_ref, o_ref):
    x = x_ref[...]
    ms = jnp.mean(x * x, axis=-1, keepdims=True)
    o_ref[...] = (x * lax.rsqrt(ms + RMS_EPS) * g_ref[...]).astype(o_ref.dtype)


def _prenorm(x2d, g):
    m, d = x2d.shape
    tm = _tile(m, 256, 8)
    return pl.pallas_call(
        _prenorm_kernel,
        out_shape=jax.ShapeDtypeStruct((m, d), BF16),
        grid=(m // tm,),
        in_specs=[pl.BlockSpec((tm, d), lambda i: (i, 0)),
                  pl.BlockSpec((1, d), lambda i: (0, 0))],
        out_specs=pl.BlockSpec((tm, d), lambda i: (i, 0)),
        compiler_params=_params("arbitrary"),
        name="prenorm",
    )(x2d, g)


def _inproj_kernel(u_ref, w_ref, s_ref, wkr_ref, z_ref, kr_ref):
    u = u_ref[...]
    z_ref[...] = (_dot(u, w_ref[...]) * s_ref[...]).astype(z_ref.dtype)

    @pl.when(pl.program_id(1) == 0)
    def _():
        kr_ref[...] = _dot(u, wkr_ref[...])


def _inproj(u, w, colscale, wkr):
    m, d = u.shape
    n = w.shape[1]
    tm = _tile(m, 1024, 16)
    tn = _tile(n, 1024)
    return pl.pallas_call(
        _inproj_kernel,
        out_shape=(jax.ShapeDtypeStruct((m, n), BF16),
                   jax.ShapeDtypeStruct((m, LANES), F32)),
        grid=(m // tm, n // tn),
        in_specs=[pl.BlockSpec((tm, d), lambda i, j: (i, 0)),
                  pl.BlockSpec((d, tn), lambda i, j: (0, j)),
                  pl.BlockSpec((1, tn), lambda i, j: (0, j)),
                  pl.BlockSpec((d, LANES), lambda i, j: (0, 0))],
        out_specs=(pl.BlockSpec((tm, tn), lambda i, j: (i, j)),
                   pl.BlockSpec((tm, LANES), lambda i, j: (i, 0))),
        compiler_params=_params("arbitrary", "arbitrary"),
        name="inproj",
    )(u, w, colscale, wkr)


def _vt_kernel(w_ref, u_ref, o_ref):
    o_ref[0, 0] = _dot_nt(w_ref[...], u_ref[...]).astype(o_ref.dtype)


def _v_transposed(w_t, u, batch, t):
    n, d = w_t.shape
    m = u.shape[0]
    nt = m // (batch * t)
    return pl.pallas_call(
        _vt_kernel,
        out_shape=jax.ShapeDtypeStruct((batch, nt, n, t), BF16),
        grid=(m // t,),
        in_specs=[pl.BlockSpec((n, d), lambda i: (0, 0)),
                  pl.BlockSpec((t, d), lambda i: (i, 0))],
        out_specs=pl.BlockSpec((1, 1, n, t), lambda i: (i // nt, i % nt, 0, 0)),
        compiler_params=_params("arbitrary"),
        name="inproj_vt",
    )(w_t, u)


def _rope_combine(t):
    r = t + pltpu.roll(t, MLA_ROPE, axis=1)
    lane = lax.broadcasted_iota(jnp.int32, r.shape, 1)
    return jnp.where(lane < MLA_ROPE, r, 0.0), lane


def _mla_q_kernel(cq_ref, g_ref, w_ref, tab_ref, o_ref, cqn_ref, *, heads_per_step, scale):
    @pl.when(pl.program_id(1) == 0)
    def _():
        c = cq_ref[...].astype(F32)
        ms = jnp.mean(c * c, axis=-1, keepdims=True)
        cqn_ref[...] = (c * lax.rsqrt(ms + RMS_EPS) * g_ref[...]).astype(cqn_ref.dtype)

    y = _dot(cqn_ref[...], w_ref[...])
    tab = tab_ref[...]
    for hh in range(heads_per_step):
        base = hh * HEAD_TILE
        nope = y[:, base:base + MLA_NOPE] * scale
        r, lane = _rope_combine(y[:, base + MLA_NOPE:base + HEAD_TILE] * tab)
        r = jnp.where(lane == MLA_ROPE, 1.0, r * scale)
        o_ref[:, base:base + MLA_NOPE] = nope.astype(o_ref.dtype)
        o_ref[:, base + MLA_NOPE:base + HEAD_TILE] = r.astype(o_ref.dtype)


def _mla_q(z, g, w, tab, n_heads, q_lora, seq):
    m = z.shape[0]
    tm = _tile(m, 1024, 16)
    tm = math.gcd(tm, seq)
    hp = _tile(n_heads, 4, 1)
    nseq = seq // tm
    kern = functools.partial(_mla_q_kernel, heads_per_step=hp,
                             scale=float((MLA_NOPE + MLA_ROPE) ** -0.5 * LOG2E))
    return pl.pallas_call(
        kern,
        out_shape=jax.ShapeDtypeStruct((m, n_heads * HEAD_TILE), BF16),
        grid=(m // tm, n_heads // hp),
        in_specs=[pl.BlockSpec((tm, q_lora), lambda i, j: (i, 0)),
                  pl.BlockSpec((1, q_lora), lambda i, j: (0, 0)),
                  pl.BlockSpec((q_lora, hp * HEAD_TILE), lambda i, j: (0, j)),
                  pl.BlockSpec((tm, LANES), lambda i, j: (i % nseq, 0))],
        out_specs=pl.BlockSpec((tm, hp * HEAD_TILE), lambda i, j: (i, j)),
        scratch_shapes=[pltpu.VMEM((tm, q_lora), BF16)],
        compiler_params=_params("arbitrary", "arbitrary"),
        name="mla_q",
    )(z, g, w, tab)


def _mla_kv_kernel(ckv_ref, kr_ref, g_ref, wk_ref, wvt_ref, tab_ref, k_ref, vt_ref, *, n_heads, t):
    c = ckv_ref[...].astype(F32)
    ms = jnp.mean(c * c, axis=-1, keepdims=True)
    cn = (c * lax.rsqrt(ms + RMS_EPS) * g_ref[...]).astype(BF16)
    kn = _dot(cn, wk_ref[...])
    kr, _ = _rope_combine(kr_ref[...] * tab_ref[...])
    kr = kr.astype(k_ref.dtype)
    for h in range(n_heads):
        k_ref[:, h * HEAD_TILE:h * HEAD_TILE + MLA_NOPE] = (
            kn[:, h * MLA_NOPE:(h + 1) * MLA_NOPE].astype(k_ref.dtype))
        k_ref[:, h * HEAD_TILE + MLA_NOPE:(h + 1) * HEAD_TILE] = kr
    vt = _dot_nt(wvt_ref[...], cn).astype(vt_ref.dtype)
    for tt in range(vt_ref.shape[1]):
        vt_ref[0, tt] = vt[:, tt * t:(tt + 1) * t]


def _mla_kv(z, kr, g, wk, wvt, tab, n_heads, kv_lora, ckv_block, batch, seq, t):
    m = z.shape[0]
    tm = math.gcd(_tile(m, 1024, 16), seq)
    t = min(t, tm)
    nseq = seq // tm
    kern = functools.partial(_mla_kv_kernel, n_heads=n_heads, t=t)
    return pl.pallas_call(
        kern,
        out_shape=(jax.ShapeDtypeStruct((m, n_heads * HEAD_TILE), BF16),
                   jax.ShapeDtypeStruct((batch, seq // t, n_heads * MLA_V_DIM, t), BF16)),
        grid=(m // tm,),
        in_specs=[pl.BlockSpec((tm, kv_lora), lambda i: (i, ckv_block)),
                  pl.BlockSpec((tm, LANES), lambda i: (i, 0)),
                  pl.BlockSpec((1, kv_lora), lambda i: (0, 0)),
                  pl.BlockSpec((kv_lora, n_heads * MLA_NOPE), lambda i: (0, 0)),
                  pl.BlockSpec((n_heads * MLA_V_DIM, kv_lora), lambda i: (0, 0)),
                  pl.BlockSpec((tm, LANES), lambda i: (i % nseq, 0))],
        out_specs=(pl.BlockSpec((tm, n_heads * HEAD_TILE), lambda i: (i, 0)),
                   pl.BlockSpec((1, tm // t, n_heads * MLA_V_DIM, t),
                                lambda i: (i // nseq, i % nseq, 0, 0))),
        compiler_params=_params("arbitrary"),
        name="mla_kv",
    )(z, kr, g, wk, wvt, tab)


def _col_max(s_t):
    return jnp.max(s_t, axis=0, keepdims=True)


def _score_stage(s_ref, mx_ref, mm, s_t):
    s_ref[mm] = s_t
    mx_ref[mm] = _col_max(s_t)


def _value_stage(s_t, mx, vt_aug, m_prev, acc_ref, mm):
    m_new = jnp.maximum(m_prev, mx)
    alpha = jnp.exp2(m_prev - m_new)
    p = jnp.exp2(s_t - m_new).astype(BF16)
    acc_ref[mm] = alpha * acc_ref[mm] + _dot(vt_aug, p)
    return m_new


def _silu(g):
    return g * jax.nn.sigmoid(g)


def _build_values(vaug_ref, vtm_ref, vt_ref, vtmeta_ref, dv, n_meta):
    vaug_ref[:, 0:dv, :] = vt_ref[0]
    vaug_ref[:, dv:, :] = jnp.ones((vaug_ref.shape[0], ONES_ROWS, vaug_ref.shape[2]), BF16)
    vtm_ref[0:dv, :] = jnp.zeros((dv, vtm_ref.shape[1]), BF16)
    vtm_ref[0:dv, 0:n_meta] = vtmeta_ref[...]
    vtm_ref[dv:, :] = jnp.ones((ONES_ROWS, vtm_ref.shape[1]), BF16)


def _attention_sweep(n_maps, qk, v_blk, meta_scores, vtm, diag_add, sa_ref, sb_ref,
                     mxa_ref, mxb_ref, acc_ref, n_full, n_sub):
    maps = range(n_maps)
    acc_ref[...] = jnp.zeros(acc_ref.shape, F32)
    for mm in maps:
        _score_stage(sa_ref, mxa_ref, mm, qk(mm, 0))
    m = []
    for mm in maps:
        s_t = meta_scores(mm)
        m.append(_value_stage(s_t, _col_max(s_t), vtm, jnp.full((1, s_t.shape[1]), M_INIT, F32),
                              acc_ref, mm))

    def body(i, m):
        j = 2 * i
        for mm in maps:
            _score_stage(sb_ref, mxb_ref, mm, qk(mm, j + 1))
        vb = v_blk(j)
        m = [_value_stage(sa_ref[mm], mxa_ref[mm], vb, m[mm], acc_ref, mm) for mm in maps]
        for mm in maps:
            _score_stage(sa_ref, mxa_ref, mm, qk(mm, j + 2))
        vb = v_blk(j + 1)
        return [_value_stage(sb_ref[mm], mxb_ref[mm], vb, m[mm], acc_ref, mm) for mm in maps]

    m = lax.fori_loop(0, n_full // 2, body, m)

    for kd in range(n_sub):
        vb = v_blk(n_full + kd)
        for mm in maps:
            s_t = (sa_ref[mm] if kd == 0 else qk(mm, n_full + kd)) + diag_add(kd)
            m[mm] = _value_stage(s_t, _col_max(s_t), vb, m[mm], acc_ref, mm)


def _diff_attn_kernel(q_ref, k_ref, vt_ref, g_ref, kmeta_ref, vtmeta_ref, eq_ref, ek_ref,
                      ekmeta_ref, dadd_ref, lq1_ref, lk1_ref, lq2_ref, lk2_ref, subln_ref,
                      o_ref, kaug_ref, kmaug_ref, vaug_ref, vtm_ref, sa_ref, sb_ref, mxa_ref, mxb_ref,
                      acc_ref, *, lambda_init, t, n_sub, n_meta):
    qi = pl.program_id(2)
    d = DIFF_HEAD_DIM
    dv = DIFF_V_DIM

    @pl.when(qi == 0)
    def _build():
        kmaug_ref[...] = jnp.zeros(kmaug_ref.shape, BF16)
        _build_values(vaug_ref, vtm_ref, vt_ref, vtmeta_ref, dv, n_meta)
        for mm in range(2):
            kaug_ref[mm, :, 0:d] = k_ref[0, :, mm * d:(mm + 1) * d]
            kaug_ref[mm, :, d:2 * d] = ek_ref[0]
            kmaug_ref[mm, 0:n_meta, 0:d] = kmeta_ref[:, mm * d:(mm + 1) * d]
            kmaug_ref[mm, :, d:2 * d] = ekmeta_ref[0]

    q = q_ref[0]
    eq = eq_ref[0]
    qa = [jnp.concatenate([q[:, mm * d:(mm + 1) * d], eq], axis=1) for mm in range(2)]

    def qk(mm, blk):
        off = pl.multiple_of(blk * t, t)
        return _dot_nt(kaug_ref[mm, pl.ds(off, t), :], qa[mm])

    _attention_sweep(2, qk, lambda blk: vaug_ref[blk],
                     lambda mm: _dot_nt(kmaug_ref[mm], qa[mm]), vtm_ref[...],
                     lambda kd: dadd_ref[0, kd], sa_ref, sb_ref, mxa_ref, mxb_ref, acc_ref,
                     qi * n_sub, n_sub)

    lam = (jnp.exp(jnp.sum(lq1_ref[...] * lk1_ref[...], axis=-1, keepdims=True))
           - jnp.exp(jnp.sum(lq2_ref[...] * lk2_ref[...], axis=-1, keepdims=True))
           + lambda_init)
    o_t = (acc_ref[0, 0:dv] / acc_ref[0, dv:dv + 1]
           - lam * (acc_ref[1, 0:dv] / acc_ref[1, dv:dv + 1]))
    ms = jnp.mean(o_t * o_t, axis=0, keepdims=True)
    o = (o_t * lax.rsqrt(ms + RMS_EPS)).T
    o = (o * subln_ref[...]) * (1.0 - lambda_init)
    o_ref[0] = (o * _silu(g_ref[0].astype(F32))).astype(o_ref.dtype)


def _diff_attn(z3, vt, z_meta, vt_meta, eq_tab, ek_tab, ekmeta_tab, dadd_tab, lams, subln,
               n_heads, blk, lambda_init, t, tq):
    b, s, _ = z3.shape
    n_meta = z_meta.shape[0]
    nq = s // tq
    n_sub = tq // t
    q0, k0, g0 = blk
    vec = pl.BlockSpec((1, DIFF_HEAD_DIM), lambda bb, h, qi: (0, 0))
    kern = functools.partial(_diff_attn_kernel, lambda_init=lambda_init, t=t, n_sub=n_sub,
                             n_meta=n_meta)
    return pl.pallas_call(
        kern,
        out_shape=jax.ShapeDtypeStruct((b, s, n_heads * DIFF_V_DIM), BF16),
        grid=(b, n_heads, nq),
        in_specs=[
            pl.BlockSpec((1, tq, HEAD_TILE), lambda bb, h, qi: (bb, qi, q0 + h)),
            pl.BlockSpec((1, s, HEAD_TILE), lambda bb, h, qi: (bb, 0, k0 + h)),
            pl.BlockSpec((1, s // t, DIFF_V_DIM, t), lambda bb, h, qi: (bb, 0, h, 0)),
            pl.BlockSpec((1, tq, DIFF_V_DIM), lambda bb, h, qi: (bb, qi, g0 + h)),
            pl.BlockSpec((n_meta, HEAD_TILE), lambda bb, h, qi: (0, k0 + h)),
            pl.BlockSpec((DIFF_V_DIM, n_meta), lambda bb, h, qi: (h, 0)),
            pl.BlockSpec((1, tq, LANES), lambda bb, h, qi: (h, qi, 0)),
            pl.BlockSpec((1, s, LANES), lambda bb, h, qi: (h, 0, 0)),
            pl.BlockSpec((1, LANES, LANES), lambda bb, h, qi: (h, 0, 0)),
            pl.BlockSpec((1, n_sub, t, tq), lambda bb, h, qi: (h, 0, 0, 0)),
            vec, vec, vec, vec,
            pl.BlockSpec((1, DIFF_V_DIM), lambda bb, h, qi: (0, 0)),
        ],
        out_specs=pl.BlockSpec((1, tq, DIFF_V_DIM), lambda bb, h, qi: (bb, qi, h)),
        scratch_shapes=[
            pltpu.VMEM((2, s, HEAD_TILE), BF16),
            pltpu.VMEM((2, LANES, HEAD_TILE), BF16),
            pltpu.VMEM((s // t, DIFF_V_DIM + ONES_ROWS, t), BF16),
            pltpu.VMEM((DIFF_V_DIM + ONES_ROWS, LANES), BF16),
            pltpu.VMEM((2, t, tq), F32),
            pltpu.VMEM((2, t, tq), F32),
            pltpu.VMEM((2, 1, tq), F32),
            pltpu.VMEM((2, 1, tq), F32),
            pltpu.VMEM((2, DIFF_V_DIM + ONES_ROWS, tq), F32),
        ],
        compiler_params=_params("arbitrary", "arbitrary", "arbitrary"),
        name="diff_attn",
    )(z3, z3, vt, z3, z_meta, vt_meta, eq_tab, ek_tab, ekmeta_tab, dadd_tab, *lams, subln)


def _mla_attn_kernel(q_ref, k_ref, vt_ref, g_ref, kmeta_ref, vtmeta_ref, padk_ref, madd_ref,
                     o_ref, kmaug_ref, vaug_ref, vtm_ref, sa_ref, sb_ref, mxa_ref, mxb_ref, acc_ref,
                     *, t, n_sub, n_meta):
    qi = pl.program_id(2)
    dv = MLA_V_DIM

    @pl.when(qi == 0)
    def _build():
        kmaug_ref[...] = padk_ref[...]
        kmaug_ref[0:n_meta, :] = kmeta_ref[...]
        _build_values(vaug_ref, vtm_ref, vt_ref, vtmeta_ref, dv, n_meta)

    q = q_ref[0]

    def qk(mm, blk):
        off = pl.multiple_of(blk * t, t)
        return _dot_nt(k_ref[0, pl.ds(off, t), :], q)

    _attention_sweep(1, qk, lambda blk: vaug_ref[blk],
                     lambda mm: _dot_nt(kmaug_ref[...], q), vtm_ref[...],
                     lambda kd: madd_ref[kd], sa_ref, sb_ref, mxa_ref, mxb_ref, acc_ref,
                     qi * n_sub, n_sub)

    o = (acc_ref[0, 0:dv] / acc_ref[0, dv:dv + 1]).T
    o_ref[0] = (o * _silu(g_ref[0].astype(F32))).astype(o_ref.dtype)


def _mla_attn(q3, k3, vt, z3, k_meta, vt_meta, padk, madd, n_heads, g0, t, tq):
    b, s, _ = q3.shape
    n_meta = k_meta.shape[0]
    nq = s // tq
    n_sub = tq // t
    kern = functools.partial(_mla_attn_kernel, t=t, n_sub=n_sub, n_meta=n_meta)
    return pl.pallas_call(
        kern,
        out_shape=jax.ShapeDtypeStruct((b, s, n_heads * MLA_V_DIM), BF16),
        grid=(b, n_heads, nq),
        in_specs=[
            pl.BlockSpec((1, tq, HEAD_TILE), lambda bb, h, qi: (bb, qi, h)),
            pl.BlockSpec((1, s, HEAD_TILE), lambda bb, h, qi: (bb, 0, h)),
            pl.BlockSpec((1, s // t, MLA_V_DIM, t), lambda bb, h, qi: (bb, 0, h, 0)),
            pl.BlockSpec((1, tq, MLA_V_DIM), lambda bb, h, qi: (bb, qi, g0 + h)),
            pl.BlockSpec((n_meta, HEAD_TILE), lambda bb, h, qi: (0, h)),
            pl.BlockSpec((MLA_V_DIM, n_meta), lambda bb, h, qi: (h, 0)),
            pl.BlockSpec((LANES, HEAD_TILE), lambda bb, h, qi: (0, 0)),
            pl.BlockSpec((n_sub, t, tq), lambda bb, h, qi: (0, 0, 0)),
        ],
        out_specs=pl.BlockSpec((1, tq, MLA_V_DIM), lambda bb, h, qi: (bb, qi, h)),
        scratch_shapes=[
            pltpu.VMEM((LANES, HEAD_TILE), BF16),
            pltpu.VMEM((s // t, MLA_V_DIM + ONES_ROWS, t), BF16),
            pltpu.VMEM((MLA_V_DIM + ONES_ROWS, LANES), BF16),
            pltpu.VMEM((1, t, tq), F32),
            pltpu.VMEM((1, t, tq), F32),
            pltpu.VMEM((1, 1, tq), F32),
            pltpu.VMEM((1, 1, tq), F32),
            pltpu.VMEM((1, MLA_V_DIM + ONES_ROWS, tq), F32),
        ],
        compiler_params=_params("arbitrary", "arbitrary", "arbitrary"),
        name="mla_attn",
    )(q3, k3, vt, z3, k_meta, vt_meta, padk, madd)


def _outproj_kernel(a_ref, b_ref, wa_ref, wb_ref, y_ref):
    y_ref[...] = _dot(a_ref[...], wa_ref[...]) + _dot(b_ref[...], wb_ref[...])


def _outproj(mix_a, mix_b, wa, wb):
    m, ka = mix_a.shape
    kb = mix_b.shape[1]
    n = wa.shape[1]
    tm = _tile(m, 1024, 16)
    tn = _tile(n, 1024)
    return pl.pallas_call(
        _outproj_kernel,
        out_shape=jax.ShapeDtypeStruct((m, n), F32),
        grid=(m // tm, n // tn),
        in_specs=[pl.BlockSpec((tm, ka), lambda i, j: (i, 0)),
                  pl.BlockSpec((tm, kb), lambda i, j: (i, 0)),
                  pl.BlockSpec((ka, tn), lambda i, j: (0, j)),
                  pl.BlockSpec((kb, tn), lambda i, j: (0, j))],
        out_specs=pl.BlockSpec((tm, tn), lambda i, j: (i, j)),
        compiler_params=_params("arbitrary", "arbitrary"),
        name="outproj",
    )(mix_a, mix_b, wa, wb)


def _postnorm_kernel(x_ref, y_ref, g_ref, o_ref):
    y = y_ref[...]
    ms = jnp.mean(y * y, axis=-1, keepdims=True)
    o_ref[...] = x_ref[...] + y * lax.rsqrt(ms + RMS_EPS) * g_ref[...]


def _postnorm(x2d, y, g):
    m, d = x2d.shape
    tm = _tile(m, 256, 8)
    return pl.pallas_call(
        _postnorm_kernel,
        out_shape=jax.ShapeDtypeStruct((m, d), F32),
        grid=(m // tm,),
        in_specs=[pl.BlockSpec((tm, d), lambda i: (i, 0)),
                  pl.BlockSpec((tm, d), lambda i: (i, 0)),
                  pl.BlockSpec((1, d), lambda i: (0, 0))],
        out_specs=pl.BlockSpec((tm, d), lambda i: (i, 0)),
        compiler_params=_params("arbitrary"),
        name="postnorm",
    )(x2d, y, g)


def _split3(v):
    v = np.asarray(v, np.float32)
    hi = v.astype(NP_BF16)
    r = v - hi.astype(np.float32)
    mid = r.astype(NP_BF16)
    lo = (r - mid.astype(np.float32)).astype(NP_BF16)
    return hi, mid, lo


def _rope_table(pos):
    inv_freq = 1.0 / (ROPE_THETA ** (jnp.arange(0, MLA_ROPE, 2, dtype=F32) / MLA_ROPE))
    ang = pos.astype(F32)[:, None] * inv_freq[None, :]
    cos, sin = jnp.cos(ang), jnp.sin(ang)
    return jnp.concatenate([cos, cos, sin, sin], axis=1)


def _alibi_tables(n_heads, n_meta, seq, t, tq):
    f32 = np.float32
    slopes = (2.0 ** (-8.0 * np.arange(1, n_heads + 1, dtype=f32) / n_heads)).astype(f32)
    slopes = (slopes * f32(LOG2E)).astype(f32)
    pos_q = ((np.arange(seq, dtype=f32) + n_meta)[None, :] * slopes[:, None]).astype(f32)
    pos_m = (np.arange(LANES, dtype=f32)[None, :] * slopes[:, None]).astype(f32)
    ones_q = np.ones((n_heads, seq), NP_BF16)
    zeros_q = np.zeros((n_heads, seq), NP_BF16)
    qh, qm, ql = _split3(-pos_q)
    kh, km, kl = _split3(pos_q)
    pad = np.zeros((n_heads, seq, LANES - 7), NP_BF16)
    eq = np.concatenate([np.stack([qh, qm, ql, ones_q, ones_q, ones_q, ones_q], -1), pad], -1)
    ek = np.concatenate([np.stack([ones_q, ones_q, ones_q, zeros_q, kh, km, kl], -1), pad], -1)
    mh, mmid, ml = _split3(pos_m)
    valid = (np.arange(LANES) < n_meta)[None, :]
    padmask = np.broadcast_to(np.where(valid, 0.0, MASK_VALUE).astype(f32),
                              (n_heads, LANES)).astype(NP_BF16)
    ones_m = np.ones((n_heads, LANES), NP_BF16)
    ekm = np.concatenate(
        [np.stack([ones_m, ones_m, ones_m, padmask, mh, mmid, ml], -1),
         np.zeros((n_heads, LANES, LANES - 7), NP_BF16)], -1)
    j = np.arange(tq).reshape(tq // t, t, 1)
    i = np.arange(tq)[None, None, :]
    visible = (j // CHUNK) <= (i // CHUNK)
    fut = (2.0 * np.maximum(j - i, 0)).astype(f32)
    dadd = np.where(visible[None], -slopes[:, None, None, None] * fut[None], MASK_VALUE).astype(f32)
    madd = np.where(visible, 0.0, MASK_VALUE).astype(f32)
    return tuple(jnp.asarray(a) for a in (eq, ek, ekm, dadd, madd))


def kernel(x, meta_tokens, norm_pre, w_in, diff_lambda_q1, diff_lambda_k1, diff_lambda_q2,
           diff_lambda_k2, diff_subln, mla_norm_q, mla_norm_kv, w_uq, w_ukv, w_out, norm_post):
    assert norm_pre.shape[0] == 1, "single-layer block"
    b, s, d = x.shape
    n_meta = meta_tokens.shape[0]
    mix = w_out.shape[1]
    diff_w = mix // 2
    mla_w = mix - diff_w
    ha = diff_w // DIFF_V_DIM
    hb = mla_w // MLA_V_DIM
    q_lora = mla_norm_q.shape[-1]
    kv_lora = mla_norm_kv.shape[-1]
    t = min(KEY_TILE, s)
    tq = min(QUERY_TILE, s)
    assert s % tq == 0 and tq % (2 * t) == 0 and t % CHUNK == 0 and n_meta <= 16
    lambda_init = 0.8 - 0.6 * math.exp(-0.3 * 0)

    w = w_in[0]
    sizes = (diff_w, diff_w, diff_w, diff_w, q_lora, kv_lora, MLA_ROPE, mla_w)
    offs = np.cumsum((0,) + sizes)
    w_dq, w_dk, w_dv, w_dg, w_cq, w_ckv, w_kr, w_mg = (w[:, offs[i]:offs[i + 1]] for i in range(8))
    w_main = jnp.concatenate([w_cq, w_ckv, w_dq, w_dk, w_dg, w_mg], axis=1).astype(BF16)
    n_main = w_main.shape[1]
    assert q_lora % kv_lora == 0 and (q_lora + kv_lora) % HEAD_TILE == 0
    q_blk0 = (q_lora + kv_lora) // HEAD_TILE
    k_blk0 = q_blk0 + diff_w // HEAD_TILE
    g_blk0 = k_blk0 + diff_w // HEAD_TILE
    mg_blk0 = (q_lora + kv_lora + 3 * diff_w) // MLA_V_DIM
    colscale = jnp.ones((1, n_main), F32).at[:, q_lora + kv_lora:q_lora + kv_lora + diff_w].set(
        DIFF_HEAD_DIM ** -0.5 * LOG2E)
    half = MLA_ROPE // 2

    def rot(wr):
        return jnp.concatenate([-wr[:, half:], wr[:, :half]], axis=1)

    w_kr2 = jnp.concatenate([w_kr, rot(w_kr)], axis=1).astype(BF16)
    w_dvt = w_dv.T.astype(BF16)

    wq = w_uq[0].reshape(q_lora, hb, MLA_NOPE + MLA_ROPE)
    wq_r = wq[:, :, MLA_NOPE:]
    wq_full = jnp.concatenate(
        [wq[:, :, :MLA_NOPE], wq_r, jnp.concatenate([-wq_r[:, :, half:], wq_r[:, :, :half]], -1)],
        axis=-1).reshape(q_lora, hb * HEAD_TILE).astype(BF16)
    wkv = w_ukv[0].reshape(kv_lora, hb, MLA_NOPE + MLA_V_DIM)
    w_kn = wkv[:, :, :MLA_NOPE].reshape(kv_lora, hb * MLA_NOPE).astype(BF16)
    w_vt = wkv[:, :, MLA_NOPE:].reshape(kv_lora, hb * MLA_V_DIM).T.astype(BF16)
    wo = w_out[0].astype(BF16)
    wo_a, wo_b = wo[:diff_w], wo[diff_w:]

    pos = jnp.arange(n_meta + s, dtype=jnp.int32)
    rope_tab = _rope_table(pos)
    tab_meta, tab_seq = rope_tab[:n_meta], rope_tab[n_meta:]
    eq_tab, ek_tab, ekm_tab, dadd_tab, madd_tab = _alibi_tables(ha, n_meta, s, t, tq)
    lane = jnp.arange(HEAD_TILE)
    row = jnp.arange(LANES)
    padk = jnp.where((lane[None, :] == MLA_NOPE + MLA_ROPE) & (row[:, None] >= n_meta),
                     MASK_VALUE, 0.0).astype(BF16)

    x2d = x.reshape(b * s, d)
    u = _prenorm(x2d, norm_pre)
    u_meta = _prenorm(meta_tokens.astype(x.dtype), norm_pre)
    z, kr = _inproj(u, w_main, colscale, w_kr2)
    z_meta, kr_meta = _inproj(u_meta, w_main, colscale, w_kr2)
    vt_a = _v_transposed(w_dvt, u, b, t)
    vt_a_meta = _v_transposed(w_dvt, u_meta, 1, n_meta)[0, 0]

    g_cq = mla_norm_q.astype(F32)
    g_ckv = mla_norm_kv.astype(F32)
    ckv_block = q_lora // kv_lora
    q_b = _mla_q(z, g_cq, wq_full, tab_seq, hb, q_lora, s)
    k_b, vt_b = _mla_kv(z, kr, g_ckv, w_kn, w_vt, tab_seq, hb, kv_lora, ckv_block, b, s, t)
    k_b_meta, vt_b_meta = _mla_kv(z_meta, kr_meta, g_ckv, w_kn, w_vt, tab_meta, hb, kv_lora,
                                  ckv_block, 1, n_meta, n_meta)
    vt_b_meta = vt_b_meta[0, 0]

    z3 = z.reshape(b, s, n_main)
    lams = [v.astype(F32) for v in (diff_lambda_q1, diff_lambda_k1, diff_lambda_q2, diff_lambda_k2)]
    mix_a = _diff_attn(z3, vt_a, z_meta, vt_a_meta, eq_tab, ek_tab, ekm_tab, dadd_tab, lams,
                       diff_subln.astype(F32), ha, (q_blk0, k_blk0, g_blk0), lambda_init, t, tq)
    mix_b = _mla_attn(q_b.reshape(b, s, hb * HEAD_TILE), k_b.reshape(b, s, hb * HEAD_TILE), vt_b,
                      z3, k_b_meta, vt_b_meta, padk, madd_tab, hb, mg_blk0, t, tq)

    y = _outproj(mix_a.reshape(b * s, diff_w), mix_b.reshape(b * s, mla_w), wo_a, wo_b)
    out = _postnorm(x2d, y, norm_post.astype(F32))
    return out.reshape(b, s, d)
```

```python
import functools
import math

import jax
import jax.numpy as jnp
import numpy as np
from jax import lax
from jax.experimental import pallas as pl
from jax.experimental.pallas import tpu as pltpu

CHUNK = 64
RMS_EPS = 1e-6
MASK_VALUE = -1e30
DIFF_HEAD_DIM = 128
DIFF_V_DIM = 2 * DIFF_HEAD_DIM
MLA_NOPE = 128
MLA_ROPE = 64
MLA_V_DIM = 128
ROPE_THETA = 10000.0

LANES = 128
HEAD_TILE = 2 * LANES
VMEM_LIMIT = 56 * 1024 * 1024

KEY_TILE = 512
DIAG_BLOCKS = 2
QUERY_TILE = DIAG_BLOCKS * KEY_TILE
M_INIT = -1e38
ONES_ROWS = 16
LOG2E = math.log2(math.e)

F32 = jnp.float32
BF16 = jnp.bfloat16
NP_BF16 = np.dtype(jnp.bfloat16)


def _params(*sem):
    return pltpu.CompilerParams(dimension_semantics=sem, vmem_limit_bytes=VMEM_LIMIT)


def _tile(dim, target, mult=LANES):
    if dim <= target:
        return dim
    t = (target // mult) * mult
    while t > mult and dim % t:
        t -= mult
    assert dim % t == 0, (dim, target)
    return t


def _dot(a, b):
    return jnp.dot(a, b, preferred_element_type=F32)


def _dot_nt(a, b):
    return lax.dot_general(a, b, (((1,), (1,)), ((), ())), preferred_element_type=F32)


def _prenorm_kernel(x_ref, g_ref, o_ref):
    x = x_ref[...]
    ms = jnp.mean(x * x, axis=-1, keepdims=True)
    o_ref[...] = (x * lax.rsqrt(ms + RMS_EPS) * g_ref[...]).astype(o_ref.dtype)


def _prenorm(x2d, g):
    m, d = x2d.shape
    tm = _tile(m, 256, 8)
    return pl.pallas_call(
        _prenorm_kernel,
        out_shape=jax.ShapeDtypeStruct((m, d), BF16),
        grid=(m // tm,),
        in_specs=[pl.BlockSpec((tm, d), lambda i: (i, 0)),
                  pl.BlockSpec((1, d), lambda i: (0, 0))],
        out_specs=pl.BlockSpec((tm, d), lambda i: (i, 0)),
        compiler_params=_params("arbitrary"),
        name="prenorm",
    )(x2d, g)


def _inproj_kernel(u_ref, w_ref, s_ref, z_ref):
    z_ref[...] = (_dot(u_ref[...], w_ref[...]) * s_ref[...]).astype(z_ref.dtype)


def _inproj_kr_kernel(u_ref, w_ref, wkr_ref, z_ref, kr_ref):
    u = u_ref[...]
    z_ref[...] = _dot(u, w_ref[...]).astype(z_ref.dtype)

    @pl.when(pl.program_id(1) == 0)
    def _():
        kr_ref[...] = _dot(u, wkr_ref[...])


def _inproj(u, w, colscale, tn, n_tiles, skip_at, skip_tiles):
    m, d = u.shape
    tm = _tile(m, 1024, 16)

    def wcol(i, j):
        return 0, jnp.where(j < skip_at, j, j + skip_tiles)

    return pl.pallas_call(
        _inproj_kernel,
        out_shape=jax.ShapeDtypeStruct((m, n_tiles * tn), BF16),
        grid=(m // tm, n_tiles),
        in_specs=[pl.BlockSpec((tm, d), lambda i, j: (i, 0)),
                  pl.BlockSpec((d, tn), wcol),
                  pl.BlockSpec((1, tn), lambda i, j: (0, j))],
        out_specs=pl.BlockSpec((tm, tn), lambda i, j: (i, j)),
        compiler_params=_params("arbitrary", "arbitrary"),
        name="inproj",
    )(u, w, colscale)


def _inproj_gate_kr(u, w, wkr):
    m, d = u.shape
    n = w.shape[1]
    tm = _tile(m, 1024, 16)
    tn = _tile(n, 1024)
    return pl.pallas_call(
        _inproj_kr_kernel,
        out_shape=(jax.ShapeDtypeStruct((m, n), BF16),
                   jax.ShapeDtypeStruct((m, LANES), F32)),
        grid=(m // tm, n // tn),
        in_specs=[pl.BlockSpec((tm, d), lambda i, j: (i, 0)),
                  pl.BlockSpec((d, tn), lambda i, j: (0, j)),
                  pl.BlockSpec((d, LANES), lambda i, j: (0, 0))],
        out_specs=(pl.BlockSpec((tm, tn), lambda i, j: (i, j)),
                   pl.BlockSpec((tm, LANES), lambda i, j: (i, 0))),
        compiler_params=_params("arbitrary", "arbitrary"),
        name="inproj_gate",
    )(u, w, wkr)


def _vt_kernel(w_ref, u_ref, o_ref):
    o_ref[0, 0] = _dot_nt(w_ref[...], u_ref[...]).astype(o_ref.dtype)


def _v_transposed(w_t, u, batch, t):
    n, d = w_t.shape
    m = u.shape[0]
    nt = m // (batch * t)
    return pl.pallas_call(
        _vt_kernel,
        out_shape=jax.ShapeDtypeStruct((batch, nt, n, t), BF16),
        grid=(m // t,),
        in_specs=[pl.BlockSpec((n, d), lambda i: (0, 0)),
                  pl.BlockSpec((t, d), lambda i: (i, 0))],
        out_specs=pl.BlockSpec((1, 1, n, t), lambda i: (i // nt, i % nt, 0, 0)),
        compiler_params=_params("arbitrary"),
        name="inproj_vt",
    )(w_t, u)


def _rope_combine(t):
    r = t + pltpu.roll(t, MLA_ROPE, axis=1)
    lane = lax.broadcasted_iota(jnp.int32, r.shape, 1)
    return jnp.where(lane < MLA_ROPE, r, 0.0), lane


def _mla_q_kernel(cq_ref, g_ref, w_ref, tab_ref, lanes_ref, o_ref, cqn_ref, *, heads_per_step, scale):
    @pl.when(pl.program_id(1) == 0)
    def _():
        c = cq_ref[...].astype(F32)
        ms = jnp.mean(c * c, axis=-1, keepdims=True)
        cqn_ref[...] = (c * lax.rsqrt(ms + RMS_EPS) * g_ref[...]).astype(cqn_ref.dtype)

    y = _dot(cqn_ref[...], w_ref[...])
    tab = tab_ref[...]
    mask_lanes = lanes_ref[...].astype(F32)
    for hh in range(heads_per_step):
        base = hh * HEAD_TILE
        nope = y[:, base:base + MLA_NOPE] * scale
        r, _ = _rope_combine(y[:, base + MLA_NOPE:base + HEAD_TILE] * tab)
        o_ref[:, base:base + MLA_NOPE] = nope.astype(o_ref.dtype)
        o_ref[:, base + MLA_NOPE:base + HEAD_TILE] = (r * scale + mask_lanes).astype(o_ref.dtype)


def _mla_q(z, g, w, tab, lanes_tab, n_heads, q_lora, cq_block, seq):
    m = z.shape[0]
    tm = _tile(m, 1024, 16)
    tm = math.gcd(tm, seq)
    hp = _tile(n_heads, 4, 1)
    nseq = seq // tm
    kern = functools.partial(_mla_q_kernel, heads_per_step=hp,
                             scale=float((MLA_NOPE + MLA_ROPE) ** -0.5 * LOG2E))
    return pl.pallas_call(
        kern,
        out_shape=jax.ShapeDtypeStruct((m, n_heads * HEAD_TILE), BF16),
        grid=(m // tm, n_heads // hp),
        in_specs=[pl.BlockSpec((tm, q_lora), lambda i, j: (i, cq_block)),
                  pl.BlockSpec((1, q_lora), lambda i, j: (0, 0)),
                  pl.BlockSpec((q_lora, hp * HEAD_TILE), lambda i, j: (0, j)),
                  pl.BlockSpec((tm, LANES), lambda i, j: (i % nseq, 0)),
                  pl.BlockSpec((tm, LANES), lambda i, j: (i % nseq, 0))],
        out_specs=pl.BlockSpec((tm, hp * HEAD_TILE), lambda i, j: (i, j)),
        scratch_shapes=[pltpu.VMEM((tm, q_lora), BF16)],
        compiler_params=_params("arbitrary", "arbitrary"),
        name="mla_q",
    )(z, g, w, tab, lanes_tab)


def _mla_kv_kernel(ckv_ref, kr_ref, g_ref, wk_ref, wvt_ref, tab_ref, lanes_ref, k_ref, vt_ref,
                   *, n_heads, t):
    c = ckv_ref[...].astype(F32)
    ms = jnp.mean(c * c, axis=-1, keepdims=True)
    cn = (c * lax.rsqrt(ms + RMS_EPS) * g_ref[...]).astype(BF16)
    kn = _dot(cn, wk_ref[...])
    kr, _ = _rope_combine(kr_ref[...] * tab_ref[...])
    kr = (kr + lanes_ref[...].astype(F32)).astype(k_ref.dtype)
    for h in range(n_heads):
        k_ref[:, h * HEAD_TILE:h * HEAD_TILE + MLA_NOPE] = (
            kn[:, h * MLA_NOPE:(h + 1) * MLA_NOPE].astype(k_ref.dtype))
        k_ref[:, h * HEAD_TILE + MLA_NOPE:(h + 1) * HEAD_TILE] = kr
    vt = _dot_nt(wvt_ref[...], cn).astype(vt_ref.dtype)
    for tt in range(vt_ref.shape[1]):
        vt_ref[0, tt] = vt[:, tt * t:(tt + 1) * t]


def _mla_kv(z, kr, g, wk, wvt, tab, lanes_tab, n_heads, kv_lora, ckv_block, batch, seq, t):
    m = z.shape[0]
    tm = math.gcd(_tile(m, 1024, 16), seq)
    t = min(t, tm)
    nseq = seq // tm
    kern = functools.partial(_mla_kv_kernel, n_heads=n_heads, t=t)
    return pl.pallas_call(
        kern,
        out_shape=(jax.ShapeDtypeStruct((m, n_heads * HEAD_TILE), BF16),
                   jax.ShapeDtypeStruct((batch, seq // t, n_heads * MLA_V_DIM, t), BF16)),
        grid=(m // tm,),
        in_specs=[pl.BlockSpec((tm, kv_lora), lambda i: (i, ckv_block)),
                  pl.BlockSpec((tm, LANES), lambda i: (i, 0)),
                  pl.BlockSpec((1, kv_lora), lambda i: (0, 0)),
                  pl.BlockSpec((kv_lora, n_heads * MLA_NOPE), lambda i: (0, 0)),
                  pl.BlockSpec((n_heads * MLA_V_DIM, kv_lora), lambda i: (0, 0)),
                  pl.BlockSpec((tm, LANES), lambda i: (i % nseq, 0)),
                  pl.BlockSpec((tm, LANES), lambda i: (i % nseq, 0))],
        out_specs=(pl.BlockSpec((tm, n_heads * HEAD_TILE), lambda i: (i, 0)),
                   pl.BlockSpec((1, tm // t, n_heads * MLA_V_DIM, t),
                                lambda i: (i // nseq, i % nseq, 0, 0))),
        compiler_params=_params("arbitrary"),
        name="mla_kv",
    )(z, kr, g, wk, wvt, tab, lanes_tab)


def _col_max(s_t):
    return jnp.max(s_t, axis=0, keepdims=True)


def _score_stage(s_ref, mx_ref, mm, s_t):
    s_ref[mm] = s_t
    mx_ref[mm] = _col_max(s_t)


def _exp_stage(s_ref, mx_ref, p_ref, mm, m_prev):
    m_new = jnp.maximum(m_prev, mx_ref[mm])
    p_ref[mm] = jnp.exp2(s_ref[mm] - m_new).astype(BF16)
    return m_new, jnp.exp2(m_prev - m_new)


def _acc_stage(p_ref, acc_ref, mm, vt_aug, alpha):
    acc_ref[mm] = alpha * acc_ref[mm] + _dot(vt_aug, p_ref[mm])


def _silu(g):
    return g * jax.nn.sigmoid(g)


def _build_values(vaug_ref, vtm_ref, vt_ref, vtmeta_ref, dv, n_meta):
    vaug_ref[:, 0:dv, :] = vt_ref[0]
    vaug_ref[:, dv:, :] = jnp.ones((vaug_ref.shape[0], ONES_ROWS, vaug_ref.shape[2]), BF16)
    vtm_ref[0:dv, :] = jnp.zeros((dv, vtm_ref.shape[1]), BF16)
    vtm_ref[0:dv, 0:n_meta] = vtmeta_ref[...]
    vtm_ref[dv:, :] = jnp.ones((ONES_ROWS, vtm_ref.shape[1]), BF16)


def _attention_sweep(n_maps, qk, v_blk, meta_scores, vtm, diag_fix, s_refs, mx_refs, p_refs,
                     acc_ref, n_full):
    maps = range(n_maps)

    def score(slot, blk, fix=None):
        for mm in maps:
            s_t = qk(mm, blk)
            _score_stage(s_refs[slot], mx_refs[slot], mm, s_t if fix is None else diag_fix(fix, s_t))

    def expo(slot, m):
        out = [_exp_stage(s_refs[slot], mx_refs[slot], p_refs[slot], mm, m[mm]) for mm in maps]
        return [o[0] for o in out], [o[1] for o in out]

    def accum(slot, blk, alpha):
        vb = v_blk(blk)
        for mm in maps:
            _acc_stage(p_refs[slot], acc_ref, mm, vb, alpha[mm])

    s_meta = [meta_scores(mm) for mm in maps]
    score(0, n_full, fix=0)
    score(1, n_full + 1, fix=1)
    m = []
    for mm in maps:
        mx = _col_max(s_meta[mm])
        acc_ref[mm] = _dot(vtm, jnp.exp2(s_meta[mm] - mx).astype(BF16))
        m.append(mx)
    m, a0 = expo(0, m)

    def body(i, carry):
        m, a0 = carry
        first = i == 0
        accum(0, jnp.where(first, n_full, 2 * i - 2), a0)
        m, a1 = expo(1, m)
        score(0, 2 * i)
        accum(1, jnp.where(first, n_full + 1, 2 * i - 1), a1)
        m, a0 = expo(0, m)
        score(1, 2 * i + 1)
        return m, a0

    m, a0 = lax.fori_loop(0, n_full // 2, body, (m, a0))

    last = jnp.where(n_full == 0, 0, n_full - 2)
    accum(0, last, a0)
    m, a1 = expo(1, m)
    accum(1, last + 1, a1)


def _diff_attn_kernel(q_ref, k_ref, vt_ref, g_ref, kmeta_ref, vtmeta_ref, eq_ref, ek_ref,
                      ekmeta_ref, core_ref, lq1_ref, lk1_ref, lq2_ref, lk2_ref, subln_ref,
                      o_ref, kaug_ref, kmaug_ref, vaug_ref, vtm_ref, s0_ref, s1_ref, mx0_ref, mx1_ref,
                      p0_ref, p1_ref, acc_ref, *, lambda_init, t, n_meta):
    qi = pl.program_id(2)
    d = DIFF_HEAD_DIM
    dv = DIFF_V_DIM

    @pl.when(qi == 0)
    def _build():
        kmaug_ref[...] = jnp.zeros(kmaug_ref.shape, BF16)
        _build_values(vaug_ref, vtm_ref, vt_ref, vtmeta_ref, dv, n_meta)
        for mm in range(2):
            kaug_ref[mm, :, 0:d] = k_ref[0, :, mm * d:(mm + 1) * d]
            kaug_ref[mm, :, d:2 * d] = ek_ref[0]
            kmaug_ref[mm, 0:n_meta, 0:d] = kmeta_ref[:, mm * d:(mm + 1) * d]
            kmaug_ref[mm, :, d:2 * d] = ekmeta_ref[0]

    q = q_ref[0]
    eq = eq_ref[0]
    qa = [jnp.concatenate([q[:, mm * d:(mm + 1) * d], eq], axis=1) for mm in range(2)]

    def qk(mm, blk):
        off = pl.multiple_of(blk * t, t)
        return _dot_nt(kaug_ref[mm, pl.ds(off, t), :], qa[mm])

    def diag_fix(kd, s_t):
        lo, hi = kd * t, (kd + 1) * t
        parts = [s_t[:, :lo]] if kd else []
        parts.append(s_t[:, lo:hi] + core_ref[0])
        if hi < s_t.shape[1]:
            parts.append(s_t[:, hi:])
        return jnp.concatenate(parts, axis=1)

    _attention_sweep(2, qk, lambda blk: vaug_ref[blk],
                     lambda mm: _dot_nt(kmaug_ref[mm], qa[mm]), vtm_ref[...], diag_fix,
                     (s0_ref, s1_ref), (mx0_ref, mx1_ref), (p0_ref, p1_ref), acc_ref,
                     qi * DIAG_BLOCKS)

    lam = (jnp.exp(jnp.sum(lq1_ref[...] * lk1_ref[...], axis=-1, keepdims=True))
           - jnp.exp(jnp.sum(lq2_ref[...] * lk2_ref[...], axis=-1, keepdims=True))
           + lambda_init)
    o_t = (acc_ref[0, 0:dv] / acc_ref[0, dv:dv + 1]
           - lam * (acc_ref[1, 0:dv] / acc_ref[1, dv:dv + 1]))
    ms = jnp.mean(o_t * o_t, axis=0, keepdims=True)
    o = (o_t * lax.rsqrt(ms + RMS_EPS)).T
    o = (o * subln_ref[...]) * (1.0 - lambda_init)
    o_ref[0] = (o * _silu(g_ref[0].astype(F32))).astype(o_ref.dtype)


def _diff_attn(z3, vt, z_meta, vt_meta, eq_tab, ek_tab, ekmeta_tab, core_tab, lams, subln,
               n_heads, blk, lambda_init, t, tq):
    b, s, _ = z3.shape
    n_meta = z_meta.shape[0]
    nq = s // tq
    q0, k0, g0 = blk
    vec = pl.BlockSpec((1, DIFF_HEAD_DIM), lambda bb, h, qi: (0, 0))
    kern = functools.partial(_diff_attn_kernel, lambda_init=lambda_init, t=t, n_meta=n_meta)
    return pl.pallas_call(
        kern,
        out_shape=jax.ShapeDtypeStruct((b, s, n_heads * DIFF_V_DIM), BF16),
        grid=(b, n_heads, nq),
        in_specs=[
            pl.BlockSpec((1, tq, HEAD_TILE), lambda bb, h, qi: (bb, qi, q0 + h)),
            pl.BlockSpec((1, s, HEAD_TILE), lambda bb, h, qi: (bb, 0, k0 + h)),
            pl.BlockSpec((1, s // t, DIFF_V_DIM, t), lambda bb, h, qi: (bb, 0, h, 0)),
            pl.BlockSpec((1, tq, DIFF_V_DIM), lambda bb, h, qi: (bb, qi, g0 + h)),
            pl.BlockSpec((n_meta, HEAD_TILE), lambda bb, h, qi: (0, k0 + h)),
            pl.BlockSpec((DIFF_V_DIM, n_meta), lambda bb, h, qi: (h, 0)),
            pl.BlockSpec((1, tq, LANES), lambda bb, h, qi: (h, qi, 0)),
            pl.BlockSpec((1, s, LANES), lambda bb, h, qi: (h, 0, 0)),
            pl.BlockSpec((1, LANES, LANES), lambda bb, h, qi: (h, 0, 0)),
            pl.BlockSpec((1, t, t), lambda bb, h, qi: (h, 0, 0)),
            vec, vec, vec, vec,
            pl.BlockSpec((1, DIFF_V_DIM), lambda bb, h, qi: (0, 0)),
        ],
        out_specs=pl.BlockSpec((1, tq, DIFF_V_DIM), lambda bb, h, qi: (bb, qi, h)),
        scratch_shapes=[
            pltpu.VMEM((2, s, HEAD_TILE), BF16),
            pltpu.VMEM((2, LANES, HEAD_TILE), BF16),
            pltpu.VMEM((s // t, DIFF_V_DIM + ONES_ROWS, t), BF16),
            pltpu.VMEM((DIFF_V_DIM + ONES_ROWS, LANES), BF16),
            pltpu.VMEM((2, t, tq), F32),
            pltpu.VMEM((2, t, tq), F32),
            pltpu.VMEM((2, 1, tq), F32),
            pltpu.VMEM((2, 1, tq), F32),
            pltpu.VMEM((2, t, tq), BF16),
            pltpu.VMEM((2, t, tq), BF16),
            pltpu.VMEM((2, DIFF_V_DIM + ONES_ROWS, tq), F32),
        ],
        compiler_params=_params("arbitrary", "arbitrary", "arbitrary"),
        name="diff_attn",
    )(z3, z3, vt, z3, z_meta, vt_meta, eq_tab, ek_tab, ekmeta_tab, core_tab, *lams, subln)


def _mla_attn_kernel(q_ref, k_ref, vt_ref, g_ref, kmeta_ref, vtmeta_ref, padk_ref,
                     o_ref, kmaug_ref, vaug_ref, vtm_ref, s0_ref, s1_ref, mx0_ref, mx1_ref,
                     p0_ref, p1_ref, acc_ref, *, t, n_meta):
    qi = pl.program_id(2)
    dv = MLA_V_DIM

    @pl.when(qi == 0)
    def _build():
        kmaug_ref[...] = padk_ref[...]
        kmaug_ref[0:n_meta, :] = kmeta_ref[...]
        _build_values(vaug_ref, vtm_ref, vt_ref, vtmeta_ref, dv, n_meta)

    q = q_ref[0]

    def qk(mm, blk):
        off = pl.multiple_of(blk * t, t)
        return _dot_nt(k_ref[0, pl.ds(off, t), :], q)

    _attention_sweep(1, qk, lambda blk: vaug_ref[blk],
                     lambda mm: _dot_nt(kmaug_ref[...], q), vtm_ref[...], lambda kd, s_t: s_t,
                     (s0_ref, s1_ref), (mx0_ref, mx1_ref), (p0_ref, p1_ref), acc_ref,
                     qi * DIAG_BLOCKS)

    o = (acc_ref[0, 0:dv] / acc_ref[0, dv:dv + 1]).T
    o_ref[0] = (o * _silu(g_ref[0].astype(F32))).astype(o_ref.dtype)


def _mla_attn(q3, k3, vt, z3, k_meta, vt_meta, padk, n_heads, g0, t, tq):
    b, s, _ = q3.shape
    n_meta = k_meta.shape[0]
    nq = s // tq
    kern = functools.partial(_mla_attn_kernel, t=t, n_meta=n_meta)
    return pl.pallas_call(
        kern,
        out_shape=jax.ShapeDtypeStruct((b, s, n_heads * MLA_V_DIM), BF16),
        grid=(b, n_heads, nq),
        in_specs=[
            pl.BlockSpec((1, tq, HEAD_TILE), lambda bb, h, qi: (bb, qi, h)),
            pl.BlockSpec((1, s, HEAD_TILE), lambda bb, h, qi: (bb, 0, h)),
            pl.BlockSpec((1, s // t, MLA_V_DIM, t), lambda bb, h, qi: (bb, 0, h, 0)),
            pl.BlockSpec((1, tq, MLA_V_DIM), lambda bb, h, qi: (bb, qi, g0 + h)),
            pl.BlockSpec((n_meta, HEAD_TILE), lambda bb, h, qi: (0, h)),
            pl.BlockSpec((MLA_V_DIM, n_meta), lambda bb, h, qi: (h, 0)),
            pl.BlockSpec((LANES, HEAD_TILE), lambda bb, h, qi: (0, 0)),
        ],
        out_specs=pl.BlockSpec((1, tq, MLA_V_DIM), lambda bb, h, qi: (bb, qi, h)),
        scratch_shapes=[
            pltpu.VMEM((LANES, HEAD_TILE), BF16),
            pltpu.VMEM((s // t, MLA_V_DIM + ONES_ROWS, t), BF16),
            pltpu.VMEM((MLA_V_DIM + ONES_ROWS, LANES), BF16),
            pltpu.VMEM((1, t, tq), F32),
            pltpu.VMEM((1, t, tq), F32),
            pltpu.VMEM((1, 1, tq), F32),
            pltpu.VMEM((1, 1, tq), F32),
            pltpu.VMEM((1, t, tq), BF16),
            pltpu.VMEM((1, t, tq), BF16),
            pltpu.VMEM((1, MLA_V_DIM + ONES_ROWS, tq), F32),
        ],
        compiler_params=_params("arbitrary", "arbitrary", "arbitrary"),
        name="mla_attn",
    )(q3, k3, vt, z3, k_meta, vt_meta, padk)


def _outproj_kernel(a_ref, b_ref, w_ref, x_ref, g_ref, o_ref, *, a_steps, n_steps):
    k = pl.program_id(1)

    @pl.when(k == 0)
    def _():
        o_ref[...] = _dot(a_ref[...], w_ref[...])

    @pl.when((k > 0) & (k < a_steps))
    def _():
        o_ref[...] += _dot(a_ref[...], w_ref[...])

    @pl.when(k >= a_steps)
    def _():
        o_ref[...] += _dot(b_ref[...], w_ref[...])

    @pl.when(k == n_steps - 1)
    def _():
        y = o_ref[...]
        ms = jnp.mean(y * y, axis=-1, keepdims=True)
        o_ref[...] = x_ref[...] + y * lax.rsqrt(ms + RMS_EPS) * g_ref[...]


def _outproj(mix_a, mix_b, w, x2d, g):
    m, ka = mix_a.shape
    kb = mix_b.shape[1]
    n = w.shape[1]
    tm = _tile(m, 512, 16)
    tk = _tile(math.gcd(ka, kb), 512)
    a_steps, b_steps = ka // tk, kb // tk
    kern = functools.partial(_outproj_kernel, a_steps=a_steps, n_steps=a_steps + b_steps)
    return pl.pallas_call(
        kern,
        out_shape=jax.ShapeDtypeStruct((m, n), F32),
        grid=(m // tm, a_steps + b_steps),
        in_specs=[pl.BlockSpec((tm, tk), lambda i, k: (i, jnp.minimum(k, a_steps - 1))),
                  pl.BlockSpec((tm, tk), lambda i, k: (i, jnp.maximum(k - a_steps, 0))),
                  pl.BlockSpec((tk, n), lambda i, k: (k, 0)),
                  pl.BlockSpec((tm, n), lambda i, k: (i, 0)),
                  pl.BlockSpec((1, n), lambda i, k: (0, 0))],
        out_specs=pl.BlockSpec((tm, n), lambda i, k: (i, 0)),
        compiler_params=_params("arbitrary", "arbitrary"),
        name="outproj",
    )(mix_a, mix_b, w, x2d, g)


def _split3(v):
    v = np.asarray(v, np.float32)
    hi = v.astype(NP_BF16)
    r = v - hi.astype(np.float32)
    mid = r.astype(NP_BF16)
    lo = (r - mid.astype(np.float32)).astype(NP_BF16)
    return hi, mid, lo


def _rope_table(pos):
    inv_freq = 1.0 / (ROPE_THETA ** (jnp.arange(0, MLA_ROPE, 2, dtype=F32) / MLA_ROPE))
    ang = pos.astype(F32)[:, None] * inv_freq[None, :]
    cos, sin = jnp.cos(ang), jnp.sin(ang)
    return jnp.concatenate([cos, cos, sin, sin], axis=1)


def _chunk_mask_lanes(seq):
    chunk = np.arange(seq) // CHUNK
    c = np.arange(seq // CHUNK - 1)
    q_lanes = (chunk[:, None] == c[None, :]).astype(np.float32)
    k_lanes = np.where(chunk[:, None] > c[None, :], MASK_VALUE, 0.0).astype(np.float32)
    return q_lanes.astype(NP_BF16), k_lanes.astype(NP_BF16)


def _diff_tables(n_heads, n_meta, seq, t):
    f32 = np.float32
    slopes = (2.0 ** (-8.0 * np.arange(1, n_heads + 1, dtype=f32) / n_heads)).astype(f32)
    slopes = (slopes * f32(LOG2E)).astype(f32)
    pos_q = ((np.arange(seq, dtype=f32) + n_meta)[None, :] * slopes[:, None]).astype(f32)
    pos_m = (np.arange(LANES, dtype=f32)[None, :] * slopes[:, None]).astype(f32)
    qc, kc = _chunk_mask_lanes(seq)
    n_c = qc.shape[1]
    assert 8 + n_c <= LANES
    ones_q = np.ones((n_heads, seq), NP_BF16)
    zeros_q = np.zeros((n_heads, seq), NP_BF16)
    qh, qm, ql = _split3(-pos_q)
    kh, km, kl = _split3(pos_q)
    tail = np.zeros((n_heads, seq, LANES - 8 - n_c), NP_BF16)

    def lanes(cols, chunk_lanes):
        return np.concatenate([np.stack(cols + [zeros_q], -1),
                               np.broadcast_to(chunk_lanes, (n_heads,) + chunk_lanes.shape), tail], -1)

    eq = lanes([qh, qm, ql, ones_q, ones_q, ones_q, ones_q], qc)
    ek = lanes([ones_q, ones_q, ones_q, zeros_q, kh, km, kl], kc)
    mh, mmid, ml = _split3(pos_m)
    valid = (np.arange(LANES) < n_meta)[None, :]
    padmask = np.broadcast_to(np.where(valid, 0.0, MASK_VALUE).astype(f32),
                              (n_heads, LANES)).astype(NP_BF16)
    ones_m = np.ones((n_heads, LANES), NP_BF16)
    ekm = np.concatenate(
        [np.stack([ones_m, ones_m, ones_m, padmask, mh, mmid, ml], -1),
         np.zeros((n_heads, LANES, LANES - 7), NP_BF16)], -1)
    j = np.arange(t)[:, None]
    i = np.arange(t)[None, :]
    same = (j // CHUNK) == (i // CHUNK)
    fut = np.where(same, 2.0 * np.maximum(j - i, 0), 0.0).astype(f32)
    core = (-slopes[:, None, None] * fut[None]).astype(f32)
    return tuple(jnp.asarray(a) for a in (eq, ek, ekm, core))


def _mla_mask_lanes(n_meta, seq):
    qc, kc = _chunk_mask_lanes(seq)
    n_c = qc.shape[1]
    assert MLA_ROPE + 1 + n_c <= LANES
    q = np.zeros((seq, LANES), NP_BF16)
    k = np.zeros((seq, LANES), NP_BF16)
    q[:, MLA_ROPE] = 1.0
    q[:, MLA_ROPE + 1:MLA_ROPE + 1 + n_c] = qc
    k[:, MLA_ROPE + 1:MLA_ROPE + 1 + n_c] = kc
    return jnp.asarray(q), jnp.asarray(k), jnp.zeros((n_meta, LANES), BF16)


def kernel(x, meta_tokens, norm_pre, w_in, diff_lambda_q1, diff_lambda_k1, diff_lambda_q2,
           diff_lambda_k2, diff_subln, mla_norm_q, mla_norm_kv, w_uq, w_ukv, w_out, norm_post):
    assert norm_pre.shape[0] == 1, "single-layer block"
    b, s, d = x.shape
    n_meta = meta_tokens.shape[0]
    mix = w_out.shape[1]
    diff_w = mix // 2
    mla_w = mix - diff_w
    ha = diff_w // DIFF_V_DIM
    hb = mla_w // MLA_V_DIM
    q_lora = mla_norm_q.shape[-1]
    kv_lora = mla_norm_kv.shape[-1]
    t = min(KEY_TILE, s)
    tq = min(QUERY_TILE, s)
    assert s % tq == 0 and tq == DIAG_BLOCKS * t and t % CHUNK == 0 and n_meta <= 16
    lambda_init = 0.8 - 0.6 * math.exp(-0.3 * 0)

    w = w_in[0]
    sizes = (diff_w, diff_w, diff_w, diff_w, q_lora, kv_lora, MLA_ROPE, mla_w)
    offs = np.cumsum((0,) + sizes)
    w_dv, w_kr, w_mg = (w[:, offs[i]:offs[i + 1]] for i in (2, 6, 7))
    w_all = w.astype(BF16)
    lat = q_lora + kv_lora
    tn = _tile(math.gcd(diff_w, lat), 1024)
    n_main = 3 * diff_w + lat
    assert (3 * diff_w) % q_lora == 0 and (3 * diff_w + q_lora) % kv_lora == 0
    q_blk0 = 0
    k_blk0 = diff_w // HEAD_TILE
    g_blk0 = 2 * diff_w // HEAD_TILE
    cq_block = 3 * diff_w // q_lora
    ckv_block = (3 * diff_w + q_lora) // kv_lora
    colscale = jnp.ones((1, n_main), F32).at[:, :diff_w].set(DIFF_HEAD_DIM ** -0.5 * LOG2E)
    half = MLA_ROPE // 2

    def rot(wr):
        return jnp.concatenate([-wr[:, half:], wr[:, :half]], axis=1)

    w_kr2 = jnp.concatenate([w_kr, rot(w_kr)], axis=1).astype(BF16)
    w_dvt = w_dv.T.astype(BF16)

    wq = w_uq[0].reshape(q_lora, hb, MLA_NOPE + MLA_ROPE)
    wq_r = wq[:, :, MLA_NOPE:]
    wq_full = jnp.concatenate(
        [wq[:, :, :MLA_NOPE], wq_r, jnp.concatenate([-wq_r[:, :, half:], wq_r[:, :, :half]], -1)],
        axis=-1).reshape(q_lora, hb * HEAD_TILE).astype(BF16)
    wkv = w_ukv[0].reshape(kv_lora, hb, MLA_NOPE + MLA_V_DIM)
    w_kn = wkv[:, :, :MLA_NOPE].reshape(kv_lora, hb * MLA_NOPE).astype(BF16)
    w_vt = wkv[:, :, MLA_NOPE:].reshape(kv_lora, hb * MLA_V_DIM).T.astype(BF16)
    wo = w_out[0].astype(BF16)

    pos = jnp.arange(n_meta + s, dtype=jnp.int32)
    rope_tab = _rope_table(pos)
    tab_meta, tab_seq = rope_tab[:n_meta], rope_tab[n_meta:]
    eq_tab, ek_tab, ekm_tab, core_tab = _diff_tables(ha, n_meta, s, t)
    qlane_tab, klane_tab, klane_meta = _mla_mask_lanes(n_meta, s)
    lane = jnp.arange(HEAD_TILE)
    row = jnp.arange(LANES)
    padk = jnp.where((lane[None, :] == MLA_NOPE + MLA_ROPE) & (row[:, None] >= n_meta),
                     MASK_VALUE, 0.0).astype(BF16)

    x2d = x.reshape(b * s, d)
    u = _prenorm(x2d, norm_pre)
    u_meta = _prenorm(meta_tokens.astype(x.dtype), norm_pre)
    main_tiles = (n_main // tn, 2 * diff_w // tn, diff_w // tn)
    z = _inproj(u, w_all, colscale, tn, *main_tiles)
    z_meta = _inproj(u_meta, w_all, colscale, tn, *main_tiles)
    w_mg16 = w_mg.astype(BF16)
    z_gate, kr = _inproj_gate_kr(u, w_mg16, w_kr2)
    _, kr_meta = _inproj_gate_kr(u_meta, w_mg16, w_kr2)
    vt_a = _v_transposed(w_dvt, u, b, t)
    vt_a_meta = _v_transposed(w_dvt, u_meta, 1, n_meta)[0, 0]

    g_cq = mla_norm_q.astype(F32)
    g_ckv = mla_norm_kv.astype(F32)
    q_b = _mla_q(z, g_cq, wq_full, tab_seq, qlane_tab, hb, q_lora, cq_block, s)
    k_b, vt_b = _mla_kv(z, kr, g_ckv, w_kn, w_vt, tab_seq, klane_tab, hb, kv_lora, ckv_block,
                        b, s, t)
    k_b_meta, vt_b_meta = _mla_kv(z_meta, kr_meta, g_ckv, w_kn, w_vt, tab_meta, klane_meta, hb,
                                  kv_lora, ckv_block, 1, n_meta, n_meta)
    vt_b_meta = vt_b_meta[0, 0]

    z3 = z.reshape(b, s, n_main)
    lams = [v.astype(F32) for v in (diff_lambda_q1, diff_lambda_k1, diff_lambda_q2, diff_lambda_k2)]
    mix_a = _diff_attn(z3, vt_a, z_meta, vt_a_meta, eq_tab, ek_tab, ekm_tab, core_tab, lams,
                       diff_subln.astype(F32), ha, (q_blk0, k_blk0, g_blk0), lambda_init, t, tq)
    mix_b = _mla_attn(q_b.reshape(b, s, hb * HEAD_TILE), k_b.reshape(b, s, hb * HEAD_TILE), vt_b,
                      z_gate.reshape(b, s, mla_w), k_b_meta, vt_b_meta, padk, hb, 0, t, tq)

    out = _outproj(mix_a.reshape(b * s, diff_w), mix_b.reshape(b * s, mla_w), wo, x2d,
                   norm_post.astype(F32))
    return out.reshape(b, s, d)
```

```python
import functools
import math

import jax
import jax.numpy as jnp
import numpy as np
from jax import lax
from jax.experimental import pallas as pl
from jax.experimental.pallas import tpu as pltpu

CHUNK = 64
RMS_EPS = 1e-6
MASK_VALUE = -1e30
DIFF_HEAD_DIM = 128
DIFF_V_DIM = 2 * DIFF_HEAD_DIM
MLA_NOPE = 128
MLA_ROPE = 64
MLA_V_DIM = 128
ROPE_THETA = 10000.0

LANES = 128
HEAD_TILE = 2 * LANES
VMEM_LIMIT = 56 * 1024 * 1024

KEY_TILE = 512
DIAG_BLOCKS = 2
QUERY_TILE = DIAG_BLOCKS * KEY_TILE
COL_TILE = 256
M_INIT = -1e38
ONES_ROWS = 16
LOG2E = math.log2(math.e)

F32 = jnp.float32
BF16 = jnp.bfloat16
NP_BF16 = np.dtype(jnp.bfloat16)


def _params(*sem):
    return pltpu.CompilerParams(dimension_semantics=sem, vmem_limit_bytes=VMEM_LIMIT)


def _tile(dim, target, mult=LANES):
    if dim <= target:
        return dim
    t = (target // mult) * mult
    while t > mult and dim % t:
        t -= mult
    assert dim % t == 0, (dim, target)
    return t


def _dot(a, b):
    return jnp.dot(a, b, preferred_element_type=F32)


def _dot_nt(a, b):
    return lax.dot_general(a, b, (((1,), (1,)), ((), ())), preferred_element_type=F32)


def _prenorm_kernel(x_ref, g_ref, o_ref):
    x = x_ref[...]
    ms = jnp.mean(x * x, axis=-1, keepdims=True)
    o_ref[...] = (x * lax.rsqrt(ms + RMS_EPS) * g_ref[...]).astype(o_ref.dtype)


def _prenorm(x2d, g):
    m, d = x2d.shape
    tm = _tile(m, 256, 8)
    return pl.pallas_call(
        _prenorm_kernel,
        out_shape=jax.ShapeDtypeStruct((m, d), BF16),
        grid=(m // tm,),
        in_specs=[pl.BlockSpec((tm, d), lambda i: (i, 0)),
                  pl.BlockSpec((1, d), lambda i: (0, 0))],
        out_specs=pl.BlockSpec((tm, d), lambda i: (i, 0)),
        compiler_params=_params("arbitrary"),
        name="prenorm",
    )(x2d, g)


def _inproj_kernel(u_ref, w_ref, s_ref, z_ref):
    z_ref[...] = (_dot(u_ref[...], w_ref[...]) * s_ref[...]).astype(z_ref.dtype)


def _inproj_kr_kernel(u_ref, w_ref, wkr_ref, z_ref, kr_ref):
    u = u_ref[...]
    z_ref[...] = _dot(u, w_ref[...]).astype(z_ref.dtype)

    @pl.when(pl.program_id(1) == 0)
    def _():
        kr_ref[...] = _dot(u, wkr_ref[...])


def _inproj(u, w, colscale, tn, n_tiles, skip_at, skip_tiles):
    m, d = u.shape
    tm = _tile(m, 1024, 16)

    def wcol(i, j):
        return 0, jnp.where(j < skip_at, j, j + skip_tiles)

    return pl.pallas_call(
        _inproj_kernel,
        out_shape=jax.ShapeDtypeStruct((m, n_tiles * tn), BF16),
        grid=(m // tm, n_tiles),
        in_specs=[pl.BlockSpec((tm, d), lambda i, j: (i, 0)),
                  pl.BlockSpec((d, tn), wcol),
                  pl.BlockSpec((1, tn), lambda i, j: (0, j))],
        out_specs=pl.BlockSpec((tm, tn), lambda i, j: (i, j)),
        compiler_params=_params("arbitrary", "arbitrary"),
        name="inproj",
    )(u, w, colscale)


def _inproj_gate_kr(u, w, wkr):
    m, d = u.shape
    n = w.shape[1]
    tm = _tile(m, 1024, 16)
    tn = _tile(n, 1024)
    return pl.pallas_call(
        _inproj_kr_kernel,
        out_shape=(jax.ShapeDtypeStruct((m, n), BF16),
                   jax.ShapeDtypeStruct((m, LANES), F32)),
        grid=(m // tm, n // tn),
        in_specs=[pl.BlockSpec((tm, d), lambda i, j: (i, 0)),
                  pl.BlockSpec((d, tn), lambda i, j: (0, j)),
                  pl.BlockSpec((d, LANES), lambda i, j: (0, 0))],
        out_specs=(pl.BlockSpec((tm, tn), lambda i, j: (i, j)),
                   pl.BlockSpec((tm, LANES), lambda i, j: (i, 0))),
        compiler_params=_params("arbitrary", "arbitrary"),
        name="inproj_gate",
    )(u, w, wkr)


def _transpose_cast_kernel(w_ref, o_ref):
    o_ref[...] = w_ref[...].T.astype(o_ref.dtype)


def _transpose_cast(w2d, col0, n):
    d = w2d.shape[0]
    tb = _tile(math.gcd(math.gcd(d, n), col0), 512)
    return pl.pallas_call(
        _transpose_cast_kernel,
        out_shape=jax.ShapeDtypeStruct((n, d), BF16),
        grid=(n // tb, d // tb),
        in_specs=[pl.BlockSpec((tb, tb), lambda i, j: (j, col0 // tb + i))],
        out_specs=pl.BlockSpec((tb, tb), lambda i, j: (i, j)),
        compiler_params=_params("arbitrary", "arbitrary"),
        name="wv_transpose",
    )(w2d)


def _vt_kernel(w_ref, u_ref, o_ref):
    o_ref[0, 0] = _dot_nt(w_ref[...], u_ref[...]).astype(o_ref.dtype)


def _v_transposed(w_t, u, batch, t):
    n, d = w_t.shape
    m = u.shape[0]
    nt = m // (batch * t)
    return pl.pallas_call(
        _vt_kernel,
        out_shape=jax.ShapeDtypeStruct((batch, nt, n, t), BF16),
        grid=(m // t,),
        in_specs=[pl.BlockSpec((n, d), lambda i: (0, 0)),
                  pl.BlockSpec((t, d), lambda i: (i, 0))],
        out_specs=pl.BlockSpec((1, 1, n, t), lambda i: (i // nt, i % nt, 0, 0)),
        compiler_params=_params("arbitrary"),
        name="inproj_vt",
    )(w_t, u)


def _rope_combine(t):
    r = t + pltpu.roll(t, MLA_ROPE, axis=1)
    lane = lax.broadcasted_iota(jnp.int32, r.shape, 1)
    return jnp.where(lane < MLA_ROPE, r, 0.0), lane


def _mla_q_kernel(cq_ref, g_ref, w_ref, tab_ref, lanes_ref, o_ref, cqn_ref, *, heads_per_step, scale):
    @pl.when(pl.program_id(1) == 0)
    def _():
        c = cq_ref[...].astype(F32)
        ms = jnp.mean(c * c, axis=-1, keepdims=True)
        cqn_ref[...] = (c * lax.rsqrt(ms + RMS_EPS) * g_ref[...]).astype(cqn_ref.dtype)

    y = _dot(cqn_ref[...], w_ref[...])
    tab = tab_ref[...]
    mask_lanes = lanes_ref[...].astype(F32)
    for hh in range(heads_per_step):
        base = hh * HEAD_TILE
        nope = y[:, base:base + MLA_NOPE] * scale
        r, _ = _rope_combine(y[:, base + MLA_NOPE:base + HEAD_TILE] * tab)
        o_ref[:, base:base + MLA_NOPE] = nope.astype(o_ref.dtype)
        o_ref[:, base + MLA_NOPE:base + HEAD_TILE] = (r * scale + mask_lanes).astype(o_ref.dtype)


def _mla_q(z, g, w, tab, lanes_tab, n_heads, q_lora, cq_block, seq):
    m = z.shape[0]
    tm = _tile(m, 1024, 16)
    tm = math.gcd(tm, seq)
    hp = _tile(n_heads, 4, 1)
    nseq = seq // tm
    kern = functools.partial(_mla_q_kernel, heads_per_step=hp,
                             scale=float((MLA_NOPE + MLA_ROPE) ** -0.5 * LOG2E))
    return pl.pallas_call(
        kern,
        out_shape=jax.ShapeDtypeStruct((m, n_heads * HEAD_TILE), BF16),
        grid=(m // tm, n_heads // hp),
        in_specs=[pl.BlockSpec((tm, q_lora), lambda i, j: (i, cq_block)),
                  pl.BlockSpec((1, q_lora), lambda i, j: (0, 0)),
                  pl.BlockSpec((q_lora, hp * HEAD_TILE), lambda i, j: (0, j)),
                  pl.BlockSpec((tm, LANES), lambda i, j: (i % nseq, 0)),
                  pl.BlockSpec((tm, LANES), lambda i, j: (i % nseq, 0))],
        out_specs=pl.BlockSpec((tm, hp * HEAD_TILE), lambda i, j: (i, j)),
        scratch_shapes=[pltpu.VMEM((tm, q_lora), BF16)],
        compiler_params=_params("arbitrary", "arbitrary"),
        name="mla_q",
    )(z, g, w, tab, lanes_tab)


def _mla_kv_kernel(ckv_ref, kr_ref, g_ref, wk_ref, wvt_ref, tab_ref, lanes_ref, k_ref, vt_ref,
                   *, n_heads, t):
    c = ckv_ref[...].astype(F32)
    ms = jnp.mean(c * c, axis=-1, keepdims=True)
    cn = (c * lax.rsqrt(ms + RMS_EPS) * g_ref[...]).astype(BF16)
    kn = _dot(cn, wk_ref[...])
    kr, _ = _rope_combine(kr_ref[...] * tab_ref[...])
    kr = (kr + lanes_ref[...].astype(F32)).astype(k_ref.dtype)
    for h in range(n_heads):
        k_ref[:, h * HEAD_TILE:h * HEAD_TILE + MLA_NOPE] = (
            kn[:, h * MLA_NOPE:(h + 1) * MLA_NOPE].astype(k_ref.dtype))
        k_ref[:, h * HEAD_TILE + MLA_NOPE:(h + 1) * HEAD_TILE] = kr
    vt = _dot_nt(wvt_ref[...], cn).astype(vt_ref.dtype)
    for tt in range(vt_ref.shape[1]):
        vt_ref[0, tt] = vt[:, tt * t:(tt + 1) * t]


def _mla_kv(z, kr, g, wk, wvt, tab, lanes_tab, n_heads, kv_lora, ckv_block, batch, seq, t):
    m = z.shape[0]
    tm = math.gcd(_tile(m, 1024, 16), seq)
    t = min(t, tm)
    nseq = seq // tm
    kern = functools.partial(_mla_kv_kernel, n_heads=n_heads, t=t)
    return pl.pallas_call(
        kern,
        out_shape=(jax.ShapeDtypeStruct((m, n_heads * HEAD_TILE), BF16),
                   jax.ShapeDtypeStruct((batch, seq // t, n_heads * MLA_V_DIM, t), BF16)),
        grid=(m // tm,),
        in_specs=[pl.BlockSpec((tm, kv_lora), lambda i: (i, ckv_block)),
                  pl.BlockSpec((tm, LANES), lambda i: (i, 0)),
                  pl.BlockSpec((1, kv_lora), lambda i: (0, 0)),
                  pl.BlockSpec((kv_lora, n_heads * MLA_NOPE), lambda i: (0, 0)),
                  pl.BlockSpec((n_heads * MLA_V_DIM, kv_lora), lambda i: (0, 0)),
                  pl.BlockSpec((tm, LANES), lambda i: (i % nseq, 0)),
                  pl.BlockSpec((tm, LANES), lambda i: (i % nseq, 0))],
        out_specs=(pl.BlockSpec((tm, n_heads * HEAD_TILE), lambda i: (i, 0)),
                   pl.BlockSpec((1, tm // t, n_heads * MLA_V_DIM, t),
                                lambda i: (i // nseq, i % nseq, 0, 0))),
        compiler_params=_params("arbitrary"),
        name="mla_kv",
    )(z, kr, g, wk, wvt, tab, lanes_tab)


def _col_max(s_t):
    return jnp.max(s_t, axis=0, keepdims=True)


def _score_stage(s_ref, mx_ref, mm, cs, s_t):
    s_ref[mm, :, cs] = s_t
    mx_ref[mm, :, cs] = _col_max(s_t)


def _exp_stage(s_ref, mx_ref, p_ref, mm, cs, m_prev):
    m_new = jnp.maximum(m_prev, mx_ref[mm, :, cs])
    p_ref[mm, :, cs] = jnp.exp2(s_ref[mm, :, cs] - m_new).astype(BF16)
    return m_new, jnp.exp2(m_prev - m_new)


def _acc_stage(p_ref, acc_ref, mm, cs, vt_aug, alpha):
    acc_ref[mm, :, cs] = alpha * acc_ref[mm, :, cs] + _dot(vt_aug, p_ref[mm, :, cs])


def _silu(g):
    return g * jax.nn.sigmoid(g)


def _build_values(vaug_ref, vtm_ref, vt_ref, vtmeta_ref, dv, n_meta):
    vaug_ref[:, 0:dv, :] = vt_ref[0]
    vaug_ref[:, dv:, :] = jnp.ones((vaug_ref.shape[0], ONES_ROWS, vaug_ref.shape[2]), BF16)
    vtm_ref[0:dv, :] = jnp.zeros((dv, vtm_ref.shape[1]), BF16)
    vtm_ref[0:dv, 0:n_meta] = vtmeta_ref[...]
    vtm_ref[dv:, :] = jnp.ones((ONES_ROWS, vtm_ref.shape[1]), BF16)


def _attention_sweep(n_maps, qk, v_blk, meta_scores, vtm, diag_fix, s_refs, mx_refs, p_refs,
                     acc_ref, n_full):
    maps = range(n_maps)
    tq = acc_ref.shape[2]
    cols = [slice(c, c + COL_TILE) for c in range(0, tq, COL_TILE)]

    def score(slot, blk, c, fix=None):
        for mm in maps:
            s_t = qk(mm, blk, cols[c])
            if fix is not None:
                s_t = diag_fix(fix, c, s_t)
            _score_stage(s_refs[slot], mx_refs[slot], mm, cols[c], s_t)

    def expo(slot, c, m, alpha):
        for mm in maps:
            m[mm][c], alpha[mm][c] = _exp_stage(s_refs[slot], mx_refs[slot], p_refs[slot], mm,
                                                cols[c], m[mm][c])

    def accum(slot, vb, c, alpha):
        for mm in maps:
            _acc_stage(p_refs[slot], acc_ref, mm, cols[c], vb, alpha[mm][c])

    def blank():
        return [[None] * len(cols) for _ in maps]

    m, a0, a1 = blank(), blank(), blank()
    for c in range(len(cols)):
        s_meta = [meta_scores(mm, cols[c]) for mm in maps]
        score(0, n_full, c, fix=0)
        score(1, n_full + 1, c, fix=1)
        for mm in maps:
            mx = _col_max(s_meta[mm])
            acc_ref[mm, :, cols[c]] = _dot(vtm, jnp.exp2(s_meta[mm] - mx).astype(BF16))
            m[mm][c] = mx
    for c in range(len(cols)):
        expo(0, c, m, a0)

    def body(i, carry):
        m, a0 = [list(x) for x in carry[0]], [list(x) for x in carry[1]]
        a1 = blank()
        first = i == 0
        vb0 = v_blk(jnp.where(first, n_full, 2 * i - 2))
        vb1 = v_blk(jnp.where(first, n_full + 1, 2 * i - 1))
        for c in range(len(cols)):
            accum(0, vb0, c, a0)
            expo(1, c, m, a1)
            score(0, 2 * i, c)
        for c in range(len(cols)):
            accum(1, vb1, c, a1)
            expo(0, c, m, a0)
            score(1, 2 * i + 1, c)
        return m, a0

    m, a0 = lax.fori_loop(0, n_full // 2, body, (m, a0))
    m, a0 = [list(x) for x in m], [list(x) for x in a0]

    last = jnp.where(n_full == 0, 0, n_full - 2)
    vb0, vb1 = v_blk(last), v_blk(last + 1)
    for c in range(len(cols)):
        accum(0, vb0, c, a0)
        expo(1, c, m, a1)
    for c in range(len(cols)):
        accum(1, vb1, c, a1)


def _diff_attn_kernel(q_ref, k_ref, vt_ref, g_ref, kmeta_ref, vtmeta_ref, eq_ref, ek_ref,
                      ekmeta_ref, core_ref, lq1_ref, lk1_ref, lq2_ref, lk2_ref, subln_ref,
                      o_ref, kaug_ref, kmaug_ref, vaug_ref, vtm_ref, s0_ref, s1_ref, mx0_ref, mx1_ref,
                      p0_ref, p1_ref, acc_ref, *, lambda_init, t, n_meta):
    qi = pl.program_id(2)
    d = DIFF_HEAD_DIM
    dv = DIFF_V_DIM

    @pl.when(qi == 0)
    def _build():
        kmaug_ref[...] = jnp.zeros(kmaug_ref.shape, BF16)
        _build_values(vaug_ref, vtm_ref, vt_ref, vtmeta_ref, dv, n_meta)
        for mm in range(2):
            kaug_ref[mm, :, 0:d] = k_ref[0, :, mm * d:(mm + 1) * d]
            kaug_ref[mm, :, d:2 * d] = ek_ref[0]
            kmaug_ref[mm, 0:n_meta, 0:d] = kmeta_ref[:, mm * d:(mm + 1) * d]
            kmaug_ref[mm, :, d:2 * d] = ekmeta_ref[0]

    q = q_ref[0]
    eq = eq_ref[0]
    qa = [jnp.concatenate([q[:, mm * d:(mm + 1) * d], eq], axis=1) for mm in range(2)]

    def qk(mm, blk, cs):
        off = pl.multiple_of(blk * t, t)
        return _dot_nt(kaug_ref[mm, pl.ds(off, t), :], qa[mm][cs])

    def diag_fix(kd, c, s_t):
        lo = c * COL_TILE - kd * t
        if 0 <= lo < t:
            return s_t + core_ref[0, :, lo:lo + COL_TILE]
        return s_t

    _attention_sweep(2, qk, lambda blk: vaug_ref[blk],
                     lambda mm, cs: _dot_nt(kmaug_ref[mm], qa[mm][cs]), vtm_ref[...], diag_fix,
                     (s0_ref, s1_ref), (mx0_ref, mx1_ref), (p0_ref, p1_ref), acc_ref,
                     qi * DIAG_BLOCKS)

    lam = (jnp.exp(jnp.sum(lq1_ref[...] * lk1_ref[...], axis=-1, keepdims=True))
           - jnp.exp(jnp.sum(lq2_ref[...] * lk2_ref[...], axis=-1, keepdims=True))
           + lambda_init)
    o_t = (acc_ref[0, 0:dv] / acc_ref[0, dv:dv + 1]
           - lam * (acc_ref[1, 0:dv] / acc_ref[1, dv:dv + 1]))
    ms = jnp.mean(o_t * o_t, axis=0, keepdims=True)
    o = (o_t * lax.rsqrt(ms + RMS_EPS)).T
    o = (o * subln_ref[...]) * (1.0 - lambda_init)
    o_ref[0] = (o * _silu(g_ref[0].astype(F32))).astype(o_ref.dtype)


def _diff_attn(z3, vt, z_meta, vt_meta, eq_tab, ek_tab, ekmeta_tab, core_tab, lams, subln,
               n_heads, blk, lambda_init, t, tq):
    b, s, _ = z3.shape
    n_meta = z_meta.shape[0]
    nq = s // tq
    q0, k0, g0 = blk
    vec = pl.BlockSpec((1, DIFF_HEAD_DIM), lambda bb, h, qi: (0, 0))
    kern = functools.partial(_diff_attn_kernel, lambda_init=lambda_init, t=t, n_meta=n_meta)
    return pl.pallas_call(
        kern,
        out_shape=jax.ShapeDtypeStruct((b, s, n_heads * DIFF_V_DIM), BF16),
        grid=(b, n_heads, nq),
        in_specs=[
            pl.BlockSpec((1, tq, HEAD_TILE), lambda bb, h, qi: (bb, qi, q0 + h)),
            pl.BlockSpec((1, s, HEAD_TILE), lambda bb, h, qi: (bb, 0, k0 + h)),
            pl.BlockSpec((1, s // t, DIFF_V_DIM, t), lambda bb, h, qi: (bb, 0, h, 0)),
            pl.BlockSpec((1, tq, DIFF_V_DIM), lambda bb, h, qi: (bb, qi, g0 + h)),
            pl.BlockSpec((n_meta, HEAD_TILE), lambda bb, h, qi: (0, k0 + h)),
            pl.BlockSpec((DIFF_V_DIM, n_meta), lambda bb, h, qi: (h, 0)),
            pl.BlockSpec((1, tq, LANES), lambda bb, h, qi: (h, qi, 0)),
            pl.BlockSpec((1, s, LANES), lambda bb, h, qi: (h, 0, 0)),
            pl.BlockSpec((1, LANES, LANES), lambda bb, h, qi: (h, 0, 0)),
            pl.BlockSpec((1, t, t), lambda bb, h, qi: (h, 0, 0)),
            vec, vec, vec, vec,
            pl.BlockSpec((1, DIFF_V_DIM), lambda bb, h, qi: (0, 0)),
        ],
        out_specs=pl.BlockSpec((1, tq, DIFF_V_DIM), lambda bb, h, qi: (bb, qi, h)),
        scratch_shapes=[
            pltpu.VMEM((2, s, HEAD_TILE), BF16),
            pltpu.VMEM((2, LANES, HEAD_TILE), BF16),
            pltpu.VMEM((s // t, DIFF_V_DIM + ONES_ROWS, t), BF16),
            pltpu.VMEM((DIFF_V_DIM + ONES_ROWS, LANES), BF16),
            pltpu.VMEM((2, t, tq), F32),
            pltpu.VMEM((2, t, tq), F32),
            pltpu.VMEM((2, 1, tq), F32),
            pltpu.VMEM((2, 1, tq), F32),
            pltpu.VMEM((2, t, tq), BF16),
            pltpu.VMEM((2, t, tq), BF16),
            pltpu.VMEM((2, DIFF_V_DIM + ONES_ROWS, tq), F32),
        ],
        compiler_params=_params("arbitrary", "arbitrary", "arbitrary"),
        name="diff_attn",
    )(z3, z3, vt, z3, z_meta, vt_meta, eq_tab, ek_tab, ekmeta_tab, core_tab, *lams, subln)


def _mla_attn_kernel(q_ref, k_ref, vt_ref, g_ref, kmeta_ref, vtmeta_ref, padk_ref,
                     o_ref, kmaug_ref, vaug_ref, vtm_ref, s0_ref, s1_ref, mx0_ref, mx1_ref,
                     p0_ref, p1_ref, acc_ref, *, t, n_meta):
    qi = pl.program_id(2)
    dv = MLA_V_DIM

    @pl.when(qi == 0)
    def _build():
        kmaug_ref[...] = padk_ref[...]
        kmaug_ref[0:n_meta, :] = kmeta_ref[...]
        _build_values(vaug_ref, vtm_ref, vt_ref, vtmeta_ref, dv, n_meta)

    q = q_ref[0]

    def qk(mm, blk, cs):
        off = pl.multiple_of(blk * t, t)
        return _dot_nt(k_ref[0, pl.ds(off, t), :], q[cs])

    _attention_sweep(1, qk, lambda blk: vaug_ref[blk],
                     lambda mm, cs: _dot_nt(kmaug_ref[...], q[cs]), vtm_ref[...],
                     lambda kd, c, s_t: s_t,
                     (s0_ref, s1_ref), (mx0_ref, mx1_ref), (p0_ref, p1_ref), acc_ref,
                     qi * DIAG_BLOCKS)

    o = (acc_ref[0, 0:dv] / acc_ref[0, dv:dv + 1]).T
    o_ref[0] = (o * _silu(g_ref[0].astype(F32))).astype(o_ref.dtype)


def _mla_attn(q3, k3, vt, z3, k_meta, vt_meta, padk, n_heads, g0, t, tq):
    b, s, _ = q3.shape
    n_meta = k_meta.shape[0]
    nq = s // tq
    kern = functools.partial(_mla_attn_kernel, t=t, n_meta=n_meta)
    return pl.pallas_call(
        kern,
        out_shape=jax.ShapeDtypeStruct((b, s, n_heads * MLA_V_DIM), BF16),
        grid=(b, n_heads, nq),
        in_specs=[
            pl.BlockSpec((1, tq, HEAD_TILE), lambda bb, h, qi: (bb, qi, h)),
            pl.BlockSpec((1, s, HEAD_TILE), lambda bb, h, qi: (bb, 0, h)),
            pl.BlockSpec((1, s // t, MLA_V_DIM, t), lambda bb, h, qi: (bb, 0, h, 0)),
            pl.BlockSpec((1, tq, MLA_V_DIM), lambda bb, h, qi: (bb, qi, g0 + h)),
            pl.BlockSpec((n_meta, HEAD_TILE), lambda bb, h, qi: (0, h)),
            pl.BlockSpec((MLA_V_DIM, n_meta), lambda bb, h, qi: (h, 0)),
            pl.BlockSpec((LANES, HEAD_TILE), lambda bb, h, qi: (0, 0)),
        ],
        out_specs=pl.BlockSpec((1, tq, MLA_V_DIM), lambda bb, h, qi: (bb, qi, h)),
        scratch_shapes=[
            pltpu.VMEM((LANES, HEAD_TILE), BF16),
            pltpu.VMEM((s // t, MLA_V_DIM + ONES_ROWS, t), BF16),
            pltpu.VMEM((MLA_V_DIM + ONES_ROWS, LANES), BF16),
            pltpu.VMEM((1, t, tq), F32),
            pltpu.VMEM((1, t, tq), F32),
            pltpu.VMEM((1, 1, tq), F32),
            pltpu.VMEM((1, 1, tq), F32),
            pltpu.VMEM((1, t, tq), BF16),
            pltpu.VMEM((1, t, tq), BF16),
            pltpu.VMEM((1, MLA_V_DIM + ONES_ROWS, tq), F32),
        ],
        compiler_params=_params("arbitrary", "arbitrary", "arbitrary"),
        name="mla_attn",
    )(q3, k3, vt, z3, k_meta, vt_meta, padk)


def _outproj_kernel(a_ref, b_ref, w_ref, x_ref, g_ref, o_ref, *, a_steps, n_steps):
    k = pl.program_id(1)

    @pl.when(k == 0)
    def _():
        o_ref[...] = _dot(a_ref[...], w_ref[...])

    @pl.when((k > 0) & (k < a_steps))
    def _():
        o_ref[...] += _dot(a_ref[...], w_ref[...])

    @pl.when(k >= a_steps)
    def _():
        o_ref[...] += _dot(b_ref[...], w_ref[...])

    @pl.when(k == n_steps - 1)
    def _():
        y = o_ref[...]
        ms = jnp.mean(y * y, axis=-1, keepdims=True)
        o_ref[...] = x_ref[...] + y * lax.rsqrt(ms + RMS_EPS) * g_ref[...]


def _outproj(mix_a, mix_b, w, x2d, g):
    m, ka = mix_a.shape
    kb = mix_b.shape[1]
    n = w.shape[1]
    tm = _tile(m, 512, 16)
    tk = _tile(math.gcd(ka, kb), 512)
    a_steps, b_steps = ka // tk, kb // tk
    kern = functools.partial(_outproj_kernel, a_steps=a_steps, n_steps=a_steps + b_steps)
    return pl.pallas_call(
        kern,
        out_shape=jax.ShapeDtypeStruct((m, n), F32),
        grid=(m // tm, a_steps + b_steps),
        in_specs=[pl.BlockSpec((tm, tk), lambda i, k: (i, jnp.minimum(k, a_steps - 1))),
                  pl.BlockSpec((tm, tk), lambda i, k: (i, jnp.maximum(k - a_steps, 0))),
                  pl.BlockSpec((tk, n), lambda i, k: (k, 0)),
                  pl.BlockSpec((tm, n), lambda i, k: (i, 0)),
                  pl.BlockSpec((1, n), lambda i, k: (0, 0))],
        out_specs=pl.BlockSpec((tm, n), lambda i, k: (i, 0)),
        compiler_params=_params("arbitrary", "arbitrary"),
        name="outproj",
    )(mix_a, mix_b, w, x2d, g)


def _split3(v):
    v = np.asarray(v, np.float32)
    hi = v.astype(NP_BF16)
    r = v - hi.astype(np.float32)
    mid = r.astype(NP_BF16)
    lo = (r - mid.astype(np.float32)).astype(NP_BF16)
    return hi, mid, lo


def _rope_table(pos):
    inv_freq = 1.0 / (ROPE_THETA ** (jnp.arange(0, MLA_ROPE, 2, dtype=F32) / MLA_ROPE))
    ang = pos.astype(F32)[:, None] * inv_freq[None, :]
    cos, sin = jnp.cos(ang), jnp.sin(ang)
    return jnp.concatenate([cos, cos, sin, sin], axis=1)


def _chunk_mask_lanes(seq):
    chunk = np.arange(seq) // CHUNK
    c = np.arange(seq // CHUNK - 1)
    q_lanes = (chunk[:, None] == c[None, :]).astype(np.float32)
    k_lanes = np.where(chunk[:, None] > c[None, :], MASK_VALUE, 0.0).astype(np.float32)
    return q_lanes.astype(NP_BF16), k_lanes.astype(NP_BF16)


def _diff_tables(n_heads, n_meta, seq, t):
    f32 = np.float32
    slopes = (2.0 ** (-8.0 * np.arange(1, n_heads + 1, dtype=f32) / n_heads)).astype(f32)
    slopes = (slopes * f32(LOG2E)).astype(f32)
    pos_q = ((np.arange(seq, dtype=f32) + n_meta)[None, :] * slopes[:, None]).astype(f32)
    pos_m = (np.arange(LANES, dtype=f32)[None, :] * slopes[:, None]).astype(f32)
    qc, kc = _chunk_mask_lanes(seq)
    n_c = qc.shape[1]
    assert 8 + n_c <= LANES
    ones_q = np.ones((n_heads, seq), NP_BF16)
    zeros_q = np.zeros((n_heads, seq), NP_BF16)
    qh, qm, ql = _split3(-pos_q)
    kh, km, kl = _split3(pos_q)
    tail = np.zeros((n_heads, seq, LANES - 8 - n_c), NP_BF16)

    def lanes(cols, chunk_lanes):
        return np.concatenate([np.stack(cols + [zeros_q], -1),
                               np.broadcast_to(chunk_lanes, (n_heads,) + chunk_lanes.shape), tail], -1)

    eq = lanes([qh, qm, ql, ones_q, ones_q, ones_q, ones_q], qc)
    ek = lanes([ones_q, ones_q, ones_q, zeros_q, kh, km, kl], kc)
    mh, mmid, ml = _split3(pos_m)
    valid = (np.arange(LANES) < n_meta)[None, :]
    padmask = np.broadcast_to(np.where(valid, 0.0, MASK_VALUE).astype(f32),
                              (n_heads, LANES)).astype(NP_BF16)
    ones_m = np.ones((n_heads, LANES), NP_BF16)
    ekm = np.concatenate(
        [np.stack([ones_m, ones_m, ones_m, padmask, mh, mmid, ml], -1),
         np.zeros((n_heads, LANES, LANES - 7), NP_BF16)], -1)
    j = np.arange(t)[:, None]
    i = np.arange(t)[None, :]
    same = (j // CHUNK) == (i // CHUNK)
    fut = np.where(same, 2.0 * np.maximum(j - i, 0), 0.0).astype(f32)
    core = (-slopes[:, None, None] * fut[None]).astype(f32)
    return tuple(jnp.asarray(a) for a in (eq, ek, ekm, core))


def _mla_mask_lanes(n_meta, seq):
    qc, kc = _chunk_mask_lanes(seq)
    n_c = qc.shape[1]
    assert MLA_ROPE + 1 + n_c <= LANES
    q = np.zeros((seq, LANES), NP_BF16)
    k = np.zeros((seq, LANES), NP_BF16)
    q[:, MLA_ROPE] = 1.0
    q[:, MLA_ROPE + 1:MLA_ROPE + 1 + n_c] = qc
    k[:, MLA_ROPE + 1:MLA_ROPE + 1 + n_c] = kc
    return jnp.asarray(q), jnp.asarray(k), jnp.zeros((n_meta, LANES), BF16)


def kernel(x, meta_tokens, norm_pre, w_in, diff_lambda_q1, diff_lambda_k1, diff_lambda_q2,
           diff_lambda_k2, diff_subln, mla_norm_q, mla_norm_kv, w_uq, w_ukv, w_out, norm_post):
    assert norm_pre.shape[0] == 1, "single-layer block"
    b, s, d = x.shape
    n_meta = meta_tokens.shape[0]
    mix = w_out.shape[1]
    diff_w = mix // 2
    mla_w = mix - diff_w
    ha = diff_w // DIFF_V_DIM
    hb = mla_w // MLA_V_DIM
    q_lora = mla_norm_q.shape[-1]
    kv_lora = mla_norm_kv.shape[-1]
    t = min(KEY_TILE, s)
    tq = min(QUERY_TILE, s)
    assert s % tq == 0 and tq == DIAG_BLOCKS * t and t % CHUNK == 0 and n_meta <= 16
    lambda_init = 0.8 - 0.6 * math.exp(-0.3 * 0)

    w = w_in[0]
    sizes = (diff_w, diff_w, diff_w, diff_w, q_lora, kv_lora, MLA_ROPE, mla_w)
    offs = np.cumsum((0,) + sizes)
    w_kr, w_mg = (w[:, offs[i]:offs[i + 1]] for i in (6, 7))
    w_all = w.astype(BF16)
    lat = q_lora + kv_lora
    tn = _tile(math.gcd(diff_w, lat), 1024)
    n_main = 3 * diff_w + lat
    assert (3 * diff_w) % q_lora == 0 and (3 * diff_w + q_lora) % kv_lora == 0
    q_blk0 = 0
    k_blk0 = diff_w // HEAD_TILE
    g_blk0 = 2 * diff_w // HEAD_TILE
    cq_block = 3 * diff_w // q_lora
    ckv_block = (3 * diff_w + q_lora) // kv_lora
    colscale = jnp.ones((1, n_main), F32).at[:, :diff_w].set(DIFF_HEAD_DIM ** -0.5 * LOG2E)
    half = MLA_ROPE // 2

    def rot(wr):
        return jnp.concatenate([-wr[:, half:], wr[:, :half]], axis=1)

    w_kr2 = jnp.concatenate([w_kr, rot(w_kr)], axis=1).astype(BF16)
    w_dvt = _transpose_cast(w, 2 * diff_w, diff_w)

    wq = w_uq[0].reshape(q_lora, hb, MLA_NOPE + MLA_ROPE)
    wq_r = wq[:, :, MLA_NOPE:]
    wq_full = jnp.concatenate(
        [wq[:, :, :MLA_NOPE], wq_r, jnp.concatenate([-wq_r[:, :, half:], wq_r[:, :, :half]], -1)],
        axis=-1).reshape(q_lora, hb * HEAD_TILE).astype(BF16)
    wkv = w_ukv[0].reshape(kv_lora, hb, MLA_NOPE + MLA_V_DIM)
    w_kn = wkv[:, :, :MLA_NOPE].reshape(kv_lora, hb * MLA_NOPE).astype(BF16)
    w_vt = wkv[:, :, MLA_NOPE:].reshape(kv_lora, hb * MLA_V_DIM).T.astype(BF16)
    wo = w_out[0].astype(BF16)

    pos = jnp.arange(n_meta + s, dtype=jnp.int32)
    rope_tab = _rope_table(pos)
    tab_meta, tab_seq = rope_tab[:n_meta], rope_tab[n_meta:]
    eq_tab, ek_tab, ekm_tab, core_tab = _diff_tables(ha, n_meta, s, t)
    qlane_tab, klane_tab, klane_meta = _mla_mask_lanes(n_meta, s)
    lane = jnp.arange(HEAD_TILE)
    row = jnp.arange(LANES)
    padk = jnp.where((lane[None, :] == MLA_NOPE + MLA_ROPE) & (row[:, None] >= n_meta),
                     MASK_VALUE, 0.0).astype(BF16)

    x2d = x.reshape(b * s, d)
    u = _prenorm(x2d, norm_pre)
    u_meta = _prenorm(meta_tokens.astype(x.dtype), norm_pre)
    main_tiles = (n_main // tn, 2 * diff_w // tn, diff_w // tn)
    z = _inproj(u, w_all, colscale, tn, *main_tiles)
    z_meta = _inproj(u_meta, w_all, colscale, tn, *main_tiles)
    w_mg16 = w_mg.astype(BF16)
    z_gate, kr = _inproj_gate_kr(u, w_mg16, w_kr2)
    _, kr_meta = _inproj_gate_kr(u_meta, w_mg16, w_kr2)
    vt_a = _v_transposed(w_dvt, u, b, t)
    vt_a_meta = _v_transposed(w_dvt, u_meta, 1, n_meta)[0, 0]

    g_cq = mla_norm_q.astype(F32)
    g_ckv = mla_norm_kv.astype(F32)
    q_b = _mla_q(z, g_cq, wq_full, tab_seq, qlane_tab, hb, q_lora, cq_block, s)
    k_b, vt_b = _mla_kv(z, kr, g_ckv, w_kn, w_vt, tab_seq, klane_tab, hb, kv_lora, ckv_block,
                        b, s, t)
    k_b_meta, vt_b_meta = _mla_kv(z_meta, kr_meta, g_ckv, w_kn, w_vt, tab_meta, klane_meta, hb,
                                  kv_lora, ckv_block, 1, n_meta, n_meta)
    vt_b_meta = vt_b_meta[0, 0]

    z3 = z.reshape(b, s, n_main)
    lams = [v.astype(F32) for v in (diff_lambda_q1, diff_lambda_k1, diff_lambda_q2, diff_lambda_k2)]
    mix_a = _diff_attn(z3, vt_a, z_meta, vt_a_meta, eq_tab, ek_tab, ekm_tab, core_tab, lams,
                       diff_subln.astype(F32), ha, (q_blk0, k_blk0, g_blk0), lambda_init, t, tq)
    mix_b = _mla_attn(q_b.reshape(b, s, hb * HEAD_TILE), k_b.reshape(b, s, hb * HEAD_TILE), vt_b,
                      z_gate.reshape(b, s, mla_w), k_b_meta, vt_b_meta, padk, hb, 0, t, tq)

    out = _outproj(mix_a.reshape(b * s, diff_w), mix_b.reshape(b * s, mla_w), wo, x2d,
                   norm_post.astype(F32))
    return out.reshape(b, s, d)
```

```python
import functools
import math

import jax
import jax.numpy as jnp
import numpy as np
from jax import lax
from jax.experimental import pallas as pl
from jax.experimental.pallas import tpu as pltpu

CHUNK = 64
RMS_EPS = 1e-6
MASK_VALUE = -1e30
DIFF_HEAD_DIM = 128
DIFF_V_DIM = 2 * DIFF_HEAD_DIM
MLA_NOPE = 128
MLA_ROPE = 64
MLA_V_DIM = 128
ROPE_THETA = 10000.0

LANES = 128
HEAD_TILE = 2 * LANES
VMEM_LIMIT = 56 * 1024 * 1024

KEY_TILE = 512
DIAG_BLOCKS = 2
QUERY_TILE = DIAG_BLOCKS * KEY_TILE
COL_TILE = 256
MLA_HEADS_PER_STEP = 2
M_INIT = -1e38
ONES_ROWS = 16
LOG2E = math.log2(math.e)

F32 = jnp.float32
BF16 = jnp.bfloat16
NP_BF16 = np.dtype(jnp.bfloat16)


def _params(*sem):
    return pltpu.CompilerParams(dimension_semantics=sem, vmem_limit_bytes=VMEM_LIMIT)


def _tile(dim, target, mult=LANES):
    if dim <= target:
        return dim
    t = (target // mult) * mult
    while t > mult and dim % t:
        t -= mult
    assert dim % t == 0, (dim, target)
    return t


def _dot(a, b):
    return jnp.dot(a, b, preferred_element_type=F32)


def _dot_nt(a, b):
    return lax.dot_general(a, b, (((1,), (1,)), ((), ())), preferred_element_type=F32)


def _prenorm_kernel(x_ref, g_ref, o_ref):
    x = x_ref[...]
    ms = jnp.mean(x * x, axis=-1, keepdims=True)
    o_ref[...] = (x * lax.rsqrt(ms + RMS_EPS) * g_ref[...]).astype(o_ref.dtype)


def _prenorm(x2d, g):
    m, d = x2d.shape
    tm = _tile(m, 256, 8)
    return pl.pallas_call(
        _prenorm_kernel,
        out_shape=jax.ShapeDtypeStruct((m, d), BF16),
        grid=(m // tm,),
        in_specs=[pl.BlockSpec((tm, d), lambda i: (i, 0)),
                  pl.BlockSpec((1, d), lambda i: (0, 0))],
        out_specs=pl.BlockSpec((tm, d), lambda i: (i, 0)),
        compiler_params=_params("arbitrary"),
        name="prenorm",
    )(x2d, g)


def _inproj_kernel(u_ref, w_ref, s_ref, z_ref):
    z_ref[...] = (_dot(u_ref[...], w_ref[...]) * s_ref[...]).astype(z_ref.dtype)


def _inproj_kr_kernel(u_ref, w_ref, wkr_ref, z_ref, kr_ref):
    u = u_ref[...]
    z_ref[...] = _dot(u, w_ref[...]).astype(z_ref.dtype)

    @pl.when(pl.program_id(1) == 0)
    def _():
        kr_ref[...] = _dot(u, wkr_ref[...])


def _inproj(u, w, colscale, tn, n_tiles, skip_at, skip_tiles):
    m, d = u.shape
    tm = _tile(m, 1024, 16)

    def wcol(i, j):
        return 0, jnp.where(j < skip_at, j, j + skip_tiles)

    return pl.pallas_call(
        _inproj_kernel,
        out_shape=jax.ShapeDtypeStruct((m, n_tiles * tn), BF16),
        grid=(m // tm, n_tiles),
        in_specs=[pl.BlockSpec((tm, d), lambda i, j: (i, 0)),
                  pl.BlockSpec((d, tn), wcol),
                  pl.BlockSpec((1, tn), lambda i, j: (0, j))],
        out_specs=pl.BlockSpec((tm, tn), lambda i, j: (i, j)),
        compiler_params=_params("arbitrary", "arbitrary"),
        name="inproj",
    )(u, w, colscale)


def _inproj_gate_kr(u, w, wkr):
    m, d = u.shape
    n = w.shape[1]
    tm = _tile(m, 1024, 16)
    tn = _tile(n, 1024)
    return pl.pallas_call(
        _inproj_kr_kernel,
        out_shape=(jax.ShapeDtypeStruct((m, n), BF16),
                   jax.ShapeDtypeStruct((m, LANES), F32)),
        grid=(m // tm, n // tn),
        in_specs=[pl.BlockSpec((tm, d), lambda i, j: (i, 0)),
                  pl.BlockSpec((d, tn), lambda i, j: (0, j)),
                  pl.BlockSpec((d, LANES), lambda i, j: (0, 0))],
        out_specs=(pl.BlockSpec((tm, tn), lambda i, j: (i, j)),
                   pl.BlockSpec((tm, LANES), lambda i, j: (i, 0))),
        compiler_params=_params("arbitrary", "arbitrary"),
        name="inproj_gate",
    )(u, w, wkr)


def _transpose_cast_kernel(w_ref, o_ref):
    o_ref[...] = w_ref[...].T.astype(o_ref.dtype)


def _transpose_cast(w2d, col0, n):
    d = w2d.shape[0]
    tb = _tile(math.gcd(math.gcd(d, n), col0), 512)
    return pl.pallas_call(
        _transpose_cast_kernel,
        out_shape=jax.ShapeDtypeStruct((n, d), BF16),
        grid=(n // tb, d // tb),
        in_specs=[pl.BlockSpec((tb, tb), lambda i, j: (j, col0 // tb + i))],
        out_specs=pl.BlockSpec((tb, tb), lambda i, j: (i, j)),
        compiler_params=_params("arbitrary", "arbitrary"),
        name="wv_transpose",
    )(w2d)


def _gate_weights_kernel(a_ref, b_ref, mg_ref, kr_ref):
    a = a_ref[...]
    mg_ref[...] = jnp.concatenate([a[:, MLA_ROPE:], b_ref[:, :LANES - MLA_ROPE]], axis=1).astype(
        mg_ref.dtype)

    @pl.when(pl.program_id(0) == 0)
    def _():
        half = MLA_ROPE // 2
        kr = a[:, :MLA_ROPE]
        kr_ref[...] = jnp.concatenate([kr, -kr[:, half:], kr[:, :half]], axis=1).astype(kr_ref.dtype)


def _gate_weights(w2d, col0, n):
    d = w2d.shape[0]
    assert col0 % LANES == 0 and n % LANES == 0 and 2 * MLA_ROPE == LANES
    c0 = col0 // LANES
    return pl.pallas_call(
        _gate_weights_kernel,
        out_shape=(jax.ShapeDtypeStruct((d, n), BF16), jax.ShapeDtypeStruct((d, LANES), BF16)),
        grid=(n // LANES,),
        in_specs=[pl.BlockSpec((d, LANES), lambda j: (0, c0 + j)),
                  pl.BlockSpec((d, LANES), lambda j: (0, c0 + j + 1))],
        out_specs=(pl.BlockSpec((d, LANES), lambda j: (0, j)),
                   pl.BlockSpec((d, LANES), lambda j: (0, 0))),
        compiler_params=_params("arbitrary"),
        name="gate_weights",
    )(w2d, w2d)


def _vt_kernel(w_ref, u_ref, o_ref):
    o_ref[0, 0] = _dot_nt(w_ref[...], u_ref[...]).astype(o_ref.dtype)


def _v_transposed(w_t, u, batch, t):
    n, d = w_t.shape
    m = u.shape[0]
    nt = m // (batch * t)
    return pl.pallas_call(
        _vt_kernel,
        out_shape=jax.ShapeDtypeStruct((batch, nt, n, t), BF16),
        grid=(m // t,),
        in_specs=[pl.BlockSpec((n, d), lambda i: (0, 0)),
                  pl.BlockSpec((t, d), lambda i: (i, 0))],
        out_specs=pl.BlockSpec((1, 1, n, t), lambda i: (i // nt, i % nt, 0, 0)),
        compiler_params=_params("arbitrary"),
        name="inproj_vt",
    )(w_t, u)


def _rope_combine(t):
    r = t + pltpu.roll(t, MLA_ROPE, axis=1)
    lane = lax.broadcasted_iota(jnp.int32, r.shape, 1)
    return jnp.where(lane < MLA_ROPE, r, 0.0), lane


def _mla_q_kernel(cq_ref, g_ref, w_ref, tab_ref, lanes_ref, o_ref, cqn_ref, *, heads_per_step, scale):
    @pl.when(pl.program_id(1) == 0)
    def _():
        c = cq_ref[...].astype(F32)
        ms = jnp.mean(c * c, axis=-1, keepdims=True)
        cqn_ref[...] = (c * lax.rsqrt(ms + RMS_EPS) * g_ref[...]).astype(cqn_ref.dtype)

    y = _dot(cqn_ref[...], w_ref[...])
    tab = tab_ref[...]
    mask_lanes = lanes_ref[...].astype(F32)
    for hh in range(heads_per_step):
        base = hh * HEAD_TILE
        nope = y[:, base:base + MLA_NOPE] * scale
        r, _ = _rope_combine(y[:, base + MLA_NOPE:base + HEAD_TILE] * tab)
        o_ref[:, base:base + MLA_NOPE] = nope.astype(o_ref.dtype)
        o_ref[:, base + MLA_NOPE:base + HEAD_TILE] = (r * scale + mask_lanes).astype(o_ref.dtype)


def _mla_q(z, g, w, tab, lanes_tab, n_heads, q_lora, cq_block, seq):
    m = z.shape[0]
    tm = _tile(m, 1024, 16)
    tm = math.gcd(tm, seq)
    hp = _tile(n_heads, 4, 1)
    nseq = seq // tm
    kern = functools.partial(_mla_q_kernel, heads_per_step=hp,
                             scale=float((MLA_NOPE + MLA_ROPE) ** -0.5 * LOG2E))
    return pl.pallas_call(
        kern,
        out_shape=jax.ShapeDtypeStruct((m, n_heads * HEAD_TILE), BF16),
        grid=(m // tm, n_heads // hp),
        in_specs=[pl.BlockSpec((tm, q_lora), lambda i, j: (i, cq_block)),
                  pl.BlockSpec((1, q_lora), lambda i, j: (0, 0)),
                  pl.BlockSpec((q_lora, hp * HEAD_TILE), lambda i, j: (0, j)),
                  pl.BlockSpec((tm, LANES), lambda i, j: (i % nseq, 0)),
                  pl.BlockSpec((tm, LANES), lambda i, j: (i % nseq, 0))],
        out_specs=pl.BlockSpec((tm, hp * HEAD_TILE), lambda i, j: (i, j)),
        scratch_shapes=[pltpu.VMEM((tm, q_lora), BF16)],
        compiler_params=_params("arbitrary", "arbitrary"),
        name="mla_q",
    )(z, g, w, tab, lanes_tab)


def _mla_kv_kernel(ckv_ref, kr_ref, g_ref, wk_ref, wvt_ref, tab_ref, lanes_ref, k_ref, vt_ref,
                   *, n_heads, t):
    c = ckv_ref[...].astype(F32)
    ms = jnp.mean(c * c, axis=-1, keepdims=True)
    cn = (c * lax.rsqrt(ms + RMS_EPS) * g_ref[...]).astype(BF16)
    kn = _dot(cn, wk_ref[...])
    kr, _ = _rope_combine(kr_ref[...] * tab_ref[...])
    kr = (kr + lanes_ref[...].astype(F32)).astype(k_ref.dtype)
    for h in range(n_heads):
        k_ref[:, h * HEAD_TILE:h * HEAD_TILE + MLA_NOPE] = (
            kn[:, h * MLA_NOPE:(h + 1) * MLA_NOPE].astype(k_ref.dtype))
        k_ref[:, h * HEAD_TILE + MLA_NOPE:(h + 1) * HEAD_TILE] = kr
    vt = _dot_nt(wvt_ref[...], cn).astype(vt_ref.dtype)
    for tt in range(vt_ref.shape[1]):
        vt_ref[0, tt] = vt[:, tt * t:(tt + 1) * t]


def _mla_kv(z, kr, g, wk, wvt, tab, lanes_tab, n_heads, kv_lora, ckv_block, batch, seq, t):
    m = z.shape[0]
    tm = math.gcd(_tile(m, 1024, 16), seq)
    t = min(t, tm)
    nseq = seq // tm
    kern = functools.partial(_mla_kv_kernel, n_heads=n_heads, t=t)
    return pl.pallas_call(
        kern,
        out_shape=(jax.ShapeDtypeStruct((m, n_heads * HEAD_TILE), BF16),
                   jax.ShapeDtypeStruct((batch, seq // t, n_heads * MLA_V_DIM, t), BF16)),
        grid=(m // tm,),
        in_specs=[pl.BlockSpec((tm, kv_lora), lambda i: (i, ckv_block)),
                  pl.BlockSpec((tm, LANES), lambda i: (i, 0)),
                  pl.BlockSpec((1, kv_lora), lambda i: (0, 0)),
                  pl.BlockSpec((kv_lora, n_heads * MLA_NOPE), lambda i: (0, 0)),
                  pl.BlockSpec((n_heads * MLA_V_DIM, kv_lora), lambda i: (0, 0)),
                  pl.BlockSpec((tm, LANES), lambda i: (i % nseq, 0)),
                  pl.BlockSpec((tm, LANES), lambda i: (i % nseq, 0))],
        out_specs=(pl.BlockSpec((tm, n_heads * HEAD_TILE), lambda i: (i, 0)),
                   pl.BlockSpec((1, tm // t, n_heads * MLA_V_DIM, t),
                                lambda i: (i // nseq, i % nseq, 0, 0))),
        compiler_params=_params("arbitrary"),
        name="mla_kv",
    )(z, kr, g, wk, wvt, tab, lanes_tab)


def _col_max(s_t):
    return jnp.max(s_t, axis=0, keepdims=True)


def _score_stage(s_ref, mx_ref, mm, cs, s_t):
    s_ref[mm, :, cs] = s_t
    mx_ref[mm, :, cs] = _col_max(s_t)


def _exp_stage(s_ref, mx_ref, p_ref, mm, cs, m_prev):
    m_new = jnp.maximum(m_prev, mx_ref[mm, :, cs])
    p_ref[mm, :, cs] = jnp.exp2(s_ref[mm, :, cs] - m_new).astype(BF16)
    return m_new, jnp.exp2(m_prev - m_new)


def _acc_stage(p_ref, acc_ref, mm, cs, vt_aug, alpha):
    acc_ref[mm, :, cs] = alpha * acc_ref[mm, :, cs] + _dot(vt_aug, p_ref[mm, :, cs])


def _silu(g):
    return g * jax.nn.sigmoid(g)


def _build_values(vaug_ref, vtm_ref, vt, vt_meta, dv, n_meta):
    vaug_ref[:, 0:dv, :] = vt
    vaug_ref[:, dv:, :] = jnp.ones((vaug_ref.shape[0], ONES_ROWS, vaug_ref.shape[2]), BF16)
    vtm_ref[0:dv, :] = jnp.zeros((dv, vtm_ref.shape[1]), BF16)
    vtm_ref[0:dv, 0:n_meta] = vt_meta
    vtm_ref[dv:, :] = jnp.ones((ONES_ROWS, vtm_ref.shape[1]), BF16)


def _attention_sweep(n_maps, qk, v_blk, meta_scores, vtm, diag_fix, s_refs, mx_refs, p_refs,
                     acc_ref, n_full):
    maps = range(n_maps)
    tq = acc_ref.shape[2]
    cols = [slice(c, c + COL_TILE) for c in range(0, tq, COL_TILE)]

    def score(slot, blk, c, fix=None):
        for mm in maps:
            s_t = qk(mm, blk, cols[c])
            if fix is not None:
                s_t = diag_fix(fix, c, s_t)
            _score_stage(s_refs[slot], mx_refs[slot], mm, cols[c], s_t)

    def expo(slot, c, m, alpha):
        for mm in maps:
            m[mm][c], alpha[mm][c] = _exp_stage(s_refs[slot], mx_refs[slot], p_refs[slot], mm,
                                                cols[c], m[mm][c])

    def accum(slot, vb, c, alpha):
        for mm in maps:
            _acc_stage(p_refs[slot], acc_ref, mm, cols[c], vb[mm], alpha[mm][c])

    def blank():
        return [[None] * len(cols) for _ in maps]

    m, a0, a1 = blank(), blank(), blank()
    for c in range(len(cols)):
        s_meta = [meta_scores(mm, cols[c]) for mm in maps]
        score(0, n_full, c, fix=0)
        score(1, n_full + 1, c, fix=1)
        for mm in maps:
            mx = _col_max(s_meta[mm])
            acc_ref[mm, :, cols[c]] = _dot(vtm[mm], jnp.exp2(s_meta[mm] - mx).astype(BF16))
            m[mm][c] = mx
    for c in range(len(cols)):
        expo(0, c, m, a0)

    def body(i, carry):
        m, a0 = [list(x) for x in carry[0]], [list(x) for x in carry[1]]
        a1 = blank()
        first = i == 0
        vb0 = v_blk(jnp.where(first, n_full, 2 * i - 2))
        vb1 = v_blk(jnp.where(first, n_full + 1, 2 * i - 1))
        for c in range(len(cols)):
            accum(0, vb0, c, a0)
            expo(1, c, m, a1)
            score(0, 2 * i, c)
        for c in range(len(cols)):
            accum(1, vb1, c, a1)
            expo(0, c, m, a0)
            score(1, 2 * i + 1, c)
        return m, a0

    m, a0 = lax.fori_loop(0, n_full // 2, body, (m, a0))
    m, a0 = [list(x) for x in m], [list(x) for x in a0]

    last = jnp.where(n_full == 0, 0, n_full - 2)
    vb0, vb1 = v_blk(last), v_blk(last + 1)
    for c in range(len(cols)):
        accum(0, vb0, c, a0)
        expo(1, c, m, a1)
    for c in range(len(cols)):
        accum(1, vb1, c, a1)


def _diff_attn_kernel(q_ref, k_ref, vt_ref, g_ref, kmeta_ref, vtmeta_ref, eq_ref, ek_ref,
                      ekmeta_ref, core_ref, lq1_ref, lk1_ref, lq2_ref, lk2_ref, subln_ref,
                      o_ref, kaug_ref, kmaug_ref, vaug_ref, vtm_ref, s0_ref, s1_ref, mx0_ref, mx1_ref,
                      p0_ref, p1_ref, acc_ref, *, lambda_init, t, n_meta):
    qi = pl.program_id(2)
    d = DIFF_HEAD_DIM
    dv = DIFF_V_DIM

    @pl.when(qi == 0)
    def _build():
        kmaug_ref[...] = jnp.zeros(kmaug_ref.shape, BF16)
        _build_values(vaug_ref, vtm_ref, vt_ref[0], vtmeta_ref[...], dv, n_meta)
        for mm in range(2):
            kaug_ref[mm, :, 0:d] = k_ref[0, :, mm * d:(mm + 1) * d]
            kaug_ref[mm, :, d:2 * d] = ek_ref[0]
            kmaug_ref[mm, 0:n_meta, 0:d] = kmeta_ref[:, mm * d:(mm + 1) * d]
            kmaug_ref[mm, :, d:2 * d] = ekmeta_ref[0]

    q = q_ref[0]
    eq = eq_ref[0]
    qa = [jnp.concatenate([q[:, mm * d:(mm + 1) * d], eq], axis=1) for mm in range(2)]

    def qk(mm, blk, cs):
        off = pl.multiple_of(blk * t, t)
        return _dot_nt(kaug_ref[mm, pl.ds(off, t), :], qa[mm][cs])

    def diag_fix(kd, c, s_t):
        lo = c * COL_TILE - kd * t
        if 0 <= lo < t:
            return s_t + core_ref[0, :, lo:lo + COL_TILE]
        return s_t

    def v_blk(blk):
        vb = vaug_ref[blk]
        return [vb, vb]

    _attention_sweep(2, qk, v_blk, lambda mm, cs: _dot_nt(kmaug_ref[mm], qa[mm][cs]),
                     [vtm_ref[...]] * 2, diag_fix,
                     (s0_ref, s1_ref), (mx0_ref, mx1_ref), (p0_ref, p1_ref), acc_ref,
                     qi * DIAG_BLOCKS)

    lam = (jnp.exp(jnp.sum(lq1_ref[...] * lk1_ref[...], axis=-1, keepdims=True))
           - jnp.exp(jnp.sum(lq2_ref[...] * lk2_ref[...], axis=-1, keepdims=True))
           + lambda_init)
    o_t = (acc_ref[0, 0:dv] / acc_ref[0, dv:dv + 1]
           - lam * (acc_ref[1, 0:dv] / acc_ref[1, dv:dv + 1]))
    ms = jnp.mean(o_t * o_t, axis=0, keepdims=True)
    o = (o_t * lax.rsqrt(ms + RMS_EPS)).T
    o = (o * subln_ref[...]) * (1.0 - lambda_init)
    o_ref[0] = (o * _silu(g_ref[0].astype(F32))).astype(o_ref.dtype)


def _diff_attn(z3, vt, z_meta, vt_meta, eq_tab, ek_tab, ekmeta_tab, core_tab, lams, subln,
               n_heads, blk, lambda_init, t, tq):
    b, s, _ = z3.shape
    n_meta = z_meta.shape[0]
    nq = s // tq
    q0, k0, g0 = blk
    vec = pl.BlockSpec((1, DIFF_HEAD_DIM), lambda bb, h, qi: (0, 0))
    kern = functools.partial(_diff_attn_kernel, lambda_init=lambda_init, t=t, n_meta=n_meta)
    return pl.pallas_call(
        kern,
        out_shape=jax.ShapeDtypeStruct((b, s, n_heads * DIFF_V_DIM), BF16),
        grid=(b, n_heads, nq),
        in_specs=[
            pl.BlockSpec((1, tq, HEAD_TILE), lambda bb, h, qi: (bb, qi, q0 + h)),
            pl.BlockSpec((1, s, HEAD_TILE), lambda bb, h, qi: (bb, 0, k0 + h)),
            pl.BlockSpec((1, s // t, DIFF_V_DIM, t), lambda bb, h, qi: (bb, 0, h, 0)),
            pl.BlockSpec((1, tq, DIFF_V_DIM), lambda bb, h, qi: (bb, qi, g0 + h)),
            pl.BlockSpec((n_meta, HEAD_TILE), lambda bb, h, qi: (0, k0 + h)),
            pl.BlockSpec((DIFF_V_DIM, n_meta), lambda bb, h, qi: (h, 0)),
            pl.BlockSpec((1, tq, LANES), lambda bb, h, qi: (h, qi, 0)),
            pl.BlockSpec((1, s, LANES), lambda bb, h, qi: (h, 0, 0)),
            pl.BlockSpec((1, LANES, LANES), lambda bb, h, qi: (h, 0, 0)),
            pl.BlockSpec((1, t, t), lambda bb, h, qi: (h, 0, 0)),
            vec, vec, vec, vec,
            pl.BlockSpec((1, DIFF_V_DIM), lambda bb, h, qi: (0, 0)),
        ],
        out_specs=pl.BlockSpec((1, tq, DIFF_V_DIM), lambda bb, h, qi: (bb, qi, h)),
        scratch_shapes=[
            pltpu.VMEM((2, s, HEAD_TILE), BF16),
            pltpu.VMEM((2, LANES, HEAD_TILE), BF16),
            pltpu.VMEM((s // t, DIFF_V_DIM + ONES_ROWS, t), BF16),
            pltpu.VMEM((DIFF_V_DIM + ONES_ROWS, LANES), BF16),
            pltpu.VMEM((2, t, tq), F32),
            pltpu.VMEM((2, t, tq), F32),
            pltpu.VMEM((2, 1, tq), F32),
            pltpu.VMEM((2, 1, tq), F32),
            pltpu.VMEM((2, t, tq), BF16),
            pltpu.VMEM((2, t, tq), BF16),
            pltpu.VMEM((2, DIFF_V_DIM + ONES_ROWS, tq), F32),
        ],
        compiler_params=_params("arbitrary", "arbitrary", "arbitrary"),
        name="diff_attn",
    )(z3, z3, vt, z3, z_meta, vt_meta, eq_tab, ek_tab, ekmeta_tab, core_tab, *lams, subln)


def _mla_attn_kernel(q_ref, k_ref, vt_ref, g_ref, kmeta_ref, vtmeta_ref, padk_ref,
                     o_ref, kmaug_ref, vaug_ref, vtm_ref, s0_ref, s1_ref, mx0_ref, mx1_ref,
                     p0_ref, p1_ref, acc_ref, *, t, n_meta, hp):
    qi = pl.program_id(2)
    dv = MLA_V_DIM
    heads = range(hp)

    @pl.when(qi == 0)
    def _build():
        for hh in heads:
            kmaug_ref[hh] = padk_ref[...]
            kmaug_ref[hh, 0:n_meta, :] = kmeta_ref[:, hh * HEAD_TILE:(hh + 1) * HEAD_TILE]
            _build_values(vaug_ref.at[hh], vtm_ref.at[hh], vt_ref[0, :, hh * dv:(hh + 1) * dv, :],
                          vtmeta_ref[hh * dv:(hh + 1) * dv, :], dv, n_meta)

    q = [q_ref[0, :, hh * HEAD_TILE:(hh + 1) * HEAD_TILE] for hh in heads]

    def qk(hh, blk, cs):
        off = pl.multiple_of(blk * t, t)
        return _dot_nt(k_ref[0, pl.ds(off, t), hh * HEAD_TILE:(hh + 1) * HEAD_TILE], q[hh][cs])

    _attention_sweep(hp, qk, lambda blk: [vaug_ref[hh, blk] for hh in heads],
                     lambda hh, cs: _dot_nt(kmaug_ref[hh], q[hh][cs]),
                     [vtm_ref[hh] for hh in heads], lambda kd, c, s_t: s_t,
                     (s0_ref, s1_ref), (mx0_ref, mx1_ref), (p0_ref, p1_ref), acc_ref,
                     qi * DIAG_BLOCKS)

    for hh in heads:
        o = (acc_ref[hh, 0:dv] / acc_ref[hh, dv:dv + 1]).T
        g = g_ref[0, :, hh * dv:(hh + 1) * dv].astype(F32)
        o_ref[0, :, hh * dv:(hh + 1) * dv] = (o * _silu(g)).astype(o_ref.dtype)


def _mla_attn(q3, k3, vt, z3, k_meta, vt_meta, padk, n_heads, g0, t, tq):
    b, s, _ = q3.shape
    n_meta = k_meta.shape[0]
    nq = s // tq
    hp = _tile(n_heads, MLA_HEADS_PER_STEP, 1)
    kern = functools.partial(_mla_attn_kernel, t=t, n_meta=n_meta, hp=hp)
    return pl.pallas_call(
        kern,
        out_shape=jax.ShapeDtypeStruct((b, s, n_heads * MLA_V_DIM), BF16),
        grid=(b, n_heads // hp, nq),
        in_specs=[
            pl.BlockSpec((1, tq, hp * HEAD_TILE), lambda bb, h, qi: (bb, qi, h)),
            pl.BlockSpec((1, s, hp * HEAD_TILE), lambda bb, h, qi: (bb, 0, h)),
            pl.BlockSpec((1, s // t, hp * MLA_V_DIM, t), lambda bb, h, qi: (bb, 0, h, 0)),
            pl.BlockSpec((1, tq, hp * MLA_V_DIM), lambda bb, h, qi: (bb, qi, g0 + h)),
            pl.BlockSpec((n_meta, hp * HEAD_TILE), lambda bb, h, qi: (0, h)),
            pl.BlockSpec((hp * MLA_V_DIM, n_meta), lambda bb, h, qi: (h, 0)),
            pl.BlockSpec((LANES, HEAD_TILE), lambda bb, h, qi: (0, 0)),
        ],
        out_specs=pl.BlockSpec((1, tq, hp * MLA_V_DIM), lambda bb, h, qi: (bb, qi, h)),
        scratch_shapes=[
            pltpu.VMEM((hp, LANES, HEAD_TILE), BF16),
            pltpu.VMEM((hp, s // t, MLA_V_DIM + ONES_ROWS, t), BF16),
            pltpu.VMEM((hp, MLA_V_DIM + ONES_ROWS, LANES), BF16),
            pltpu.VMEM((hp, t, tq), F32),
            pltpu.VMEM((hp, t, tq), F32),
            pltpu.VMEM((hp, 1, tq), F32),
            pltpu.VMEM((hp, 1, tq), F32),
            pltpu.VMEM((hp, t, tq), BF16),
            pltpu.VMEM((hp, t, tq), BF16),
            pltpu.VMEM((hp, MLA_V_DIM + ONES_ROWS, tq), F32),
        ],
        compiler_params=_params("arbitrary", "arbitrary", "arbitrary"),
        name="mla_attn",
    )(q3, k3, vt, z3, k_meta, vt_meta, padk)


def _outproj_kernel(a_ref, b_ref, w_ref, x_ref, g_ref, o_ref, *, a_steps, n_steps):
    k = pl.program_id(1)

    @pl.when(k == 0)
    def _():
        o_ref[...] = _dot(a_ref[...], w_ref[...])

    @pl.when((k > 0) & (k < a_steps))
    def _():
        o_ref[...] += _dot(a_ref[...], w_ref[...])

    @pl.when(k >= a_steps)
    def _():
        o_ref[...] += _dot(b_ref[...], w_ref[...])

    @pl.when(k == n_steps - 1)
    def _():
        y = o_ref[...]
        ms = jnp.mean(y * y, axis=-1, keepdims=True)
        o_ref[...] = x_ref[...] + y * lax.rsqrt(ms + RMS_EPS) * g_ref[...]


def _outproj(mix_a, mix_b, w, x2d, g):
    m, ka = mix_a.shape
    kb = mix_b.shape[1]
    n = w.shape[1]
    tm = _tile(m, 512, 16)
    tk = _tile(math.gcd(ka, kb), 512)
    a_steps, b_steps = ka // tk, kb // tk
    kern = functools.partial(_outproj_kernel, a_steps=a_steps, n_steps=a_steps + b_steps)
    return pl.pallas_call(
        kern,
        out_shape=jax.ShapeDtypeStruct((m, n), F32),
        grid=(m // tm, a_steps + b_steps),
        in_specs=[pl.BlockSpec((tm, tk), lambda i, k: (i, jnp.minimum(k, a_steps - 1))),
                  pl.BlockSpec((tm, tk), lambda i, k: (i, jnp.maximum(k - a_steps, 0))),
                  pl.BlockSpec((tk, n), lambda i, k: (k, 0)),
                  pl.BlockSpec((tm, n), lambda i, k: (i, 0)),
                  pl.BlockSpec((1, n), lambda i, k: (0, 0))],
        out_specs=pl.BlockSpec((tm, n), lambda i, k: (i, 0)),
        compiler_params=_params("arbitrary", "arbitrary"),
        name="outproj",
    )(mix_a, mix_b, w, x2d, g)


def _split3(v):
    v = np.asarray(v, np.float32)
    hi = v.astype(NP_BF16)
    r = v - hi.astype(np.float32)
    mid = r.astype(NP_BF16)
    lo = (r - mid.astype(np.float32)).astype(NP_BF16)
    return hi, mid, lo


def _rope_table(pos):
    inv_freq = 1.0 / (ROPE_THETA ** (jnp.arange(0, MLA_ROPE, 2, dtype=F32) / MLA_ROPE))
    ang = pos.astype(F32)[:, None] * inv_freq[None, :]
    cos, sin = jnp.cos(ang), jnp.sin(ang)
    return jnp.concatenate([cos, cos, sin, sin], axis=1)


def _chunk_mask_lanes(seq):
    chunk = np.arange(seq) // CHUNK
    c = np.arange(seq // CHUNK - 1)
    q_lanes = (chunk[:, None] == c[None, :]).astype(np.float32)
    k_lanes = np.where(chunk[:, None] > c[None, :], MASK_VALUE, 0.0).astype(np.float32)
    return q_lanes.astype(NP_BF16), k_lanes.astype(NP_BF16)


def _diff_tables(n_heads, n_meta, seq, t):
    f32 = np.float32
    slopes = (2.0 ** (-8.0 * np.arange(1, n_heads + 1, dtype=f32) / n_heads)).astype(f32)
    slopes = (slopes * f32(LOG2E)).astype(f32)
    pos_q = ((np.arange(seq, dtype=f32) + n_meta)[None, :] * slopes[:, None]).astype(f32)
    pos_m = (np.arange(LANES, dtype=f32)[None, :] * slopes[:, None]).astype(f32)
    qc, kc = _chunk_mask_lanes(seq)
    n_c = qc.shape[1]
    assert 8 + n_c <= LANES
    ones_q = np.ones((n_heads, seq), NP_BF16)
    zeros_q = np.zeros((n_heads, seq), NP_BF16)
    qh, qm, ql = _split3(-pos_q)
    kh, km, kl = _split3(pos_q)
    tail = np.zeros((n_heads, seq, LANES - 8 - n_c), NP_BF16)

    def lanes(cols, chunk_lanes):
        return np.concatenate([np.stack(cols + [zeros_q], -1),
                               np.broadcast_to(chunk_lanes, (n_heads,) + chunk_lanes.shape), tail], -1)

    eq = lanes([qh, qm, ql, ones_q, ones_q, ones_q, ones_q], qc)
    ek = lanes([ones_q, ones_q, ones_q, zeros_q, kh, km, kl], kc)
    mh, mmid, ml = _split3(pos_m)
    valid = (np.arange(LANES) < n_meta)[None, :]
    padmask = np.broadcast_to(np.where(valid, 0.0, MASK_VALUE).astype(f32),
                              (n_heads, LANES)).astype(NP_BF16)
    ones_m = np.ones((n_heads, LANES), NP_BF16)
    ekm = np.concatenate(
        [np.stack([ones_m, ones_m, ones_m, padmask, mh, mmid, ml], -1),
         np.zeros((n_heads, LANES, LANES - 7), NP_BF16)], -1)
    j = np.arange(t)[:, None]
    i = np.arange(t)[None, :]
    same = (j // CHUNK) == (i // CHUNK)
    fut = np.where(same, 2.0 * np.maximum(j - i, 0), 0.0).astype(f32)
    core = (-slopes[:, None, None] * fut[None]).astype(f32)
    return tuple(jnp.asarray(a) for a in (eq, ek, ekm, core))


def _mla_mask_lanes(n_meta, seq):
    qc, kc = _chunk_mask_lanes(seq)
    n_c = qc.shape[1]
    assert MLA_ROPE + 1 + n_c <= LANES
    q = np.zeros((seq, LANES), NP_BF16)
    k = np.zeros((seq, LANES), NP_BF16)
    q[:, MLA_ROPE] = 1.0
    q[:, MLA_ROPE + 1:MLA_ROPE + 1 + n_c] = qc
    k[:, MLA_ROPE + 1:MLA_ROPE + 1 + n_c] = kc
    return jnp.asarray(q), jnp.asarray(k), jnp.zeros((n_meta, LANES), BF16)


def kernel(x, meta_tokens, norm_pre, w_in, diff_lambda_q1, diff_lambda_k1, diff_lambda_q2,
           diff_lambda_k2, diff_subln, mla_norm_q, mla_norm_kv, w_uq, w_ukv, w_out, norm_post):
    assert norm_pre.shape[0] == 1, "single-layer block"
    b, s, d = x.shape
    n_meta = meta_tokens.shape[0]
    mix = w_out.shape[1]
    diff_w = mix // 2
    mla_w = mix - diff_w
    ha = diff_w // DIFF_V_DIM
    hb = mla_w // MLA_V_DIM
    q_lora = mla_norm_q.shape[-1]
    kv_lora = mla_norm_kv.shape[-1]
    t = min(KEY_TILE, s)
    tq = min(QUERY_TILE, s)
    assert s % tq == 0 and tq == DIAG_BLOCKS * t and t % CHUNK == 0 and n_meta <= 16
    lambda_init = 0.8 - 0.6 * math.exp(-0.3 * 0)

    w = w_in[0]
    w_all = w.astype(BF16)
    lat = q_lora + kv_lora
    tn = _tile(math.gcd(diff_w, lat), 1024)
    n_main = 3 * diff_w + lat
    assert (3 * diff_w) % q_lora == 0 and (3 * diff_w + q_lora) % kv_lora == 0
    q_blk0 = 0
    k_blk0 = diff_w // HEAD_TILE
    g_blk0 = 2 * diff_w // HEAD_TILE
    cq_block = 3 * diff_w // q_lora
    ckv_block = (3 * diff_w + q_lora) // kv_lora
    colscale = jnp.ones((1, n_main), F32).at[:, :diff_w].set(DIFF_HEAD_DIM ** -0.5 * LOG2E)
    half = MLA_ROPE // 2
    w_dvt = _transpose_cast(w, 2 * diff_w, diff_w)
    w_mg16, w_kr2 = _gate_weights(w, 4 * diff_w + lat, mla_w)

    wq = w_uq[0].reshape(q_lora, hb, MLA_NOPE + MLA_ROPE)
    wq_r = wq[:, :, MLA_NOPE:]
    wq_full = jnp.concatenate(
        [wq[:, :, :MLA_NOPE], wq_r, jnp.concatenate([-wq_r[:, :, half:], wq_r[:, :, :half]], -1)],
        axis=-1).reshape(q_lora, hb * HEAD_TILE).astype(BF16)
    wkv = w_ukv[0].reshape(kv_lora, hb, MLA_NOPE + MLA_V_DIM)
    w_kn = wkv[:, :, :MLA_NOPE].reshape(kv_lora, hb * MLA_NOPE).astype(BF16)
    w_vt = wkv[:, :, MLA_NOPE:].reshape(kv_lora, hb * MLA_V_DIM).T.astype(BF16)
    wo = w_out[0].astype(BF16)

    pos = jnp.arange(n_meta + s, dtype=jnp.int32)
    rope_tab = _rope_table(pos)
    tab_meta, tab_seq = rope_tab[:n_meta], rope_tab[n_meta:]
    eq_tab, ek_tab, ekm_tab, core_tab = _diff_tables(ha, n_meta, s, t)
    qlane_tab, klane_tab, klane_meta = _mla_mask_lanes(n_meta, s)
    lane = jnp.arange(HEAD_TILE)
    row = jnp.arange(LANES)
    padk = jnp.where((lane[None, :] == MLA_NOPE + MLA_ROPE) & (row[:, None] >= n_meta),
                     MASK_VALUE, 0.0).astype(BF16)

    x2d = x.reshape(b * s, d)
    u = _prenorm(x2d, norm_pre)
    u_meta = _prenorm(meta_tokens.astype(x.dtype), norm_pre)
    main_tiles = (n_main // tn, 2 * diff_w // tn, diff_w // tn)
    z = _inproj(u, w_all, colscale, tn, *main_tiles)
    z_meta = _inproj(u_meta, w_all, colscale, tn, *main_tiles)
    z_gate, kr = _inproj_gate_kr(u, w_mg16, w_kr2)
    _, kr_meta = _inproj_gate_kr(u_meta, w_mg16, w_kr2)
    vt_a = _v_transposed(w_dvt, u, b, t)
    vt_a_meta = _v_transposed(w_dvt, u_meta, 1, n_meta)[0, 0]

    g_cq = mla_norm_q.astype(F32)
    g_ckv = mla_norm_kv.astype(F32)
    q_b = _mla_q(z, g_cq, wq_full, tab_seq, qlane_tab, hb, q_lora, cq_block, s)
    k_b, vt_b = _mla_kv(z, kr, g_ckv, w_kn, w_vt, tab_seq, klane_tab, hb, kv_lora, ckv_block,
                        b, s, t)
    k_b_meta, vt_b_meta = _mla_kv(z_meta, kr_meta, g_ckv, w_kn, w_vt, tab_meta, klane_meta, hb,
                                  kv_lora, ckv_block, 1, n_meta, n_meta)
    vt_b_meta = vt_b_meta[0, 0]

    z3 = z.reshape(b, s, n_main)
    lams = [v.astype(F32) for v in (diff_lambda_q1, diff_lambda_k1, diff_lambda_q2, diff_lambda_k2)]
    mix_a = _diff_attn(z3, vt_a, z_meta, vt_a_meta, eq_tab, ek_tab, ekm_tab, core_tab, lams,
                       diff_subln.astype(F32), ha, (q_blk0, k_blk0, g_blk0), lambda_init, t, tq)
    mix_b = _mla_attn(q_b.reshape(b, s, hb * HEAD_TILE), k_b.reshape(b, s, hb * HEAD_TILE), vt_b,
                      z_gate.reshape(b, s, mla_w), k_b_meta, vt_b_meta, padk, hb, 0, t, tq)

    out = _outproj(mix_a.reshape(b * s, diff_w), mix_b.reshape(b * s, mla_w), wo, x2d,
                   norm_post.astype(F32))
    return out.reshape(b, s, d)
```

```python
import functools
import math

import jax
import jax.numpy as jnp
import numpy as np
from jax import lax
from jax.experimental import pallas as pl
from jax.experimental.pallas import tpu as pltpu

CHUNK = 64
RMS_EPS = 1e-6
MASK_VALUE = -1e30
DIFF_HEAD_DIM = 128
DIFF_V_DIM = 2 * DIFF_HEAD_DIM
MLA_NOPE = 128
MLA_ROPE = 64
MLA_V_DIM = 128
ROPE_THETA = 10000.0

LANES = 128
HEAD_TILE = 2 * LANES
VMEM_LIMIT = 56 * 1024 * 1024

KEY_TILE = 512
DIAG_BLOCKS = 2
QUERY_TILE = DIAG_BLOCKS * KEY_TILE
COL_TILE = 256
MLA_HEADS_PER_STEP = 2
M_INIT = -1e38
ONES_ROWS = 16
LOG2E = math.log2(math.e)

F32 = jnp.float32
BF16 = jnp.bfloat16
NP_BF16 = np.dtype(jnp.bfloat16)


def _params(*sem):
    return pltpu.CompilerParams(dimension_semantics=sem, vmem_limit_bytes=VMEM_LIMIT)


def _tile(dim, target, mult=LANES):
    if dim <= target:
        return dim
    t = (target // mult) * mult
    while t > mult and dim % t:
        t -= mult
    assert dim % t == 0, (dim, target)
    return t


def _dot(a, b):
    return jnp.dot(a, b, preferred_element_type=F32)


def _dot_nt(a, b):
    return lax.dot_general(a, b, (((1,), (1,)), ((), ())), preferred_element_type=F32)


def _prenorm_kernel(x_ref, g_ref, o_ref):
    x = x_ref[...]
    ms = jnp.mean(x * x, axis=-1, keepdims=True)
    o_ref[...] = (x * lax.rsqrt(ms + RMS_EPS) * g_ref[...]).astype(o_ref.dtype)


def _prenorm(x2d, g):
    m, d = x2d.shape
    tm = _tile(m, 256, 8)
    return pl.pallas_call(
        _prenorm_kernel,
        out_shape=jax.ShapeDtypeStruct((m, d), BF16),
        grid=(m // tm,),
        in_specs=[pl.BlockSpec((tm, d), lambda i: (i, 0)),
                  pl.BlockSpec((1, d), lambda i: (0, 0))],
        out_specs=pl.BlockSpec((tm, d), lambda i: (i, 0)),
        compiler_params=_params("arbitrary"),
        name="prenorm",
    )(x2d, g)


def _inproj_kernel(u_ref, w_ref, s_ref, z_ref):
    z_ref[...] = (_dot_nt(u_ref[...], w_ref[...]) * s_ref[...]).astype(z_ref.dtype)


def _inproj_kr_kernel(u_ref, w_ref, wkr_ref, z_ref, kr_ref):
    u = u_ref[...]
    z_ref[...] = _dot_nt(u, w_ref[...]).astype(z_ref.dtype)

    @pl.when(pl.program_id(1) == 0)
    def _():
        kr_ref[...] = _dot_nt(u, wkr_ref[...])


def _inproj(u, w_t, colscale, tn, n_tiles, skip_at, skip_tiles):
    m, d = u.shape
    tm = _tile(m, 1024, 16)

    def wrow(i, j):
        return jnp.where(j < skip_at, j, j + skip_tiles), 0

    return pl.pallas_call(
        _inproj_kernel,
        out_shape=jax.ShapeDtypeStruct((m, n_tiles * tn), BF16),
        grid=(m // tm, n_tiles),
        in_specs=[pl.BlockSpec((tm, d), lambda i, j: (i, 0)),
                  pl.BlockSpec((tn, d), wrow),
                  pl.BlockSpec((1, tn), lambda i, j: (0, j))],
        out_specs=pl.BlockSpec((tm, tn), lambda i, j: (i, j)),
        compiler_params=_params("arbitrary", "arbitrary"),
        name="inproj",
    )(u, w_t, colscale)


def _inproj_gate_kr(u, w_t, wkr_t):
    m, d = u.shape
    n = w_t.shape[0]
    tm = _tile(m, 1024, 16)
    tn = _tile(n, 1024)
    return pl.pallas_call(
        _inproj_kr_kernel,
        out_shape=(jax.ShapeDtypeStruct((m, n), BF16),
                   jax.ShapeDtypeStruct((m, LANES), F32)),
        grid=(m // tm, n // tn),
        in_specs=[pl.BlockSpec((tm, d), lambda i, j: (i, 0)),
                  pl.BlockSpec((tn, d), lambda i, j: (j, 0)),
                  pl.BlockSpec((LANES, d), lambda i, j: (0, 0))],
        out_specs=(pl.BlockSpec((tm, tn), lambda i, j: (i, j)),
                   pl.BlockSpec((tm, LANES), lambda i, j: (i, 0))),
        compiler_params=_params("arbitrary", "arbitrary"),
        name="inproj_gate",
    )(u, w_t, wkr_t)


def _vt_kernel(w_ref, u_ref, o_ref):
    o_ref[0, 0] = _dot_nt(w_ref[...], u_ref[...]).astype(o_ref.dtype)


def _v_transposed(w_t, row_block, n, u, batch, t):
    d = w_t.shape[1]
    m = u.shape[0]
    nt = m // (batch * t)
    return pl.pallas_call(
        _vt_kernel,
        out_shape=jax.ShapeDtypeStruct((batch, nt, n, t), BF16),
        grid=(m // t,),
        in_specs=[pl.BlockSpec((n, d), lambda i: (row_block, 0)),
                  pl.BlockSpec((t, d), lambda i: (i, 0))],
        out_specs=pl.BlockSpec((1, 1, n, t), lambda i: (i // nt, i % nt, 0, 0)),
        compiler_params=_params("arbitrary"),
        name="inproj_vt",
    )(w_t, u)


def _rope_combine(t):
    r = t + pltpu.roll(t, MLA_ROPE, axis=1)
    lane = lax.broadcasted_iota(jnp.int32, r.shape, 1)
    return jnp.where(lane < MLA_ROPE, r, 0.0), lane


def _mla_q_kernel(cq_ref, g_ref, w_ref, tab_ref, lanes_ref, o_ref, cqn_ref, *, heads_per_step, scale):
    @pl.when(pl.program_id(1) == 0)
    def _():
        c = cq_ref[...].astype(F32)
        ms = jnp.mean(c * c, axis=-1, keepdims=True)
        cqn_ref[...] = (c * lax.rsqrt(ms + RMS_EPS) * g_ref[...]).astype(cqn_ref.dtype)

    y = _dot(cqn_ref[...], w_ref[...])
    tab = tab_ref[...]
    mask_lanes = lanes_ref[...].astype(F32)
    for hh in range(heads_per_step):
        base = hh * HEAD_TILE
        nope = y[:, base:base + MLA_NOPE] * scale
        r, _ = _rope_combine(y[:, base + MLA_NOPE:base + HEAD_TILE] * tab)
        o_ref[:, base:base + MLA_NOPE] = nope.astype(o_ref.dtype)
        o_ref[:, base + MLA_NOPE:base + HEAD_TILE] = (r * scale + mask_lanes).astype(o_ref.dtype)


def _mla_q(z, g, w, tab, lanes_tab, n_heads, q_lora, cq_block, seq):
    m = z.shape[0]
    tm = _tile(m, 1024, 16)
    tm = math.gcd(tm, seq)
    hp = _tile(n_heads, 4, 1)
    nseq = seq // tm
    kern = functools.partial(_mla_q_kernel, heads_per_step=hp,
                             scale=float((MLA_NOPE + MLA_ROPE) ** -0.5 * LOG2E))
    return pl.pallas_call(
        kern,
        out_shape=jax.ShapeDtypeStruct((m, n_heads * HEAD_TILE), BF16),
        grid=(m // tm, n_heads // hp),
        in_specs=[pl.BlockSpec((tm, q_lora), lambda i, j: (i, cq_block)),
                  pl.BlockSpec((1, q_lora), lambda i, j: (0, 0)),
                  pl.BlockSpec((q_lora, hp * HEAD_TILE), lambda i, j: (0, j)),
                  pl.BlockSpec((tm, LANES), lambda i, j: (i % nseq, 0)),
                  pl.BlockSpec((tm, LANES), lambda i, j: (i % nseq, 0))],
        out_specs=pl.BlockSpec((tm, hp * HEAD_TILE), lambda i, j: (i, j)),
        scratch_shapes=[pltpu.VMEM((tm, q_lora), BF16)],
        compiler_params=_params("arbitrary", "arbitrary"),
        name="mla_q",
    )(z, g, w, tab, lanes_tab)


def _mla_kv_kernel(ckv_ref, kr_ref, g_ref, wk_ref, wvt_ref, tab_ref, lanes_ref, k_ref, vt_ref,
                   *, n_heads, t):
    c = ckv_ref[...].astype(F32)
    ms = jnp.mean(c * c, axis=-1, keepdims=True)
    cn = (c * lax.rsqrt(ms + RMS_EPS) * g_ref[...]).astype(BF16)
    kn = _dot(cn, wk_ref[...])
    kr, _ = _rope_combine(kr_ref[...] * tab_ref[...])
    kr = (kr + lanes_ref[...].astype(F32)).astype(k_ref.dtype)
    for h in range(n_heads):
        k_ref[:, h * HEAD_TILE:h * HEAD_TILE + MLA_NOPE] = (
            kn[:, h * MLA_NOPE:(h + 1) * MLA_NOPE].astype(k_ref.dtype))
        k_ref[:, h * HEAD_TILE + MLA_NOPE:(h + 1) * HEAD_TILE] = kr
    vt = _dot_nt(wvt_ref[...], cn).astype(vt_ref.dtype)
    for tt in range(vt_ref.shape[1]):
        vt_ref[0, tt] = vt[:, tt * t:(tt + 1) * t]


def _mla_kv(z, kr, g, wk, wvt, tab, lanes_tab, n_heads, kv_lora, ckv_block, batch, seq, t):
    m = z.shape[0]
    tm = math.gcd(_tile(m, 1024, 16), seq)
    t = min(t, tm)
    nseq = seq // tm
    kern = functools.partial(_mla_kv_kernel, n_heads=n_heads, t=t)
    return pl.pallas_call(
        kern,
        out_shape=(jax.ShapeDtypeStruct((m, n_heads * HEAD_TILE), BF16),
                   jax.ShapeDtypeStruct((batch, seq // t, n_heads * MLA_V_DIM, t), BF16)),
        grid=(m // tm,),
        in_specs=[pl.BlockSpec((tm, kv_lora), lambda i: (i, ckv_block)),
                  pl.BlockSpec((tm, LANES), lambda i: (i, 0)),
                  pl.BlockSpec((1, kv_lora), lambda i: (0, 0)),
                  pl.BlockSpec((kv_lora, n_heads * MLA_NOPE), lambda i: (0, 0)),
                  pl.BlockSpec((n_heads * MLA_V_DIM, kv_lora), lambda i: (0, 0)),
                  pl.BlockSpec((tm, LANES), lambda i: (i % nseq, 0)),
                  pl.BlockSpec((tm, LANES), lambda i: (i % nseq, 0))],
        out_specs=(pl.BlockSpec((tm, n_heads * HEAD_TILE), lambda i: (i, 0)),
                   pl.BlockSpec((1, tm // t, n_heads * MLA_V_DIM, t),
                                lambda i: (i // nseq, i % nseq, 0, 0))),
        compiler_params=_params("arbitrary"),
        name="mla_kv",
    )(z, kr, g, wk, wvt, tab, lanes_tab)


def _col_max(s_t):
    return jnp.max(s_t, axis=0, keepdims=True)


def _score_stage(s_ref, mx_ref, mm, cs, s_t):
    s_ref[mm, :, cs] = s_t
    mx_ref[mm, :, cs] = _col_max(s_t)


def _exp_stage(s_ref, mx_ref, p_ref, mm, cs, m_prev):
    m_new = jnp.maximum(m_prev, mx_ref[mm, :, cs])
    p_ref[mm, :, cs] = jnp.exp2(s_ref[mm, :, cs] - m_new).astype(BF16)
    return m_new, jnp.exp2(m_prev - m_new)


def _acc_stage(p_ref, acc_ref, mm, cs, vt_aug, alpha):
    acc_ref[mm, :, cs] = alpha * acc_ref[mm, :, cs] + _dot(vt_aug, p_ref[mm, :, cs])


def _silu(g):
    return g * jax.nn.sigmoid(g)


def _build_values(vaug_ref, vtm_ref, vt, vt_meta, dv, n_meta):
    vaug_ref[:, 0:dv, :] = vt
    vaug_ref[:, dv:, :] = jnp.ones((vaug_ref.shape[0], ONES_ROWS, vaug_ref.shape[2]), BF16)
    vtm_ref[0:dv, :] = jnp.zeros((dv, vtm_ref.shape[1]), BF16)
    vtm_ref[0:dv, 0:n_meta] = vt_meta
    vtm_ref[dv:, :] = jnp.ones((ONES_ROWS, vtm_ref.shape[1]), BF16)


def _attention_sweep(n_maps, qk, v_blk, meta_scores, vtm, diag_fix, s_refs, mx_refs, p_refs,
                     acc_ref, n_full):
    maps = range(n_maps)
    tq = acc_ref.shape[2]
    cols = [slice(c, c + COL_TILE) for c in range(0, tq, COL_TILE)]

    def score(slot, blk, c, fix=None):
        for mm in maps:
            s_t = qk(mm, blk, cols[c])
            if fix is not None:
                s_t = diag_fix(fix, c, s_t)
            _score_stage(s_refs[slot], mx_refs[slot], mm, cols[c], s_t)

    def expo(slot, c, m, alpha):
        for mm in maps:
            m[mm][c], alpha[mm][c] = _exp_stage(s_refs[slot], mx_refs[slot], p_refs[slot], mm,
                                                cols[c], m[mm][c])

    def accum(slot, vb, c, alpha):
        for mm in maps:
            _acc_stage(p_refs[slot], acc_ref, mm, cols[c], vb[mm], alpha[mm][c])

    def blank():
        return [[None] * len(cols) for _ in maps]

    m, a0, a1 = blank(), blank(), blank()
    for c in range(len(cols)):
        s_meta = [meta_scores(mm, cols[c]) for mm in maps]
        score(0, n_full, c, fix=0)
        score(1, n_full + 1, c, fix=1)
        for mm in maps:
            mx = _col_max(s_meta[mm])
            acc_ref[mm, :, cols[c]] = _dot(vtm[mm], jnp.exp2(s_meta[mm] - mx).astype(BF16))
            m[mm][c] = mx
    for c in range(len(cols)):
        expo(0, c, m, a0)

    def body(i, carry):
        m, a0 = [list(x) for x in carry[0]], [list(x) for x in carry[1]]
        a1 = blank()
        first = i == 0
        vb0 = v_blk(jnp.where(first, n_full, 2 * i - 2))
        vb1 = v_blk(jnp.where(first, n_full + 1, 2 * i - 1))
        for c in range(len(cols)):
            accum(0, vb0, c, a0)
            expo(1, c, m, a1)
            score(0, 2 * i, c)
        for c in range(len(cols)):
            accum(1, vb1, c, a1)
            expo(0, c, m, a0)
            score(1, 2 * i + 1, c)
        return m, a0

    m, a0 = lax.fori_loop(0, n_full // 2, body, (m, a0))
    m, a0 = [list(x) for x in m], [list(x) for x in a0]

    last = jnp.where(n_full == 0, 0, n_full - 2)
    vb0, vb1 = v_blk(last), v_blk(last + 1)
    for c in range(len(cols)):
        accum(0, vb0, c, a0)
        expo(1, c, m, a1)
    for c in range(len(cols)):
        accum(1, vb1, c, a1)


def _diff_attn_kernel(q_ref, k_ref, vt_ref, g_ref, kmeta_ref, vtmeta_ref, eq_ref, ek_ref,
                      ekmeta_ref, core_ref, lq1_ref, lk1_ref, lq2_ref, lk2_ref, subln_ref,
                      o_ref, kaug_ref, kmaug_ref, vaug_ref, vtm_ref, s0_ref, s1_ref, mx0_ref, mx1_ref,
                      p0_ref, p1_ref, acc_ref, *, lambda_init, t, n_meta):
    qi = pl.program_id(2)
    d = DIFF_HEAD_DIM
    dv = DIFF_V_DIM

    @pl.when(qi == 0)
    def _build():
        kmaug_ref[...] = jnp.zeros(kmaug_ref.shape, BF16)
        _build_values(vaug_ref, vtm_ref, vt_ref[0], vtmeta_ref[...], dv, n_meta)
        for mm in range(2):
            kaug_ref[mm, :, 0:d] = k_ref[0, :, mm * d:(mm + 1) * d]
            kaug_ref[mm, :, d:2 * d] = ek_ref[0]
            kmaug_ref[mm, 0:n_meta, 0:d] = kmeta_ref[:, mm * d:(mm + 1) * d]
            kmaug_ref[mm, :, d:2 * d] = ekmeta_ref[0]

    q = q_ref[0]
    eq = eq_ref[0]
    qa = [jnp.concatenate([q[:, mm * d:(mm + 1) * d], eq], axis=1) for mm in range(2)]

    def qk(mm, blk, cs):
        off = pl.multiple_of(blk * t, t)
        return _dot_nt(kaug_ref[mm, pl.ds(off, t), :], qa[mm][cs])

    def diag_fix(kd, c, s_t):
        lo = c * COL_TILE - kd * t
        if 0 <= lo < t:
            return s_t + core_ref[0, :, lo:lo + COL_TILE]
        return s_t

    def v_blk(blk):
        vb = vaug_ref[blk]
        return [vb, vb]

    _attention_sweep(2, qk, v_blk, lambda mm, cs: _dot_nt(kmaug_ref[mm], qa[mm][cs]),
                     [vtm_ref[...]] * 2, diag_fix,
                     (s0_ref, s1_ref), (mx0_ref, mx1_ref), (p0_ref, p1_ref), acc_ref,
                     qi * DIAG_BLOCKS)

    lam = (jnp.exp(jnp.sum(lq1_ref[...] * lk1_ref[...], axis=-1, keepdims=True))
           - jnp.exp(jnp.sum(lq2_ref[...] * lk2_ref[...], axis=-1, keepdims=True))
           + lambda_init)
    o_t = (acc_ref[0, 0:dv] / acc_ref[0, dv:dv + 1]
           - lam * (acc_ref[1, 0:dv] / acc_ref[1, dv:dv + 1]))
    ms = jnp.mean(o_t * o_t, axis=0, keepdims=True)
    o = (o_t * lax.rsqrt(ms + RMS_EPS)).T
    o = (o * subln_ref[...]) * (1.0 - lambda_init)
    o_ref[0] = (o * _silu(g_ref[0].astype(F32))).astype(o_ref.dtype)


def _diff_attn(z3, vt, z_meta, vt_meta, eq_tab, ek_tab, ekmeta_tab, core_tab, lams, subln,
               n_heads, blk, lambda_init, t, tq):
    b, s, _ = z3.shape
    n_meta = z_meta.shape[0]
    nq = s // tq
    q0, k0, g0 = blk
    vec = pl.BlockSpec((1, DIFF_HEAD_DIM), lambda bb, h, qi: (0, 0))
    kern = functools.partial(_diff_attn_kernel, lambda_init=lambda_init, t=t, n_meta=n_meta)
    return pl.pallas_call(
        kern,
        out_shape=jax.ShapeDtypeStruct((b, s, n_heads * DIFF_V_DIM), BF16),
        grid=(b, n_heads, nq),
        in_specs=[
            pl.BlockSpec((1, tq, HEAD_TILE), lambda bb, h, qi: (bb, qi, q0 + h)),
            pl.BlockSpec((1, s, HEAD_TILE), lambda bb, h, qi: (bb, 0, k0 + h)),
            pl.BlockSpec((1, s // t, DIFF_V_DIM, t), lambda bb, h, qi: (bb, 0, h, 0)),
            pl.BlockSpec((1, tq, DIFF_V_DIM), lambda bb, h, qi: (bb, qi, g0 + h)),
            pl.BlockSpec((n_meta, HEAD_TILE), lambda bb, h, qi: (0, k0 + h)),
            pl.BlockSpec((DIFF_V_DIM, n_meta), lambda bb, h, qi: (h, 0)),
            pl.BlockSpec((1, tq, LANES), lambda bb, h, qi: (h, qi, 0)),
            pl.BlockSpec((1, s, LANES), lambda bb, h, qi: (h, 0, 0)),
            pl.BlockSpec((1, LANES, LANES), lambda bb, h, qi: (h, 0, 0)),
            pl.BlockSpec((1, t, t), lambda bb, h, qi: (h, 0, 0)),
            vec, vec, vec, vec,
            pl.BlockSpec((1, DIFF_V_DIM), lambda bb, h, qi: (0, 0)),
        ],
        out_specs=pl.BlockSpec((1, tq, DIFF_V_DIM), lambda bb, h, qi: (bb, qi, h)),
        scratch_shapes=[
            pltpu.VMEM((2, s, HEAD_TILE), BF16),
            pltpu.VMEM((2, LANES, HEAD_TILE), BF16),
            pltpu.VMEM((s // t, DIFF_V_DIM + ONES_ROWS, t), BF16),
            pltpu.VMEM((DIFF_V_DIM + ONES_ROWS, LANES), BF16),
            pltpu.VMEM((2, t, tq), F32),
            pltpu.VMEM((2, t, tq), F32),
            pltpu.VMEM((2, 1, tq), F32),
            pltpu.VMEM((2, 1, tq), F32),
            pltpu.VMEM((2, t, tq), BF16),
            pltpu.VMEM((2, t, tq), BF16),
            pltpu.VMEM((2, DIFF_V_DIM + ONES_ROWS, tq), F32),
        ],
        compiler_params=_params("arbitrary", "arbitrary", "arbitrary"),
        name="diff_attn",
    )(z3, z3, vt, z3, z_meta, vt_meta, eq_tab, ek_tab, ekmeta_tab, core_tab, *lams, subln)


def _mla_attn_kernel(q_ref, k_ref, vt_ref, g_ref, kmeta_ref, vtmeta_ref, padk_ref,
                     o_ref, kmaug_ref, vaug_ref, vtm_ref, s0_ref, s1_ref, mx0_ref, mx1_ref,
                     p0_ref, p1_ref, acc_ref, *, t, n_meta, hp):
    qi = pl.program_id(2)
    dv = MLA_V_DIM
    heads = range(hp)

    @pl.when(qi == 0)
    def _build():
        for hh in heads:
            kmaug_ref[hh] = padk_ref[...]
            kmaug_ref[hh, 0:n_meta, :] = kmeta_ref[:, hh * HEAD_TILE:(hh + 1) * HEAD_TILE]
            _build_values(vaug_ref.at[hh], vtm_ref.at[hh], vt_ref[0, :, hh * dv:(hh + 1) * dv, :],
                          vtmeta_ref[hh * dv:(hh + 1) * dv, :], dv, n_meta)

    q = [q_ref[0, :, hh * HEAD_TILE:(hh + 1) * HEAD_TILE] for hh in heads]

    def qk(hh, blk, cs):
        off = pl.multiple_of(blk * t, t)
        return _dot_nt(k_ref[0, pl.ds(off, t), hh * HEAD_TILE:(hh + 1) * HEAD_TILE], q[hh][cs])

    _attention_sweep(hp, qk, lambda blk: [vaug_ref[hh, blk] for hh in heads],
                     lambda hh, cs: _dot_nt(kmaug_ref[hh], q[hh][cs]),
                     [vtm_ref[hh] for hh in heads], lambda kd, c, s_t: s_t,
                     (s0_ref, s1_ref), (mx0_ref, mx1_ref), (p0_ref, p1_ref), acc_ref,
                     qi * DIAG_BLOCKS)

    for hh in heads:
        o = (acc_ref[hh, 0:dv] / acc_ref[hh, dv:dv + 1]).T
        g = g_ref[0, :, hh * dv:(hh + 1) * dv].astype(F32)
        o_ref[0, :, hh * dv:(hh + 1) * dv] = (o * _silu(g)).astype(o_ref.dtype)


def _mla_attn(q3, k3, vt, z3, k_meta, vt_meta, padk, n_heads, g0, t, tq):
    b, s, _ = q3.shape
    n_meta = k_meta.shape[0]
    nq = s // tq
    hp = _tile(n_heads, MLA_HEADS_PER_STEP, 1)
    kern = functools.partial(_mla_attn_kernel, t=t, n_meta=n_meta, hp=hp)
    return pl.pallas_call(
        kern,
        out_shape=jax.ShapeDtypeStruct((b, s, n_heads * MLA_V_DIM), BF16),
        grid=(b, n_heads // hp, nq),
        in_specs=[
            pl.BlockSpec((1, tq, hp * HEAD_TILE), lambda bb, h, qi: (bb, qi, h)),
            pl.BlockSpec((1, s, hp * HEAD_TILE), lambda bb, h, qi: (bb, 0, h)),
            pl.BlockSpec((1, s // t, hp * MLA_V_DIM, t), lambda bb, h, qi: (bb, 0, h, 0)),
            pl.BlockSpec((1, tq, hp * MLA_V_DIM), lambda bb, h, qi: (bb, qi, g0 + h)),
            pl.BlockSpec((n_meta, hp * HEAD_TILE), lambda bb, h, qi: (0, h)),
            pl.BlockSpec((hp * MLA_V_DIM, n_meta), lambda bb, h, qi: (h, 0)),
            pl.BlockSpec((LANES, HEAD_TILE), lambda bb, h, qi: (0, 0)),
        ],
        out_specs=pl.BlockSpec((1, tq, hp * MLA_V_DIM), lambda bb, h, qi: (bb, qi, h)),
        scratch_shapes=[
            pltpu.VMEM((hp, LANES, HEAD_TILE), BF16),
            pltpu.VMEM((hp, s // t, MLA_V_DIM + ONES_ROWS, t), BF16),
            pltpu.VMEM((hp, MLA_V_DIM + ONES_ROWS, LANES), BF16),
            pltpu.VMEM((hp, t, tq), F32),
            pltpu.VMEM((hp, t, tq), F32),
            pltpu.VMEM((hp, 1, tq), F32),
            pltpu.VMEM((hp, 1, tq), F32),
            pltpu.VMEM((hp, t, tq), BF16),
            pltpu.VMEM((hp, t, tq), BF16),
            pltpu.VMEM((hp, MLA_V_DIM + ONES_ROWS, tq), F32),
        ],
        compiler_params=_params("arbitrary", "arbitrary", "arbitrary"),
        name="mla_attn",
    )(q3, k3, vt, z3, k_meta, vt_meta, padk)


def _outproj_kernel(a_ref, b_ref, w_ref, x_ref, g_ref, o_ref, *, a_steps, n_steps):
    k = pl.program_id(1)

    @pl.when(k == 0)
    def _():
        o_ref[...] = _dot(a_ref[...], w_ref[...])

    @pl.when((k > 0) & (k < a_steps))
    def _():
        o_ref[...] += _dot(a_ref[...], w_ref[...])

    @pl.when(k >= a_steps)
    def _():
        o_ref[...] += _dot(b_ref[...], w_ref[...])

    @pl.when(k == n_steps - 1)
    def _():
        y = o_ref[...]
        ms = jnp.mean(y * y, axis=-1, keepdims=True)
        o_ref[...] = x_ref[...] + y * lax.rsqrt(ms + RMS_EPS) * g_ref[...]


def _outproj(mix_a, mix_b, w, x2d, g):
    m, ka = mix_a.shape
    kb = mix_b.shape[1]
    n = w.shape[1]
    tm = _tile(m, 512, 16)
    tk = _tile(math.gcd(ka, kb), 512)
    a_steps, b_steps = ka // tk, kb // tk
    kern = functools.partial(_outproj_kernel, a_steps=a_steps, n_steps=a_steps + b_steps)
    return pl.pallas_call(
        kern,
        out_shape=jax.ShapeDtypeStruct((m, n), F32),
        grid=(m // tm, a_steps + b_steps),
        in_specs=[pl.BlockSpec((tm, tk), lambda i, k: (i, jnp.minimum(k, a_steps - 1))),
                  pl.BlockSpec((tm, tk), lambda i, k: (i, jnp.maximum(k - a_steps, 0))),
                  pl.BlockSpec((tk, n), lambda i, k: (k, 0)),
                  pl.BlockSpec((tm, n), lambda i, k: (i, 0)),
                  pl.BlockSpec((1, n), lambda i, k: (0, 0))],
        out_specs=pl.BlockSpec((tm, n), lambda i, k: (i, 0)),
        compiler_params=_params("arbitrary", "arbitrary"),
        name="outproj",
    )(mix_a, mix_b, w, x2d, g)


def _split3(v):
    v = np.asarray(v, np.float32)
    hi = v.astype(NP_BF16)
    r = v - hi.astype(np.float32)
    mid = r.astype(NP_BF16)
    lo = (r - mid.astype(np.float32)).astype(NP_BF16)
    return hi, mid, lo


def _rope_table(pos):
    inv_freq = 1.0 / (ROPE_THETA ** (jnp.arange(0, MLA_ROPE, 2, dtype=F32) / MLA_ROPE))
    ang = pos.astype(F32)[:, None] * inv_freq[None, :]
    cos, sin = jnp.cos(ang), jnp.sin(ang)
    return jnp.concatenate([cos, cos, sin, sin], axis=1)


def _chunk_mask_lanes(seq):
    chunk = np.arange(seq) // CHUNK
    c = np.arange(seq // CHUNK - 1)
    q_lanes = (chunk[:, None] == c[None, :]).astype(np.float32)
    k_lanes = np.where(chunk[:, None] > c[None, :], MASK_VALUE, 0.0).astype(np.float32)
    return q_lanes.astype(NP_BF16), k_lanes.astype(NP_BF16)


def _diff_tables(n_heads, n_meta, seq, t):
    f32 = np.float32
    slopes = (2.0 ** (-8.0 * np.arange(1, n_heads + 1, dtype=f32) / n_heads)).astype(f32)
    slopes = (slopes * f32(LOG2E)).astype(f32)
    pos_q = ((np.arange(seq, dtype=f32) + n_meta)[None, :] * slopes[:, None]).astype(f32)
    pos_m = (np.arange(LANES, dtype=f32)[None, :] * slopes[:, None]).astype(f32)
    qc, kc = _chunk_mask_lanes(seq)
    n_c = qc.shape[1]
    assert 8 + n_c <= LANES
    ones_q = np.ones((n_heads, seq), NP_BF16)
    zeros_q = np.zeros((n_heads, seq), NP_BF16)
    qh, qm, ql = _split3(-pos_q)
    kh, km, kl = _split3(pos_q)
    tail = np.zeros((n_heads, seq, LANES - 8 - n_c), NP_BF16)

    def lanes(cols, chunk_lanes):
        return np.concatenate([np.stack(cols + [zeros_q], -1),
                               np.broadcast_to(chunk_lanes, (n_heads,) + chunk_lanes.shape), tail], -1)

    eq = lanes([qh, qm, ql, ones_q, ones_q, ones_q, ones_q], qc)
    ek = lanes([ones_q, ones_q, ones_q, zeros_q, kh, km, kl], kc)
    mh, mmid, ml = _split3(pos_m)
    valid = (np.arange(LANES) < n_meta)[None, :]
    padmask = np.broadcast_to(np.where(valid, 0.0, MASK_VALUE).astype(f32),
                              (n_heads, LANES)).astype(NP_BF16)
    ones_m = np.ones((n_heads, LANES), NP_BF16)
    ekm = np.concatenate(
        [np.stack([ones_m, ones_m, ones_m, padmask, mh, mmid, ml], -1),
         np.zeros((n_heads, LANES, LANES - 7), NP_BF16)], -1)
    j = np.arange(t)[:, None]
    i = np.arange(t)[None, :]
    same = (j // CHUNK) == (i // CHUNK)
    fut = np.where(same, 2.0 * np.maximum(j - i, 0), 0.0).astype(f32)
    core = (-slopes[:, None, None] * fut[None]).astype(f32)
    return tuple(jnp.asarray(a) for a in (eq, ek, ekm, core))


def _mla_mask_lanes(n_meta, seq):
    qc, kc = _chunk_mask_lanes(seq)
    n_c = qc.shape[1]
    assert MLA_ROPE + 1 + n_c <= LANES
    q = np.zeros((seq, LANES), NP_BF16)
    k = np.zeros((seq, LANES), NP_BF16)
    q[:, MLA_ROPE] = 1.0
    q[:, MLA_ROPE + 1:MLA_ROPE + 1 + n_c] = qc
    k[:, MLA_ROPE + 1:MLA_ROPE + 1 + n_c] = kc
    return jnp.asarray(q), jnp.asarray(k), jnp.zeros((n_meta, LANES), BF16)


def kernel(x, meta_tokens, norm_pre, w_in, diff_lambda_q1, diff_lambda_k1, diff_lambda_q2,
           diff_lambda_k2, diff_subln, mla_norm_q, mla_norm_kv, w_uq, w_ukv, w_out, norm_post):
    assert norm_pre.shape[0] == 1, "single-layer block"
    b, s, d = x.shape
    n_meta = meta_tokens.shape[0]
    mix = w_out.shape[1]
    diff_w = mix // 2
    mla_w = mix - diff_w
    ha = diff_w // DIFF_V_DIM
    hb = mla_w // MLA_V_DIM
    q_lora = mla_norm_q.shape[-1]
    kv_lora = mla_norm_kv.shape[-1]
    t = min(KEY_TILE, s)
    tq = min(QUERY_TILE, s)
    assert s % tq == 0 and tq == DIAG_BLOCKS * t and t % CHUNK == 0 and n_meta <= 16
    lambda_init = 0.8 - 0.6 * math.exp(-0.3 * 0)

    w_all = jnp.transpose(w_in[0]).astype(BF16)
    lat = q_lora + kv_lora
    tn = _tile(math.gcd(diff_w, lat), 1024)
    n_main = 3 * diff_w + lat
    assert (3 * diff_w) % q_lora == 0 and (3 * diff_w + q_lora) % kv_lora == 0
    q_blk0 = 0
    k_blk0 = diff_w // HEAD_TILE
    g_blk0 = 2 * diff_w // HEAD_TILE
    cq_block = 3 * diff_w // q_lora
    ckv_block = (3 * diff_w + q_lora) // kv_lora
    colscale = jnp.ones((1, n_main), F32).at[:, :diff_w].set(DIFF_HEAD_DIM ** -0.5 * LOG2E)
    half = MLA_ROPE // 2
    kr0 = 4 * diff_w + lat
    w_krt = w_all[kr0:kr0 + MLA_ROPE]
    w_kr2 = jnp.concatenate([w_krt, -w_krt[half:], w_krt[:half]], axis=0)
    w_mg16 = w_all[kr0 + MLA_ROPE:kr0 + MLA_ROPE + mla_w]

    wq = w_uq[0].reshape(q_lora, hb, MLA_NOPE + MLA_ROPE)
    wq_r = wq[:, :, MLA_NOPE:]
    wq_full = jnp.concatenate(
        [wq[:, :, :MLA_NOPE], wq_r, jnp.concatenate([-wq_r[:, :, half:], wq_r[:, :, :half]], -1)],
        axis=-1).reshape(q_lora, hb * HEAD_TILE).astype(BF16)
    wkv = w_ukv[0].reshape(kv_lora, hb, MLA_NOPE + MLA_V_DIM)
    w_kn = wkv[:, :, :MLA_NOPE].reshape(kv_lora, hb * MLA_NOPE).astype(BF16)
    w_vt = wkv[:, :, MLA_NOPE:].reshape(kv_lora, hb * MLA_V_DIM).T.astype(BF16)
    wo = w_out[0].astype(BF16)

    pos = jnp.arange(n_meta + s, dtype=jnp.int32)
    rope_tab = _rope_table(pos)
    tab_meta, tab_seq = rope_tab[:n_meta], rope_tab[n_meta:]
    eq_tab, ek_tab, ekm_tab, core_tab = _diff_tables(ha, n_meta, s, t)
    qlane_tab, klane_tab, klane_meta = _mla_mask_lanes(n_meta, s)
    lane = jnp.arange(HEAD_TILE)
    row = jnp.arange(LANES)
    padk = jnp.where((lane[None, :] == MLA_NOPE + MLA_ROPE) & (row[:, None] >= n_meta),
                     MASK_VALUE, 0.0).astype(BF16)

    x2d = x.reshape(b * s, d)
    u = _prenorm(x2d, norm_pre)
    u_meta = _prenorm(meta_tokens.astype(x.dtype), norm_pre)
    main_tiles = (n_main // tn, 2 * diff_w // tn, diff_w // tn)
    z = _inproj(u, w_all, colscale, tn, *main_tiles)
    z_meta = _inproj(u_meta, w_all, colscale, tn, *main_tiles)
    z_gate, kr = _inproj_gate_kr(u, w_mg16, w_kr2)
    _, kr_meta = _inproj_gate_kr(u_meta, w_mg16, w_kr2)
    vt_a = _v_transposed(w_all, 2, diff_w, u, b, t)
    vt_a_meta = _v_transposed(w_all, 2, diff_w, u_meta, 1, n_meta)[0, 0]

    g_cq = mla_norm_q.astype(F32)
    g_ckv = mla_norm_kv.astype(F32)
    q_b = _mla_q(z, g_cq, wq_full, tab_seq, qlane_tab, hb, q_lora, cq_block, s)
    k_b, vt_b = _mla_kv(z, kr, g_ckv, w_kn, w_vt, tab_seq, klane_tab, hb, kv_lora, ckv_block,
                        b, s, t)
    k_b_meta, vt_b_meta = _mla_kv(z_meta, kr_meta, g_ckv, w_kn, w_vt, tab_meta, klane_meta, hb,
                                  kv_lora, ckv_block, 1, n_meta, n_meta)
    vt_b_meta = vt_b_meta[0, 0]

    z3 = z.reshape(b, s, n_main)
    lams = [v.astype(F32) for v in (diff_lambda_q1, diff_lambda_k1, diff_lambda_q2, diff_lambda_k2)]
    mix_a = _diff_attn(z3, vt_a, z_meta, vt_a_meta, eq_tab, ek_tab, ekm_tab, core_tab, lams,
                       diff_subln.astype(F32), ha, (q_blk0, k_blk0, g_blk0), lambda_init, t, tq)
    mix_b = _mla_attn(q_b.reshape(b, s, hb * HEAD_TILE), k_b.reshape(b, s, hb * HEAD_TILE), vt_b,
                      z_gate.reshape(b, s, mla_w), k_b_meta, vt_b_meta, padk, hb, 0, t, tq)

    out = _outproj(mix_a.reshape(b * s, diff_w), mix_b.reshape(b * s, mla_w), wo, x2d,
                   norm_post.astype(F32))
    return out.reshape(b, s, d)
```

```python
import functools
import math

import jax
import jax.numpy as jnp
import numpy as np
from jax import lax
from jax.experimental import pallas as pl
from jax.experimental.pallas import tpu as pltpu

CHUNK = 64
RMS_EPS = 1e-6
MASK_VALUE = -1e30
DIFF_HEAD_DIM = 128
DIFF_V_DIM = 2 * DIFF_HEAD_DIM
MLA_NOPE = 128
MLA_ROPE = 64
MLA_V_DIM = 128
ROPE_THETA = 10000.0

LANES = 128
HEAD_TILE = 2 * LANES
VMEM_LIMIT = 56 * 1024 * 1024

KEY_TILE = 512
DIAG_BLOCKS = 2
QUERY_TILE = DIAG_BLOCKS * KEY_TILE
COL_TILE = 256
MLA_HEADS_PER_STEP = 2
M_INIT = -1e38
ONES_ROWS = 16
LOG2E = math.log2(math.e)

F32 = jnp.float32
BF16 = jnp.bfloat16
NP_BF16 = np.dtype(jnp.bfloat16)


def _params(*sem):
    return pltpu.CompilerParams(dimension_semantics=sem, vmem_limit_bytes=VMEM_LIMIT)


def _tile(dim, target, mult=LANES):
    if dim <= target:
        return dim
    t = (target // mult) * mult
    while t > mult and dim % t:
        t -= mult
    assert dim % t == 0, (dim, target)
    return t


def _dot(a, b):
    return jnp.dot(a, b, preferred_element_type=F32)


def _dot_nt(a, b):
    return lax.dot_general(a, b, (((1,), (1,)), ((), ())), preferred_element_type=F32)


def _prenorm_kernel(x_ref, g_ref, o_ref):
    x = x_ref[...]
    ms = jnp.mean(x * x, axis=-1, keepdims=True)
    o_ref[...] = (x * lax.rsqrt(ms + RMS_EPS) * g_ref[...]).astype(o_ref.dtype)


def _prenorm(x2d, g):
    m, d = x2d.shape
    tm = _tile(m, 256, 8)
    return pl.pallas_call(
        _prenorm_kernel,
        out_shape=jax.ShapeDtypeStruct((m, d), BF16),
        grid=(m // tm,),
        in_specs=[pl.BlockSpec((tm, d), lambda i: (i, 0)),
                  pl.BlockSpec((1, d), lambda i: (0, 0))],
        out_specs=pl.BlockSpec((tm, d), lambda i: (i, 0)),
        compiler_params=_params("arbitrary"),
        name="prenorm",
    )(x2d, g)


def _inproj_kernel(u_ref, w_ref, s_ref, z_ref):
    z_ref[...] = (_dot_nt(u_ref[...], w_ref[...]) * s_ref[...]).astype(z_ref.dtype)


def _inproj_kr_kernel(u_ref, w_ref, wkr_ref, z_ref, kr_ref):
    u = u_ref[...]
    z_ref[...] = _dot_nt(u, w_ref[...]).astype(z_ref.dtype)

    @pl.when(pl.program_id(1) == 0)
    def _():
        kr_ref[...] = _dot_nt(u, wkr_ref[...])


def _inproj(u, w_t, colscale, tn, n_tiles, skip_at, skip_tiles):
    m, d = u.shape
    tm = _tile(m, 1024, 16)

    def wrow(i, j):
        return jnp.where(j < skip_at, j, j + skip_tiles), 0

    return pl.pallas_call(
        _inproj_kernel,
        out_shape=jax.ShapeDtypeStruct((m, n_tiles * tn), BF16),
        grid=(m // tm, n_tiles),
        in_specs=[pl.BlockSpec((tm, d), lambda i, j: (i, 0)),
                  pl.BlockSpec((tn, d), wrow),
                  pl.BlockSpec((1, tn), lambda i, j: (0, j))],
        out_specs=pl.BlockSpec((tm, tn), lambda i, j: (i, j)),
        compiler_params=_params("arbitrary", "arbitrary"),
        name="inproj",
    )(u, w_t, colscale)


def _inproj_gate_kr(u, w_t, wkr_t):
    m, d = u.shape
    n = w_t.shape[0]
    tm = _tile(m, 1024, 16)
    tn = _tile(n, 1024)
    return pl.pallas_call(
        _inproj_kr_kernel,
        out_shape=(jax.ShapeDtypeStruct((m, n), BF16),
                   jax.ShapeDtypeStruct((m, LANES), F32)),
        grid=(m // tm, n // tn),
        in_specs=[pl.BlockSpec((tm, d), lambda i, j: (i, 0)),
                  pl.BlockSpec((tn, d), lambda i, j: (j, 0)),
                  pl.BlockSpec((LANES, d), lambda i, j: (0, 0))],
        out_specs=(pl.BlockSpec((tm, tn), lambda i, j: (i, j)),
                   pl.BlockSpec((tm, LANES), lambda i, j: (i, 0))),
        compiler_params=_params("arbitrary", "arbitrary"),
        name="inproj_gate",
    )(u, w_t, wkr_t)


def _vt_kernel(w_ref, u_ref, o_ref):
    o_ref[0, 0] = _dot_nt(w_ref[...], u_ref[...]).astype(o_ref.dtype)


def _v_transposed(w_t, row_block, n, u, batch, t):
    d = w_t.shape[1]
    m = u.shape[0]
    nt = m // (batch * t)
    return pl.pallas_call(
        _vt_kernel,
        out_shape=jax.ShapeDtypeStruct((batch, nt, n, t), BF16),
        grid=(m // t,),
        in_specs=[pl.BlockSpec((n, d), lambda i: (row_block, 0)),
                  pl.BlockSpec((t, d), lambda i: (i, 0))],
        out_specs=pl.BlockSpec((1, 1, n, t), lambda i: (i // nt, i % nt, 0, 0)),
        compiler_params=_params("arbitrary"),
        name="inproj_vt",
    )(w_t, u)


def _rope_combine(t):
    r = t + pltpu.roll(t, MLA_ROPE, axis=1)
    lane = lax.broadcasted_iota(jnp.int32, r.shape, 1)
    return jnp.where(lane < MLA_ROPE, r, 0.0), lane


def _mla_q_kernel(cq_ref, g_ref, w_ref, tab_ref, lanes_ref, o_ref, cqn_ref, *, heads_per_step, scale):
    @pl.when(pl.program_id(1) == 0)
    def _():
        c = cq_ref[...].astype(F32)
        ms = jnp.mean(c * c, axis=-1, keepdims=True)
        cqn_ref[...] = (c * lax.rsqrt(ms + RMS_EPS) * g_ref[...]).astype(cqn_ref.dtype)

    y = _dot(cqn_ref[...], w_ref[...])
    tab = tab_ref[...]
    mask_lanes = lanes_ref[...].astype(F32)
    for hh in range(heads_per_step):
        base = hh * HEAD_TILE
        nope = y[:, base:base + MLA_NOPE] * scale
        r, _ = _rope_combine(y[:, base + MLA_NOPE:base + HEAD_TILE] * tab)
        o_ref[:, base:base + MLA_NOPE] = nope.astype(o_ref.dtype)
        o_ref[:, base + MLA_NOPE:base + HEAD_TILE] = (r * scale + mask_lanes).astype(o_ref.dtype)


def _mla_q(z, g, w, tab, lanes_tab, n_heads, q_lora, cq_block, seq):
    m = z.shape[0]
    tm = _tile(m, 1024, 16)
    tm = math.gcd(tm, seq)
    hp = _tile(n_heads, 4, 1)
    nseq = seq // tm
    kern = functools.partial(_mla_q_kernel, heads_per_step=hp,
                             scale=float((MLA_NOPE + MLA_ROPE) ** -0.5 * LOG2E))
    return pl.pallas_call(
        kern,
        out_shape=jax.ShapeDtypeStruct((m, n_heads * HEAD_TILE), BF16),
        grid=(m // tm, n_heads // hp),
        in_specs=[pl.BlockSpec((tm, q_lora), lambda i, j: (i, cq_block)),
                  pl.BlockSpec((1, q_lora), lambda i, j: (0, 0)),
                  pl.BlockSpec((q_lora, hp * HEAD_TILE), lambda i, j: (0, j)),
                  pl.BlockSpec((tm, LANES), lambda i, j: (i % nseq, 0)),
                  pl.BlockSpec((tm, LANES), lambda i, j: (i % nseq, 0))],
        out_specs=pl.BlockSpec((tm, hp * HEAD_TILE), lambda i, j: (i, j)),
        scratch_shapes=[pltpu.VMEM((tm, q_lora), BF16)],
        compiler_params=_params("arbitrary", "arbitrary"),
        name="mla_q",
    )(z, g, w, tab, lanes_tab)


def _mla_kv_kernel(ckv_ref, kr_ref, g_ref, wk_ref, wvt_ref, tab_ref, lanes_ref, k_ref, vt_ref,
                   *, n_heads, t):
    c = ckv_ref[...].astype(F32)
    ms = jnp.mean(c * c, axis=-1, keepdims=True)
    cn = (c * lax.rsqrt(ms + RMS_EPS) * g_ref[...]).astype(BF16)
    kn = _dot(cn, wk_ref[...])
    kr, _ = _rope_combine(kr_ref[...] * tab_ref[...])
    kr = (kr + lanes_ref[...].astype(F32)).astype(k_ref.dtype)
    for h in range(n_heads):
        k_ref[:, h * HEAD_TILE:h * HEAD_TILE + MLA_NOPE] = (
            kn[:, h * MLA_NOPE:(h + 1) * MLA_NOPE].astype(k_ref.dtype))
        k_ref[:, h * HEAD_TILE + MLA_NOPE:(h + 1) * HEAD_TILE] = kr
    vt = _dot_nt(wvt_ref[...], cn).astype(vt_ref.dtype)
    for tt in range(vt_ref.shape[1]):
        vt_ref[0, tt] = vt[:, tt * t:(tt + 1) * t]


def _mla_kv(z, kr, g, wk, wvt, tab, lanes_tab, n_heads, kv_lora, ckv_block, batch, seq, t):
    m = z.shape[0]
    tm = math.gcd(_tile(m, 1024, 16), seq)
    t = min(t, tm)
    nseq = seq // tm
    kern = functools.partial(_mla_kv_kernel, n_heads=n_heads, t=t)
    return pl.pallas_call(
        kern,
        out_shape=(jax.ShapeDtypeStruct((m, n_heads * HEAD_TILE), BF16),
                   jax.ShapeDtypeStruct((batch, seq // t, n_heads * MLA_V_DIM, t), BF16)),
        grid=(m // tm,),
        in_specs=[pl.BlockSpec((tm, kv_lora), lambda i: (i, ckv_block)),
                  pl.BlockSpec((tm, LANES), lambda i: (i, 0)),
                  pl.BlockSpec((1, kv_lora), lambda i: (0, 0)),
                  pl.BlockSpec((kv_lora, n_heads * MLA_NOPE), lambda i: (0, 0)),
                  pl.BlockSpec((n_heads * MLA_V_DIM, kv_lora), lambda i: (0, 0)),
                  pl.BlockSpec((tm, LANES), lambda i: (i % nseq, 0)),
                  pl.BlockSpec((tm, LANES), lambda i: (i % nseq, 0))],
        out_specs=(pl.BlockSpec((tm, n_heads * HEAD_TILE), lambda i: (i, 0)),
                   pl.BlockSpec((1, tm // t, n_heads * MLA_V_DIM, t),
                                lambda i: (i // nseq, i % nseq, 0, 0))),
        compiler_params=_params("arbitrary"),
        name="mla_kv",
    )(z, kr, g, wk, wvt, tab, lanes_tab)


def _col_max(s_t):
    return jnp.max(s_t, axis=0, keepdims=True)


def _score_stage(s_ref, mx_ref, mm, cs, s_t):
    s_ref[mm, :, cs] = s_t
    mx_ref[mm, :, cs] = _col_max(s_t)


def _exp_stage(s_ref, mx_ref, p_ref, mm, cs, m_prev):
    m_new = jnp.maximum(m_prev, mx_ref[mm, :, cs])
    p_ref[mm, :, cs] = jnp.exp2(s_ref[mm, :, cs] - m_new).astype(BF16)
    return m_new, jnp.exp2(m_prev - m_new)


def _acc_stage(p_ref, acc_ref, mm, cs, vt_aug, alpha):
    acc_ref[mm, :, cs] = alpha * acc_ref[mm, :, cs] + _dot(vt_aug, p_ref[mm, :, cs])


def _silu(g):
    return g * jax.nn.sigmoid(g)


def _build_values(vaug_ref, vtm_ref, vt, vt_meta, dv, n_meta):
    vaug_ref[:, 0:dv, :] = vt
    vaug_ref[:, dv:, :] = jnp.ones((vaug_ref.shape[0], ONES_ROWS, vaug_ref.shape[2]), BF16)
    vtm_ref[0:dv, :] = jnp.zeros((dv, vtm_ref.shape[1]), BF16)
    vtm_ref[0:dv, 0:n_meta] = vt_meta
    vtm_ref[dv:, :] = jnp.ones((ONES_ROWS, vtm_ref.shape[1]), BF16)


def _attention_sweep(n_maps, qk, v_blk, meta_scores, vtm, diag_fix, s_refs, mx_refs, p_refs,
                     acc_ref, n_full):
    maps = range(n_maps)
    tq = acc_ref.shape[2]
    cols = [slice(c, c + COL_TILE) for c in range(0, tq, COL_TILE)]

    def score(slot, blk, c, fix=None):
        for mm in maps:
            s_t = qk(mm, blk, cols[c])
            if fix is not None:
                s_t = diag_fix(fix, c, s_t)
            _score_stage(s_refs[slot], mx_refs[slot], mm, cols[c], s_t)

    def expo(slot, c, m, alpha):
        for mm in maps:
            m[mm][c], alpha[mm][c] = _exp_stage(s_refs[slot], mx_refs[slot], p_refs[slot], mm,
                                                cols[c], m[mm][c])

    def accum(slot, vb, c, alpha):
        for mm in maps:
            _acc_stage(p_refs[slot], acc_ref, mm, cols[c], vb[mm], alpha[mm][c])

    def blank():
        return [[None] * len(cols) for _ in maps]

    m, a0, a1 = blank(), blank(), blank()
    for c in range(len(cols)):
        s_meta = [meta_scores(mm, cols[c]) for mm in maps]
        score(0, n_full, c, fix=0)
        score(1, n_full + 1, c, fix=1)
        for mm in maps:
            mx = _col_max(s_meta[mm])
            acc_ref[mm, :, cols[c]] = _dot(vtm[mm], jnp.exp2(s_meta[mm] - mx).astype(BF16))
            m[mm][c] = mx
    for c in range(len(cols)):
        expo(0, c, m, a0)

    def body(i, carry):
        m, a0 = [list(x) for x in carry[0]], [list(x) for x in carry[1]]
        a1 = blank()
        first = i == 0
        vb0 = v_blk(jnp.where(first, n_full, 2 * i - 2))
        vb1 = v_blk(jnp.where(first, n_full + 1, 2 * i - 1))
        for c in range(len(cols)):
            accum(0, vb0, c, a0)
            expo(1, c, m, a1)
            score(0, 2 * i, c)
        for c in range(len(cols)):
            accum(1, vb1, c, a1)
            expo(0, c, m, a0)
            score(1, 2 * i + 1, c)
        return m, a0

    m, a0 = lax.fori_loop(0, n_full // 2, body, (m, a0))
    m, a0 = [list(x) for x in m], [list(x) for x in a0]

    last = jnp.where(n_full == 0, 0, n_full - 2)
    vb0, vb1 = v_blk(last), v_blk(last + 1)
    for c in range(len(cols)):
        accum(0, vb0, c, a0)
        expo(1, c, m, a1)
    for c in range(len(cols)):
        accum(1, vb1, c, a1)


def _diff_attn_kernel(q_ref, k_ref, vt_ref, g_ref, kmeta_ref, vtmeta_ref, eq_ref, ek_ref,
                      ekmeta_ref, core_ref, lq1_ref, lk1_ref, lq2_ref, lk2_ref, subln_ref,
                      o_ref, kaug_ref, kmaug_ref, vaug_ref, vtm_ref, s0_ref, s1_ref, mx0_ref, mx1_ref,
                      p0_ref, p1_ref, acc_ref, *, lambda_init, t, n_meta):
    qi = pl.program_id(2)
    d = DIFF_HEAD_DIM
    dv = DIFF_V_DIM

    @pl.when(qi == 0)
    def _build():
        kmaug_ref[...] = jnp.zeros(kmaug_ref.shape, BF16)
        _build_values(vaug_ref, vtm_ref, vt_ref[0], vtmeta_ref[...], dv, n_meta)
        for mm in range(2):
            kaug_ref[mm, :, 0:d] = k_ref[0, :, mm * d:(mm + 1) * d]
            kaug_ref[mm, :, d:2 * d] = ek_ref[0]
            kmaug_ref[mm, 0:n_meta, 0:d] = kmeta_ref[:, mm * d:(mm + 1) * d]
            kmaug_ref[mm, :, d:2 * d] = ekmeta_ref[0]

    q = q_ref[0]
    eq = eq_ref[0]
    qa = [jnp.concatenate([q[:, mm * d:(mm + 1) * d], eq], axis=1) for mm in range(2)]

    def qk(mm, blk, cs):
        off = pl.multiple_of(blk * t, t)
        return _dot_nt(kaug_ref[mm, pl.ds(off, t), :], qa[mm][cs])

    def diag_fix(kd, c, s_t):
        lo = c * COL_TILE - kd * t
        if 0 <= lo < t:
            return s_t + core_ref[0, :, lo:lo + COL_TILE]
        return s_t

    def v_blk(blk):
        vb = vaug_ref[blk]
        return [vb, vb]

    _attention_sweep(2, qk, v_blk, lambda mm, cs: _dot_nt(kmaug_ref[mm], qa[mm][cs]),
                     [vtm_ref[...]] * 2, diag_fix,
                     (s0_ref, s1_ref), (mx0_ref, mx1_ref), (p0_ref, p1_ref), acc_ref,
                     qi * DIAG_BLOCKS)

    lam = (jnp.exp(jnp.sum(lq1_ref[...] * lk1_ref[...], axis=-1, keepdims=True))
           - jnp.exp(jnp.sum(lq2_ref[...] * lk2_ref[...], axis=-1, keepdims=True))
           + lambda_init)
    o_t = (acc_ref[0, 0:dv] / acc_ref[0, dv:dv + 1]
           - lam * (acc_ref[1, 0:dv] / acc_ref[1, dv:dv + 1]))
    ms = jnp.mean(o_t * o_t, axis=0, keepdims=True)
    o = (o_t * lax.rsqrt(ms + RMS_EPS)).T
    o = (o * subln_ref[...]) * (1.0 - lambda_init)
    o_ref[0] = (o * _silu(g_ref[0].astype(F32))).astype(o_ref.dtype)


def _diff_attn(z3, vt, z_meta, vt_meta, eq_tab, ek_tab, ekmeta_tab, core_tab, lams, subln,
               n_heads, blk, lambda_init, t, tq):
    b, s, _ = z3.shape
    n_meta = z_meta.shape[0]
    nq = s // tq
    q0, k0, g0 = blk
    vec = pl.BlockSpec((1, DIFF_HEAD_DIM), lambda bb, h, qi: (0, 0))
    kern = functools.partial(_diff_attn_kernel, lambda_init=lambda_init, t=t, n_meta=n_meta)
    return pl.pallas_call(
        kern,
        out_shape=jax.ShapeDtypeStruct((b, s, n_heads * DIFF_V_DIM), BF16),
        grid=(b, n_heads, nq),
        in_specs=[
            pl.BlockSpec((1, tq, HEAD_TILE), lambda bb, h, qi: (bb, qi, q0 + h)),
            pl.BlockSpec((1, s, HEAD_TILE), lambda bb, h, qi: (bb, 0, k0 + h)),
            pl.BlockSpec((1, s // t, DIFF_V_DIM, t), lambda bb, h, qi: (bb, 0, h, 0)),
            pl.BlockSpec((1, tq, DIFF_V_DIM), lambda bb, h, qi: (bb, qi, g0 + h)),
            pl.BlockSpec((n_meta, HEAD_TILE), lambda bb, h, qi: (0, k0 + h)),
            pl.BlockSpec((DIFF_V_DIM, n_meta), lambda bb, h, qi: (h, 0)),
            pl.BlockSpec((1, tq, LANES), lambda bb, h, qi: (h, qi, 0)),
            pl.BlockSpec((1, s, LANES), lambda bb, h, qi: (h, 0, 0)),
            pl.BlockSpec((1, LANES, LANES), lambda bb, h, qi: (h, 0, 0)),
            pl.BlockSpec((1, t, t), lambda bb, h, qi: (h, 0, 0)),
            vec, vec, vec, vec,
            pl.BlockSpec((1, DIFF_V_DIM), lambda bb, h, qi: (0, 0)),
        ],
        out_specs=pl.BlockSpec((1, tq, DIFF_V_DIM), lambda bb, h, qi: (bb, qi, h)),
        scratch_shapes=[
            pltpu.VMEM((2, s, HEAD_TILE), BF16),
            pltpu.VMEM((2, LANES, HEAD_TILE), BF16),
            pltpu.VMEM((s // t, DIFF_V_DIM + ONES_ROWS, t), BF16),
            pltpu.VMEM((DIFF_V_DIM + ONES_ROWS, LANES), BF16),
            pltpu.VMEM((2, t, tq), F32),
            pltpu.VMEM((2, t, tq), F32),
            pltpu.VMEM((2, 1, tq), F32),
            pltpu.VMEM((2, 1, tq), F32),
            pltpu.VMEM((2, t, tq), BF16),
            pltpu.VMEM((2, t, tq), BF16),
            pltpu.VMEM((2, DIFF_V_DIM + ONES_ROWS, tq), F32),
        ],
        compiler_params=_params("arbitrary", "arbitrary", "arbitrary"),
        name="diff_attn",
    )(z3, z3, vt, z3, z_meta, vt_meta, eq_tab, ek_tab, ekmeta_tab, core_tab, *lams, subln)


def _mla_attn_kernel(q_ref, k_ref, vt_ref, g_ref, kmeta_ref, vtmeta_ref, padk_ref,
                     o_ref, kmaug_ref, vaug_ref, vtm_ref, s0_ref, s1_ref, mx0_ref, mx1_ref,
                     p0_ref, p1_ref, acc_ref, *, t, n_meta, hp):
    qi = pl.program_id(2)
    dv = MLA_V_DIM
    heads = range(hp)

    @pl.when(qi == 0)
    def _build():
        for hh in heads:
            kmaug_ref[hh] = padk_ref[...]
            kmaug_ref[hh, 0:n_meta, :] = kmeta_ref[:, hh * HEAD_TILE:(hh + 1) * HEAD_TILE]
            _build_values(vaug_ref.at[hh], vtm_ref.at[hh], vt_ref[0, :, hh * dv:(hh + 1) * dv, :],
                          vtmeta_ref[hh * dv:(hh + 1) * dv, :], dv, n_meta)

    q = [q_ref[0, :, hh * HEAD_TILE:(hh + 1) * HEAD_TILE] for hh in heads]

    def qk(hh, blk, cs):
        off = pl.multiple_of(blk * t, t)
        return _dot_nt(k_ref[0, pl.ds(off, t), hh * HEAD_TILE:(hh + 1) * HEAD_TILE], q[hh][cs])

    _attention_sweep(hp, qk, lambda blk: [vaug_ref[hh, blk] for hh in heads],
                     lambda hh, cs: _dot_nt(kmaug_ref[hh], q[hh][cs]),
                     [vtm_ref[hh] for hh in heads], lambda kd, c, s_t: s_t,
                     (s0_ref, s1_ref), (mx0_ref, mx1_ref), (p0_ref, p1_ref), acc_ref,
                     qi * DIAG_BLOCKS)

    for hh in heads:
        o = (acc_ref[hh, 0:dv] / acc_ref[hh, dv:dv + 1]).T
        g = g_ref[0, :, hh * dv:(hh + 1) * dv].astype(F32)
        o_ref[0, :, hh * dv:(hh + 1) * dv] = (o * _silu(g)).astype(o_ref.dtype)


def _mla_attn(q3, k3, vt, z3, k_meta, vt_meta, padk, n_heads, g0, t, tq):
    b, s, _ = q3.shape
    n_meta = k_meta.shape[0]
    nq = s // tq
    hp = _tile(n_heads, MLA_HEADS_PER_STEP, 1)
    kern = functools.partial(_mla_attn_kernel, t=t, n_meta=n_meta, hp=hp)
    return pl.pallas_call(
        kern,
        out_shape=jax.ShapeDtypeStruct((b, s, n_heads * MLA_V_DIM), BF16),
        grid=(b, n_heads // hp, nq),
        in_specs=[
            pl.BlockSpec((1, tq, hp * HEAD_TILE), lambda bb, h, qi: (bb, qi, h)),
            pl.BlockSpec((1, s, hp * HEAD_TILE), lambda bb, h, qi: (bb, 0, h)),
            pl.BlockSpec((1, s // t, hp * MLA_V_DIM, t), lambda bb, h, qi: (bb, 0, h, 0)),
            pl.BlockSpec((1, tq, hp * MLA_V_DIM), lambda bb, h, qi: (bb, qi, g0 + h)),
            pl.BlockSpec((n_meta, hp * HEAD_TILE), lambda bb, h, qi: (0, h)),
            pl.BlockSpec((hp * MLA_V_DIM, n_meta), lambda bb, h, qi: (h, 0)),
            pl.BlockSpec((LANES, HEAD_TILE), lambda bb, h, qi: (0, 0)),
        ],
        out_specs=pl.BlockSpec((1, tq, hp * MLA_V_DIM), lambda bb, h, qi: (bb, qi, h)),
        scratch_shapes=[
            pltpu.VMEM((hp, LANES, HEAD_TILE), BF16),
            pltpu.VMEM((hp, s // t, MLA_V_DIM + ONES_ROWS, t), BF16),
            pltpu.VMEM((hp, MLA_V_DIM + ONES_ROWS, LANES), BF16),
            pltpu.VMEM((hp, t, tq), F32),
            pltpu.VMEM((hp, t, tq), F32),
            pltpu.VMEM((hp, 1, tq), F32),
            pltpu.VMEM((hp, 1, tq), F32),
            pltpu.VMEM((hp, t, tq), BF16),
            pltpu.VMEM((hp, t, tq), BF16),
            pltpu.VMEM((hp, MLA_V_DIM + ONES_ROWS, tq), F32),
        ],
        compiler_params=_params("arbitrary", "arbitrary", "arbitrary"),
        name="mla_attn",
    )(q3, k3, vt, z3, k_meta, vt_meta, padk)


def _outproj_kernel(a_ref, b_ref, wa_ref, wb_ref, x_ref, g_ref, o_ref, xs_ref, ss_ref,
                    *, n_steps, tn):
    j = pl.program_id(1)
    y = _dot(a_ref[...], wa_ref[...]) + _dot(b_ref[...], wb_ref[...])
    part = jnp.sum(y * y, axis=-1, keepdims=True)

    @pl.when(j == 0)
    def _():
        ss_ref[...] = part

    @pl.when(j > 0)
    def _():
        ss_ref[...] += part

    cols = pl.ds(pl.multiple_of(j * tn, tn), tn)
    o_ref[:, cols] = y
    xs_ref[:, cols] = x_ref[...]

    @pl.when(j == n_steps - 1)
    def _():
        r = lax.rsqrt(ss_ref[...] * (1.0 / o_ref.shape[1]) + RMS_EPS)
        o_ref[...] = xs_ref[...] + o_ref[...] * r * g_ref[...]


def _outproj(mix_a, mix_b, w, x2d, g):
    m, ka = mix_a.shape
    n = w.shape[1]
    assert mix_b.shape[1] == ka and w.shape[0] == 2 * ka
    tm = _tile(m, 512, 16)
    tn = _tile(n, 512)
    kern = functools.partial(_outproj_kernel, n_steps=n // tn, tn=tn)
    return pl.pallas_call(
        kern,
        out_shape=jax.ShapeDtypeStruct((m, n), F32),
        grid=(m // tm, n // tn),
        in_specs=[pl.BlockSpec((tm, ka), lambda i, j: (i, 0)),
                  pl.BlockSpec((tm, ka), lambda i, j: (i, 0)),
                  pl.BlockSpec((ka, tn), lambda i, j: (0, j)),
                  pl.BlockSpec((ka, tn), lambda i, j: (1, j)),
                  pl.BlockSpec((tm, tn), lambda i, j: (i, j)),
                  pl.BlockSpec((1, n), lambda i, j: (0, 0))],
        out_specs=pl.BlockSpec((tm, n), lambda i, j: (i, 0)),
        scratch_shapes=[pltpu.VMEM((tm, n), F32), pltpu.VMEM((tm, 1), F32)],
        compiler_params=_params("arbitrary", "arbitrary"),
        name="outproj",
    )(mix_a, mix_b, w, w, x2d, g)


def _split3(v):
    v = np.asarray(v, np.float32)
    hi = v.astype(NP_BF16)
    r = v - hi.astype(np.float32)
    mid = r.astype(NP_BF16)
    lo = (r - mid.astype(np.float32)).astype(NP_BF16)
    return hi, mid, lo


def _rope_table(pos):
    inv_freq = 1.0 / (ROPE_THETA ** (jnp.arange(0, MLA_ROPE, 2, dtype=F32) / MLA_ROPE))
    ang = pos.astype(F32)[:, None] * inv_freq[None, :]
    cos, sin = jnp.cos(ang), jnp.sin(ang)
    return jnp.concatenate([cos, cos, sin, sin], axis=1)


def _chunk_mask_lanes(seq):
    chunk = np.arange(seq) // CHUNK
    c = np.arange(seq // CHUNK - 1)
    q_lanes = (chunk[:, None] == c[None, :]).astype(np.float32)
    k_lanes = np.where(chunk[:, None] > c[None, :], MASK_VALUE, 0.0).astype(np.float32)
    return q_lanes.astype(NP_BF16), k_lanes.astype(NP_BF16)


def _diff_tables(n_heads, n_meta, seq, t):
    f32 = np.float32
    slopes = (2.0 ** (-8.0 * np.arange(1, n_heads + 1, dtype=f32) / n_heads)).astype(f32)
    slopes = (slopes * f32(LOG2E)).astype(f32)
    pos_q = ((np.arange(seq, dtype=f32) + n_meta)[None, :] * slopes[:, None]).astype(f32)
    pos_m = (np.arange(LANES, dtype=f32)[None, :] * slopes[:, None]).astype(f32)
    qc, kc = _chunk_mask_lanes(seq)
    n_c = qc.shape[1]
    assert 8 + n_c <= LANES
    ones_q = np.ones((n_heads, seq), NP_BF16)
    zeros_q = np.zeros((n_heads, seq), NP_BF16)
    qh, qm, ql = _split3(-pos_q)
    kh, km, kl = _split3(pos_q)
    tail = np.zeros((n_heads, seq, LANES - 8 - n_c), NP_BF16)

    def lanes(cols, chunk_lanes):
        return np.concatenate([np.stack(cols + [zeros_q], -1),
                               np.broadcast_to(chunk_lanes, (n_heads,) + chunk_lanes.shape), tail], -1)

    eq = lanes([qh, qm, ql, ones_q, ones_q, ones_q, ones_q], qc)
    ek = lanes([ones_q, ones_q, ones_q, zeros_q, kh, km, kl], kc)
    mh, mmid, ml = _split3(pos_m)
    valid = (np.arange(LANES) < n_meta)[None, :]
    padmask = np.broadcast_to(np.where(valid, 0.0, MASK_VALUE).astype(f32),
                              (n_heads, LANES)).astype(NP_BF16)
    ones_m = np.ones((n_heads, LANES), NP_BF16)
    ekm = np.concatenate(
        [np.stack([ones_m, ones_m, ones_m, padmask, mh, mmid, ml], -1),
         np.zeros((n_heads, LANES, LANES - 7), NP_BF16)], -1)
    j = np.arange(t)[:, None]
    i = np.arange(t)[None, :]
    same = (j // CHUNK) == (i // CHUNK)
    fut = np.where(same, 2.0 * np.maximum(j - i, 0), 0.0).astype(f32)
    core = (-slopes[:, None, None] * fut[None]).astype(f32)
    return tuple(jnp.asarray(a) for a in (eq, ek, ekm, core))


def _mla_mask_lanes(n_meta, seq):
    qc, kc = _chunk_mask_lanes(seq)
    n_c = qc.shape[1]
    assert MLA_ROPE + 1 + n_c <= LANES
    q = np.zeros((seq, LANES), NP_BF16)
    k = np.zeros((seq, LANES), NP_BF16)
    q[:, MLA_ROPE] = 1.0
    q[:, MLA_ROPE + 1:MLA_ROPE + 1 + n_c] = qc
    k[:, MLA_ROPE + 1:MLA_ROPE + 1 + n_c] = kc
    return jnp.asarray(q), jnp.asarray(k), jnp.zeros((n_meta, LANES), BF16)


def kernel(x, meta_tokens, norm_pre, w_in, diff_lambda_q1, diff_lambda_k1, diff_lambda_q2,
           diff_lambda_k2, diff_subln, mla_norm_q, mla_norm_kv, w_uq, w_ukv, w_out, norm_post):
    assert norm_pre.shape[0] == 1, "single-layer block"
    b, s, d = x.shape
    n_meta = meta_tokens.shape[0]
    mix = w_out.shape[1]
    diff_w = mix // 2
    mla_w = mix - diff_w
    ha = diff_w // DIFF_V_DIM
    hb = mla_w // MLA_V_DIM
    q_lora = mla_norm_q.shape[-1]
    kv_lora = mla_norm_kv.shape[-1]
    t = min(KEY_TILE, s)
    tq = min(QUERY_TILE, s)
    assert s % tq == 0 and tq == DIAG_BLOCKS * t and t % CHUNK == 0 and n_meta <= 16
    lambda_init = 0.8 - 0.6 * math.exp(-0.3 * 0)

    w_all = jnp.transpose(w_in[0]).astype(BF16)
    lat = q_lora + kv_lora
    tn = _tile(math.gcd(diff_w, lat), 1024)
    n_main = 3 * diff_w + lat
    assert (3 * diff_w) % q_lora == 0 and (3 * diff_w + q_lora) % kv_lora == 0
    q_blk0 = 0
    k_blk0 = diff_w // HEAD_TILE
    g_blk0 = 2 * diff_w // HEAD_TILE
    cq_block = 3 * diff_w // q_lora
    ckv_block = (3 * diff_w + q_lora) // kv_lora
    colscale = jnp.ones((1, n_main), F32).at[:, :diff_w].set(DIFF_HEAD_DIM ** -0.5 * LOG2E)
    half = MLA_ROPE // 2
    kr0 = 4 * diff_w + lat
    w_krt = w_all[kr0:kr0 + MLA_ROPE]
    w_kr2 = jnp.concatenate([w_krt, -w_krt[half:], w_krt[:half]], axis=0)
    w_mg16 = w_all[kr0 + MLA_ROPE:kr0 + MLA_ROPE + mla_w]

    wq = w_uq[0].reshape(q_lora, hb, MLA_NOPE + MLA_ROPE)
    wq_r = wq[:, :, MLA_NOPE:]
    wq_full = jnp.concatenate(
        [wq[:, :, :MLA_NOPE], wq_r, jnp.concatenate([-wq_r[:, :, half:], wq_r[:, :, :half]], -1)],
        axis=-1).reshape(q_lora, hb * HEAD_TILE).astype(BF16)
    wkv = w_ukv[0].reshape(kv_lora, hb, MLA_NOPE + MLA_V_DIM)
    w_kn = wkv[:, :, :MLA_NOPE].reshape(kv_lora, hb * MLA_NOPE).astype(BF16)
    w_vt = wkv[:, :, MLA_NOPE:].reshape(kv_lora, hb * MLA_V_DIM).T.astype(BF16)
    wo = w_out[0].astype(BF16)

    pos = jnp.arange(n_meta + s, dtype=jnp.int32)
    rope_tab = _rope_table(pos)
    tab_meta, tab_seq = rope_tab[:n_meta], rope_tab[n_meta:]
    eq_tab, ek_tab, ekm_tab, core_tab = _diff_tables(ha, n_meta, s, t)
    qlane_tab, klane_tab, klane_meta = _mla_mask_lanes(n_meta, s)
    lane = jnp.arange(HEAD_TILE)
    row = jnp.arange(LANES)
    padk = jnp.where((lane[None, :] == MLA_NOPE + MLA_ROPE) & (row[:, None] >= n_meta),
                     MASK_VALUE, 0.0).astype(BF16)

    x2d = x.reshape(b * s, d)
    u = _prenorm(x2d, norm_pre)
    u_meta = _prenorm(meta_tokens.astype(x.dtype), norm_pre)
    main_tiles = (n_main // tn, 2 * diff_w // tn, diff_w // tn)
    z = _inproj(u, w_all, colscale, tn, *main_tiles)
    z_meta = _inproj(u_meta, w_all, colscale, tn, *main_tiles)
    z_gate, kr = _inproj_gate_kr(u, w_mg16, w_kr2)
    _, kr_meta = _inproj_gate_kr(u_meta, w_mg16, w_kr2)
    vt_a = _v_transposed(w_all, 2, diff_w, u, b, t)
    vt_a_meta = _v_transposed(w_all, 2, diff_w, u_meta, 1, n_meta)[0, 0]

    g_cq = mla_norm_q.astype(F32)
    g_ckv = mla_norm_kv.astype(F32)
    q_b = _mla_q(z, g_cq, wq_full, tab_seq, qlane_tab, hb, q_lora, cq_block, s)
    k_b, vt_b = _mla_kv(z, kr, g_ckv, w_kn, w_vt, tab_seq, klane_tab, hb, kv_lora, ckv_block,
                        b, s, t)
    k_b_meta, vt_b_meta = _mla_kv(z_meta, kr_meta, g_ckv, w_kn, w_vt, tab_meta, klane_meta, hb,
                                  kv_lora, ckv_block, 1, n_meta, n_meta)
    vt_b_meta = vt_b_meta[0, 0]

    z3 = z.reshape(b, s, n_main)
    lams = [v.astype(F32) for v in (diff_lambda_q1, diff_lambda_k1, diff_lambda_q2, diff_lambda_k2)]
    mix_a = _diff_attn(z3, vt_a, z_meta, vt_a_meta, eq_tab, ek_tab, ekm_tab, core_tab, lams,
                       diff_subln.astype(F32), ha, (q_blk0, k_blk0, g_blk0), lambda_init, t, tq)
    mix_b = _mla_attn(q_b.reshape(b, s, hb * HEAD_TILE), k_b.reshape(b, s, hb * HEAD_TILE), vt_b,
                      z_gate.reshape(b, s, mla_w), k_b_meta, vt_b_meta, padk, hb, 0, t, tq)

    out = _outproj(mix_a.reshape(b * s, diff_w), mix_b.reshape(b * s, mla_w), wo, x2d,
                   norm_post.astype(F32))
    return out.reshape(b, s, d)
```

```python
import functools
import math

import jax
import jax.numpy as jnp
import numpy as np
from jax import lax
from jax.experimental import pallas as pl
from jax.experimental.pallas import tpu as pltpu

CHUNK = 64
RMS_EPS = 1e-6
MASK_VALUE = -1e30
DIFF_HEAD_DIM = 128
DIFF_V_DIM = 2 * DIFF_HEAD_DIM
MLA_NOPE = 128
MLA_ROPE = 64
MLA_V_DIM = 128
ROPE_THETA = 10000.0

LANES = 128
HEAD_TILE = 2 * LANES
VMEM_LIMIT = 56 * 1024 * 1024

KEY_TILE = 512
DIAG_BLOCKS = 2
QUERY_TILE = DIAG_BLOCKS * KEY_TILE
COL_TILE = 256
MLA_HEADS_PER_STEP = 2
M_INIT = -1e38
ONES_ROWS = 16
LOG2E = math.log2(math.e)

F32 = jnp.float32
BF16 = jnp.bfloat16
NP_BF16 = np.dtype(jnp.bfloat16)


def _params(*sem):
    return pltpu.CompilerParams(dimension_semantics=sem, vmem_limit_bytes=VMEM_LIMIT)


def _tile(dim, target, mult=LANES):
    if dim <= target:
        return dim
    t = (target // mult) * mult
    while t > mult and dim % t:
        t -= mult
    assert dim % t == 0, (dim, target)
    return t


def _dot(a, b):
    return jnp.dot(a, b, preferred_element_type=F32)


def _dot_nt(a, b):
    return lax.dot_general(a, b, (((1,), (1,)), ((), ())), preferred_element_type=F32)


def _prenorm_kernel(x_ref, g_ref, o_ref):
    x = x_ref[...]
    ms = jnp.mean(x * x, axis=-1, keepdims=True)
    o_ref[...] = (x * lax.rsqrt(ms + RMS_EPS) * g_ref[...]).astype(o_ref.dtype)


def _prenorm(x2d, g):
    m, d = x2d.shape
    tm = _tile(m, 256, 8)
    return pl.pallas_call(
        _prenorm_kernel,
        out_shape=jax.ShapeDtypeStruct((m, d), BF16),
        grid=(m // tm,),
        in_specs=[pl.BlockSpec((tm, d), lambda i: (i, 0)),
                  pl.BlockSpec((1, d), lambda i: (0, 0))],
        out_specs=pl.BlockSpec((tm, d), lambda i: (i, 0)),
        compiler_params=_params("arbitrary"),
        name="prenorm",
    )(x2d, g)


def _inproj_kernel(u_ref, w_ref, s_ref, z_ref):
    z_ref[...] = (_dot_nt(u_ref[...], w_ref[...]) * s_ref[...]).astype(z_ref.dtype)


def _inproj_kr_kernel(u_ref, w_ref, wkr_ref, z_ref, kr_ref):
    u = u_ref[...]
    z_ref[...] = _dot_nt(u, w_ref[...]).astype(z_ref.dtype)

    @pl.when(pl.program_id(1) == 0)
    def _():
        kr_ref[...] = _dot_nt(u, wkr_ref[...])


def _inproj(u, w_t, colscale, tn, n_tiles, skip_at, skip_tiles):
    m, d = u.shape
    tm = _tile(m, 1024, 16)

    def wrow(i, j):
        return jnp.where(j < skip_at, j, j + skip_tiles), 0

    return pl.pallas_call(
        _inproj_kernel,
        out_shape=jax.ShapeDtypeStruct((m, n_tiles * tn), BF16),
        grid=(m // tm, n_tiles),
        in_specs=[pl.BlockSpec((tm, d), lambda i, j: (i, 0)),
                  pl.BlockSpec((tn, d), wrow),
                  pl.BlockSpec((1, tn), lambda i, j: (0, j))],
        out_specs=pl.BlockSpec((tm, tn), lambda i, j: (i, j)),
        compiler_params=_params("arbitrary", "arbitrary"),
        name="inproj",
    )(u, w_t, colscale)


def _inproj_gate_kr(u, w_t, wkr_t):
    m, d = u.shape
    n = w_t.shape[0]
    tm = _tile(m, 1024, 16)
    tn = _tile(n, 1024)
    return pl.pallas_call(
        _inproj_kr_kernel,
        out_shape=(jax.ShapeDtypeStruct((m, n), BF16),
                   jax.ShapeDtypeStruct((m, LANES), F32)),
        grid=(m // tm, n // tn),
        in_specs=[pl.BlockSpec((tm, d), lambda i, j: (i, 0)),
                  pl.BlockSpec((tn, d), lambda i, j: (j, 0)),
                  pl.BlockSpec((LANES, d), lambda i, j: (0, 0))],
        out_specs=(pl.BlockSpec((tm, tn), lambda i, j: (i, j)),
                   pl.BlockSpec((tm, LANES), lambda i, j: (i, 0))),
        compiler_params=_params("arbitrary", "arbitrary"),
        name="inproj_gate",
    )(u, w_t, wkr_t)


def _vt_kernel(w_ref, u_ref, o_ref):
    o_ref[0, 0] = _dot_nt(w_ref[...], u_ref[...]).astype(o_ref.dtype)


def _v_transposed(w_t, row_block, n, u, batch, t):
    d = w_t.shape[1]
    m = u.shape[0]
    nt = m // (batch * t)
    return pl.pallas_call(
        _vt_kernel,
        out_shape=jax.ShapeDtypeStruct((batch, nt, n, t), BF16),
        grid=(m // t,),
        in_specs=[pl.BlockSpec((n, d), lambda i: (row_block, 0)),
                  pl.BlockSpec((t, d), lambda i: (i, 0))],
        out_specs=pl.BlockSpec((1, 1, n, t), lambda i: (i // nt, i % nt, 0, 0)),
        compiler_params=_params("arbitrary"),
        name="inproj_vt",
    )(w_t, u)


def _rope_combine(t):
    r = t + pltpu.roll(t, MLA_ROPE, axis=1)
    lane = lax.broadcasted_iota(jnp.int32, r.shape, 1)
    return jnp.where(lane < MLA_ROPE, r, 0.0), lane


def _mla_q_kernel(cq_ref, g_ref, w_ref, tab_ref, lanes_ref, o_ref, cqn_ref, *, heads_per_step, scale):
    @pl.when(pl.program_id(1) == 0)
    def _():
        c = cq_ref[...].astype(F32)
        ms = jnp.mean(c * c, axis=-1, keepdims=True)
        cqn_ref[...] = (c * lax.rsqrt(ms + RMS_EPS) * g_ref[...]).astype(cqn_ref.dtype)

    y = _dot(cqn_ref[...], w_ref[...])
    tab = tab_ref[...]
    mask_lanes = lanes_ref[...].astype(F32)
    for hh in range(heads_per_step):
        base = hh * HEAD_TILE
        nope = y[:, base:base + MLA_NOPE] * scale
        r, _ = _rope_combine(y[:, base + MLA_NOPE:base + HEAD_TILE] * tab)
        o_ref[:, base:base + MLA_NOPE] = nope.astype(o_ref.dtype)
        o_ref[:, base + MLA_NOPE:base + HEAD_TILE] = (r * scale + mask_lanes).astype(o_ref.dtype)


def _mla_q(z, g, w, tab, lanes_tab, n_heads, q_lora, cq_block, seq):
    m = z.shape[0]
    tm = _tile(m, 1024, 16)
    tm = math.gcd(tm, seq)
    hp = _tile(n_heads, 4, 1)
    nseq = seq // tm
    kern = functools.partial(_mla_q_kernel, heads_per_step=hp,
                             scale=float((MLA_NOPE + MLA_ROPE) ** -0.5 * LOG2E))
    return pl.pallas_call(
        kern,
        out_shape=jax.ShapeDtypeStruct((m, n_heads * HEAD_TILE), BF16),
        grid=(m // tm, n_heads // hp),
        in_specs=[pl.BlockSpec((tm, q_lora), lambda i, j: (i, cq_block)),
                  pl.BlockSpec((1, q_lora), lambda i, j: (0, 0)),
                  pl.BlockSpec((q_lora, hp * HEAD_TILE), lambda i, j: (0, j)),
                  pl.BlockSpec((tm, LANES), lambda i, j: (i % nseq, 0)),
                  pl.BlockSpec((tm, LANES), lambda i, j: (i % nseq, 0))],
        out_specs=pl.BlockSpec((tm, hp * HEAD_TILE), lambda i, j: (i, j)),
        scratch_shapes=[pltpu.VMEM((tm, q_lora), BF16)],
        compiler_params=_params("arbitrary", "arbitrary"),
        name="mla_q",
    )(z, g, w, tab, lanes_tab)


def _mla_kv_kernel(ckv_ref, kr_ref, g_ref, wk_ref, wvt_ref, tab_ref, lanes_ref, k_ref, vt_ref,
                   *, n_heads, t):
    c = ckv_ref[...].astype(F32)
    ms = jnp.mean(c * c, axis=-1, keepdims=True)
    cn = (c * lax.rsqrt(ms + RMS_EPS) * g_ref[...]).astype(BF16)
    kn = _dot(cn, wk_ref[...])
    kr, _ = _rope_combine(kr_ref[...] * tab_ref[...])
    kr = (kr + lanes_ref[...].astype(F32)).astype(k_ref.dtype)
    for h in range(n_heads):
        k_ref[:, h * HEAD_TILE:h * HEAD_TILE + MLA_NOPE] = (
            kn[:, h * MLA_NOPE:(h + 1) * MLA_NOPE].astype(k_ref.dtype))
        k_ref[:, h * HEAD_TILE + MLA_NOPE:(h + 1) * HEAD_TILE] = kr
    vt = _dot_nt(wvt_ref[...], cn).astype(vt_ref.dtype)
    for tt in range(vt_ref.shape[1]):
        vt_ref[0, tt] = vt[:, tt * t:(tt + 1) * t]


def _mla_kv(z, kr, g, wk, wvt, tab, lanes_tab, n_heads, kv_lora, ckv_block, batch, seq, t):
    m = z.shape[0]
    tm = math.gcd(_tile(m, 1024, 16), seq)
    t = min(t, tm)
    nseq = seq // tm
    kern = functools.partial(_mla_kv_kernel, n_heads=n_heads, t=t)
    return pl.pallas_call(
        kern,
        out_shape=(jax.ShapeDtypeStruct((m, n_heads * HEAD_TILE), BF16),
                   jax.ShapeDtypeStruct((batch, seq // t, n_heads * MLA_V_DIM, t), BF16)),
        grid=(m // tm,),
        in_specs=[pl.BlockSpec((tm, kv_lora), lambda i: (i, ckv_block)),
                  pl.BlockSpec((tm, LANES), lambda i: (i, 0)),
                  pl.BlockSpec((1, kv_lora), lambda i: (0, 0)),
                  pl.BlockSpec((kv_lora, n_heads * MLA_NOPE), lambda i: (0, 0)),
                  pl.BlockSpec((n_heads * MLA_V_DIM, kv_lora), lambda i: (0, 0)),
                  pl.BlockSpec((tm, LANES), lambda i: (i % nseq, 0)),
                  pl.BlockSpec((tm, LANES), lambda i: (i % nseq, 0))],
        out_specs=(pl.BlockSpec((tm, n_heads * HEAD_TILE), lambda i: (i, 0)),
                   pl.BlockSpec((1, tm // t, n_heads * MLA_V_DIM, t),
                                lambda i: (i // nseq, i % nseq, 0, 0))),
        compiler_params=_params("arbitrary"),
        name="mla_kv",
    )(z, kr, g, wk, wvt, tab, lanes_tab)


def _col_max(s_t):
    return jnp.max(s_t, axis=0, keepdims=True)


def _score_stage(s_ref, mx_ref, mm, cs, s_t):
    s_ref[mm, :, cs] = s_t
    mx_ref[mm, :, cs] = _col_max(s_t)


def _exp_stage(s_ref, mx_ref, p_ref, mm, cs, m_prev):
    m_new = jnp.maximum(m_prev, mx_ref[mm, :, cs])
    p_ref[mm, :, cs] = jnp.exp2(s_ref[mm, :, cs] - m_new).astype(BF16)
    return m_new, jnp.exp2(m_prev - m_new)


def _acc_stage(p_ref, acc_ref, mm, cs, vt_aug, alpha):
    acc_ref[mm, :, cs] = alpha * acc_ref[mm, :, cs] + _dot(vt_aug, p_ref[mm, :, cs])


def _silu(g):
    return g * jax.nn.sigmoid(g)


def _build_values(vaug_ref, vtm_ref, vt, vt_meta, dv, n_meta):
    vaug_ref[:, 0:dv, :] = vt
    vaug_ref[:, dv:, :] = jnp.ones((vaug_ref.shape[0], ONES_ROWS, vaug_ref.shape[2]), BF16)
    vtm_ref[0:dv, :] = jnp.zeros((dv, vtm_ref.shape[1]), BF16)
    vtm_ref[0:dv, 0:n_meta] = vt_meta
    vtm_ref[dv:, :] = jnp.ones((ONES_ROWS, vtm_ref.shape[1]), BF16)


def _attention_sweep(n_maps, qk, v_blk, meta_scores, vtm, diag_fix, s_refs, mx_refs, p_refs,
                     acc_ref, n_full):
    maps = range(n_maps)
    tq = acc_ref.shape[2]
    cols = [slice(c, c + COL_TILE) for c in range(0, tq, COL_TILE)]

    def score(slot, blk, c, fix=None):
        for mm in maps:
            s_t = qk(mm, blk, cols[c])
            if fix is not None:
                s_t = diag_fix(fix, c, s_t)
            _score_stage(s_refs[slot], mx_refs[slot], mm, cols[c], s_t)

    def expo(slot, c, m, alpha):
        for mm in maps:
            m[mm][c], alpha[mm][c] = _exp_stage(s_refs[slot], mx_refs[slot], p_refs[slot], mm,
                                                cols[c], m[mm][c])

    def accum(slot, vb, c, alpha):
        for mm in maps:
            _acc_stage(p_refs[slot], acc_ref, mm, cols[c], vb[mm], alpha[mm][c])

    def blank():
        return [[None] * len(cols) for _ in maps]

    m, a0, a1 = blank(), blank(), blank()
    for c in range(len(cols)):
        s_meta = [meta_scores(mm, cols[c]) for mm in maps]
        score(0, n_full, c, fix=0)
        score(1, n_full + 1, c, fix=1)
        for mm in maps:
            mx = _col_max(s_meta[mm])
            acc_ref[mm, :, cols[c]] = _dot(vtm[mm], jnp.exp2(s_meta[mm] - mx).astype(BF16))
            m[mm][c] = mx
    for c in range(len(cols)):
        expo(0, c, m, a0)

    def body(i, carry):
        m, a0 = [list(x) for x in carry[0]], [list(x) for x in carry[1]]
        a1 = blank()
        first = i == 0
        vb0 = v_blk(jnp.where(first, n_full, 2 * i - 2))
        vb1 = v_blk(jnp.where(first, n_full + 1, 2 * i - 1))
        for c in range(len(cols)):
            accum(0, vb0, c, a0)
            expo(1, c, m, a1)
            score(0, 2 * i, c)
        for c in range(len(cols)):
            accum(1, vb1, c, a1)
            expo(0, c, m, a0)
            score(1, 2 * i + 1, c)
        return m, a0

    m, a0 = lax.fori_loop(0, n_full // 2, body, (m, a0))
    m, a0 = [list(x) for x in m], [list(x) for x in a0]

    last = jnp.where(n_full == 0, 0, n_full - 2)
    vb0, vb1 = v_blk(last), v_blk(last + 1)
    for c in range(len(cols)):
        accum(0, vb0, c, a0)
        expo(1, c, m, a1)
    for c in range(len(cols)):
        accum(1, vb1, c, a1)


def _diff_attn_kernel(q_ref, k_ref, vt_ref, g_ref, kmeta_ref, vtmeta_ref, eq_ref, ek_ref,
                      ekmeta_ref, core_ref, lq1_ref, lk1_ref, lq2_ref, lk2_ref, subln_ref,
                      o_ref, kaug_ref, kmaug_ref, vaug_ref, vtm_ref, s0_ref, s1_ref, mx0_ref, mx1_ref,
                      p0_ref, p1_ref, acc_ref, *, lambda_init, t, n_meta):
    d = DIFF_HEAD_DIM
    dv = DIFF_V_DIM
    tq = acc_ref.shape[2]

    kmaug_ref[...] = jnp.zeros(kmaug_ref.shape, BF16)
    _build_values(vaug_ref, vtm_ref, vt_ref[0], vtmeta_ref[...], dv, n_meta)
    for mm in range(2):
        kaug_ref[mm, :, 0:d] = k_ref[0, :, mm * d:(mm + 1) * d]
        kaug_ref[mm, :, d:2 * d] = ek_ref[0]
        kmaug_ref[mm, 0:n_meta, 0:d] = kmeta_ref[:, mm * d:(mm + 1) * d]
        kmaug_ref[mm, :, d:2 * d] = ekmeta_ref[0]

    def diag_fix(kd, c, s_t):
        lo = c * COL_TILE - kd * t
        if 0 <= lo < t:
            return s_t + core_ref[0, :, lo:lo + COL_TILE]
        return s_t

    def v_blk(blk):
        vb = vaug_ref[blk]
        return [vb, vb]

    lam = (jnp.exp(jnp.sum(lq1_ref[...] * lk1_ref[...], axis=-1, keepdims=True))
           - jnp.exp(jnp.sum(lq2_ref[...] * lk2_ref[...], axis=-1, keepdims=True))
           + lambda_init)

    for qi in range(q_ref.shape[1] // tq):
        rows = slice(qi * tq, (qi + 1) * tq)
        q = q_ref[0, rows, :]
        eq = eq_ref[0, rows, :]
        qa = [jnp.concatenate([q[:, mm * d:(mm + 1) * d], eq], axis=1) for mm in range(2)]

        def qk(mm, blk, cs, qa=qa):
            off = pl.multiple_of(blk * t, t)
            return _dot_nt(kaug_ref[mm, pl.ds(off, t), :], qa[mm][cs])

        _attention_sweep(2, qk, v_blk, lambda mm, cs, qa=qa: _dot_nt(kmaug_ref[mm], qa[mm][cs]),
                         [vtm_ref[...]] * 2, diag_fix,
                         (s0_ref, s1_ref), (mx0_ref, mx1_ref), (p0_ref, p1_ref), acc_ref,
                         qi * DIAG_BLOCKS)

        o_t = (acc_ref[0, 0:dv] / acc_ref[0, dv:dv + 1]
               - lam * (acc_ref[1, 0:dv] / acc_ref[1, dv:dv + 1]))
        ms = jnp.mean(o_t * o_t, axis=0, keepdims=True)
        o = (o_t * lax.rsqrt(ms + RMS_EPS)).T
        o = (o * subln_ref[...]) * (1.0 - lambda_init)
        o_ref[0, rows, :] = (o * _silu(g_ref[0, rows, :].astype(F32))).astype(o_ref.dtype)


def _diff_attn(z3, vt, z_meta, vt_meta, eq_tab, ek_tab, ekmeta_tab, core_tab, lams, subln,
               n_heads, blk, lambda_init, t, tq):
    b, s, _ = z3.shape
    n_meta = z_meta.shape[0]
    q0, k0, g0 = blk
    vec = pl.BlockSpec((1, DIFF_HEAD_DIM), lambda bb, h: (0, 0))
    kern = functools.partial(_diff_attn_kernel, lambda_init=lambda_init, t=t, n_meta=n_meta)
    return pl.pallas_call(
        kern,
        out_shape=jax.ShapeDtypeStruct((b, s, n_heads * DIFF_V_DIM), BF16),
        grid=(b, n_heads),
        in_specs=[
            pl.BlockSpec((1, s, HEAD_TILE), lambda bb, h: (bb, 0, q0 + h)),
            pl.BlockSpec((1, s, HEAD_TILE), lambda bb, h: (bb, 0, k0 + h)),
            pl.BlockSpec((1, s // t, DIFF_V_DIM, t), lambda bb, h: (bb, 0, h, 0)),
            pl.BlockSpec((1, s, DIFF_V_DIM), lambda bb, h: (bb, 0, g0 + h)),
            pl.BlockSpec((n_meta, HEAD_TILE), lambda bb, h: (0, k0 + h)),
            pl.BlockSpec((DIFF_V_DIM, n_meta), lambda bb, h: (h, 0)),
            pl.BlockSpec((1, s, LANES), lambda bb, h: (h, 0, 0)),
            pl.BlockSpec((1, s, LANES), lambda bb, h: (h, 0, 0)),
            pl.BlockSpec((1, LANES, LANES), lambda bb, h: (h, 0, 0)),
            pl.BlockSpec((1, t, t), lambda bb, h: (h, 0, 0)),
            vec, vec, vec, vec,
            pl.BlockSpec((1, DIFF_V_DIM), lambda bb, h: (0, 0)),
        ],
        out_specs=pl.BlockSpec((1, s, DIFF_V_DIM), lambda bb, h: (bb, 0, h)),
        scratch_shapes=[
            pltpu.VMEM((2, s, HEAD_TILE), BF16),
            pltpu.VMEM((2, LANES, HEAD_TILE), BF16),
            pltpu.VMEM((s // t, DIFF_V_DIM + ONES_ROWS, t), BF16),
            pltpu.VMEM((DIFF_V_DIM + ONES_ROWS, LANES), BF16),
            pltpu.VMEM((2, t, tq), F32),
            pltpu.VMEM((2, t, tq), F32),
            pltpu.VMEM((2, 1, tq), F32),
            pltpu.VMEM((2, 1, tq), F32),
            pltpu.VMEM((2, t, tq), BF16),
            pltpu.VMEM((2, t, tq), BF16),
            pltpu.VMEM((2, DIFF_V_DIM + ONES_ROWS, tq), F32),
        ],
        compiler_params=_params("arbitrary", "arbitrary"),
        name="diff_attn",
    )(z3, z3, vt, z3, z_meta, vt_meta, eq_tab, ek_tab, ekmeta_tab, core_tab, *lams, subln)


def _mla_attn_kernel(q_ref, k_ref, vt_ref, g_ref, kmeta_ref, vtmeta_ref, padk_ref,
                     o_ref, kmaug_ref, vaug_ref, vtm_ref, s0_ref, s1_ref, mx0_ref, mx1_ref,
                     p0_ref, p1_ref, acc_ref, *, t, n_meta, hp):
    dv = MLA_V_DIM
    heads = range(hp)
    tq = acc_ref.shape[2]

    for hh in heads:
        kmaug_ref[hh] = padk_ref[...]
        kmaug_ref[hh, 0:n_meta, :] = kmeta_ref[:, hh * HEAD_TILE:(hh + 1) * HEAD_TILE]
        _build_values(vaug_ref.at[hh], vtm_ref.at[hh], vt_ref[0, :, hh * dv:(hh + 1) * dv, :],
                      vtmeta_ref[hh * dv:(hh + 1) * dv, :], dv, n_meta)

    for qi in range(q_ref.shape[1] // tq):
        rows = slice(qi * tq, (qi + 1) * tq)
        q = [q_ref[0, rows, hh * HEAD_TILE:(hh + 1) * HEAD_TILE] for hh in heads]

        def qk(hh, blk, cs, q=q):
            off = pl.multiple_of(blk * t, t)
            return _dot_nt(k_ref[0, pl.ds(off, t), hh * HEAD_TILE:(hh + 1) * HEAD_TILE], q[hh][cs])

        _attention_sweep(hp, qk, lambda blk: [vaug_ref[hh, blk] for hh in heads],
                         lambda hh, cs, q=q: _dot_nt(kmaug_ref[hh], q[hh][cs]),
                         [vtm_ref[hh] for hh in heads], lambda kd, c, s_t: s_t,
                         (s0_ref, s1_ref), (mx0_ref, mx1_ref), (p0_ref, p1_ref), acc_ref,
                         qi * DIAG_BLOCKS)

        for hh in heads:
            o = (acc_ref[hh, 0:dv] / acc_ref[hh, dv:dv + 1]).T
            g = g_ref[0, rows, hh * dv:(hh + 1) * dv].astype(F32)
            o_ref[0, rows, hh * dv:(hh + 1) * dv] = (o * _silu(g)).astype(o_ref.dtype)


def _mla_attn(q3, k3, vt, z3, k_meta, vt_meta, padk, n_heads, g0, t, tq):
    b, s, _ = q3.shape
    n_meta = k_meta.shape[0]
    hp = _tile(n_heads, MLA_HEADS_PER_STEP, 1)
    kern = functools.partial(_mla_attn_kernel, t=t, n_meta=n_meta, hp=hp)
    return pl.pallas_call(
        kern,
        out_shape=jax.ShapeDtypeStruct((b, s, n_heads * MLA_V_DIM), BF16),
        grid=(b, n_heads // hp),
        in_specs=[
            pl.BlockSpec((1, s, hp * HEAD_TILE), lambda bb, h: (bb, 0, h)),
            pl.BlockSpec((1, s, hp * HEAD_TILE), lambda bb, h: (bb, 0, h)),
            pl.BlockSpec((1, s // t, hp * MLA_V_DIM, t), lambda bb, h: (bb, 0, h, 0)),
            pl.BlockSpec((1, s, hp * MLA_V_DIM), lambda bb, h: (bb, 0, g0 + h)),
            pl.BlockSpec((n_meta, hp * HEAD_TILE), lambda bb, h: (0, h)),
            pl.BlockSpec((hp * MLA_V_DIM, n_meta), lambda bb, h: (h, 0)),
            pl.BlockSpec((LANES, HEAD_TILE), lambda bb, h: (0, 0)),
        ],
        out_specs=pl.BlockSpec((1, s, hp * MLA_V_DIM), lambda bb, h: (bb, 0, h)),
        scratch_shapes=[
            pltpu.VMEM((hp, LANES, HEAD_TILE), BF16),
            pltpu.VMEM((hp, s // t, MLA_V_DIM + ONES_ROWS, t), BF16),
            pltpu.VMEM((hp, MLA_V_DIM + ONES_ROWS, LANES), BF16),
            pltpu.VMEM((hp, t, tq), F32),
            pltpu.VMEM((hp, t, tq), F32),
            pltpu.VMEM((hp, 1, tq), F32),
            pltpu.VMEM((hp, 1, tq), F32),
            pltpu.VMEM((hp, t, tq), BF16),
            pltpu.VMEM((hp, t, tq), BF16),
            pltpu.VMEM((hp, MLA_V_DIM + ONES_ROWS, tq), F32),
        ],
        compiler_params=_params("arbitrary", "arbitrary"),
        name="mla_attn",
    )(q3, k3, vt, z3, k_meta, vt_meta, padk)


def _outproj_kernel(a_ref, b_ref, wa_ref, wb_ref, y_ref):
    y = _dot(a_ref[...], wa_ref[...]) + _dot(b_ref[...], wb_ref[...])
    y_ref[...] = y.astype(y_ref.dtype)


def _outproj(mix_a, mix_b, w):
    m, ka = mix_a.shape
    n = w.shape[1]
    assert mix_b.shape[1] == ka and w.shape[0] == 2 * ka
    tm = _tile(m, 1024, 16)
    tn = _tile(n, 1024)
    return pl.pallas_call(
        _outproj_kernel,
        out_shape=jax.ShapeDtypeStruct((m, n), BF16),
        grid=(m // tm, n // tn),
        in_specs=[pl.BlockSpec((tm, ka), lambda i, j: (i, 0)),
                  pl.BlockSpec((tm, ka), lambda i, j: (i, 0)),
                  pl.BlockSpec((ka, tn), lambda i, j: (0, j)),
                  pl.BlockSpec((ka, tn), lambda i, j: (1, j))],
        out_specs=pl.BlockSpec((tm, tn), lambda i, j: (i, j)),
        compiler_params=_params("arbitrary", "arbitrary"),
        name="outproj",
    )(mix_a, mix_b, w, w)


def _postnorm_kernel(x_ref, y_ref, g_ref, o_ref):
    y = y_ref[...].astype(F32)
    ms = jnp.mean(y * y, axis=-1, keepdims=True)
    o_ref[...] = x_ref[...] + y * lax.rsqrt(ms + RMS_EPS) * g_ref[...]


def _postnorm(x2d, y, g):
    m, d = x2d.shape
    tm = _tile(m, 256, 16)
    return pl.pallas_call(
        _postnorm_kernel,
        out_shape=jax.ShapeDtypeStruct((m, d), F32),
        grid=(m // tm,),
        in_specs=[pl.BlockSpec((tm, d), lambda i: (i, 0)),
                  pl.BlockSpec((tm, d), lambda i: (i, 0)),
                  pl.BlockSpec((1, d), lambda i: (0, 0))],
        out_specs=pl.BlockSpec((tm, d), lambda i: (i, 0)),
        compiler_params=_params("arbitrary"),
        name="postnorm",
    )(x2d, y, g)


def _split3(v):
    v = np.asarray(v, np.float32)
    hi = v.astype(NP_BF16)
    r = v - hi.astype(np.float32)
    mid = r.astype(NP_BF16)
    lo = (r - mid.astype(np.float32)).astype(NP_BF16)
    return hi, mid, lo


def _rope_table(pos):
    inv_freq = 1.0 / (ROPE_THETA ** (jnp.arange(0, MLA_ROPE, 2, dtype=F32) / MLA_ROPE))
    ang = pos.astype(F32)[:, None] * inv_freq[None, :]
    cos, sin = jnp.cos(ang), jnp.sin(ang)
    return jnp.concatenate([cos, cos, sin, sin], axis=1)


def _chunk_mask_lanes(seq):
    chunk = np.arange(seq) // CHUNK
    c = np.arange(seq // CHUNK - 1)
    q_lanes = (chunk[:, None] == c[None, :]).astype(np.float32)
    k_lanes = np.where(chunk[:, None] > c[None, :], MASK_VALUE, 0.0).astype(np.float32)
    return q_lanes.astype(NP_BF16), k_lanes.astype(NP_BF16)


def _diff_tables(n_heads, n_meta, seq, t):
    f32 = np.float32
    slopes = (2.0 ** (-8.0 * np.arange(1, n_heads + 1, dtype=f32) / n_heads)).astype(f32)
    slopes = (slopes * f32(LOG2E)).astype(f32)
    pos_q = ((np.arange(seq, dtype=f32) + n_meta)[None, :] * slopes[:, None]).astype(f32)
    pos_m = (np.arange(LANES, dtype=f32)[None, :] * slopes[:, None]).astype(f32)
    qc, kc = _chunk_mask_lanes(seq)
    n_c = qc.shape[1]
    assert 8 + n_c <= LANES
    ones_q = np.ones((n_heads, seq), NP_BF16)
    zeros_q = np.zeros((n_heads, seq), NP_BF16)
    qh, qm, ql = _split3(-pos_q)
    kh, km, kl = _split3(pos_q)
    tail = np.zeros((n_heads, seq, LANES - 8 - n_c), NP_BF16)

    def lanes(cols, chunk_lanes):
        return np.concatenate([np.stack(cols + [zeros_q], -1),
                               np.broadcast_to(chunk_lanes, (n_heads,) + chunk_lanes.shape), tail], -1)

    eq = lanes([qh, qm, ql, ones_q, ones_q, ones_q, ones_q], qc)
    ek = lanes([ones_q, ones_q, ones_q, zeros_q, kh, km, kl], kc)
    mh, mmid, ml = _split3(pos_m)
    valid = (np.arange(LANES) < n_meta)[None, :]
    padmask = np.broadcast_to(np.where(valid, 0.0, MASK_VALUE).astype(f32),
                              (n_heads, LANES)).astype(NP_BF16)
    ones_m = np.ones((n_heads, LANES), NP_BF16)
    ekm = np.concatenate(
        [np.stack([ones_m, ones_m, ones_m, padmask, mh, mmid, ml], -1),
         np.zeros((n_heads, LANES, LANES - 7), NP_BF16)], -1)
    j = np.arange(t)[:, None]
    i = np.arange(t)[None, :]
    same = (j // CHUNK) == (i // CHUNK)
    fut = np.where(same, 2.0 * np.maximum(j - i, 0), 0.0).astype(f32)
    core = (-slopes[:, None, None] * fut[None]).astype(f32)
    return tuple(jnp.asarray(a) for a in (eq, ek, ekm, core))


def _mla_mask_lanes(n_meta, seq):
    qc, kc = _chunk_mask_lanes(seq)
    n_c = qc.shape[1]
    assert MLA_ROPE + 1 + n_c <= LANES
    q = np.zeros((seq, LANES), NP_BF16)
    k = np.zeros((seq, LANES), NP_BF16)
    q[:, MLA_ROPE] = 1.0
    q[:, MLA_ROPE + 1:MLA_ROPE + 1 + n_c] = qc
    k[:, MLA_ROPE + 1:MLA_ROPE + 1 + n_c] = kc
    return jnp.asarray(q), jnp.asarray(k), jnp.zeros((n_meta, LANES), BF16)


def kernel(x, meta_tokens, norm_pre, w_in, diff_lambda_q1, diff_lambda_k1, diff_lambda_q2,
           diff_lambda_k2, diff_subln, mla_norm_q, mla_norm_kv, w_uq, w_ukv, w_out, norm_post):
    assert norm_pre.shape[0] == 1, "single-layer block"
    b, s, d = x.shape
    n_meta = meta_tokens.shape[0]
    mix = w_out.shape[1]
    diff_w = mix // 2
    mla_w = mix - diff_w
    ha = diff_w // DIFF_V_DIM
    hb = mla_w // MLA_V_DIM
    q_lora = mla_norm_q.shape[-1]
    kv_lora = mla_norm_kv.shape[-1]
    t = min(KEY_TILE, s)
    tq = min(QUERY_TILE, s)
    assert s % tq == 0 and tq == DIAG_BLOCKS * t and t % CHUNK == 0 and n_meta <= 16
    lambda_init = 0.8 - 0.6 * math.exp(-0.3 * 0)

    w_all = jnp.transpose(w_in[0]).astype(BF16)
    lat = q_lora + kv_lora
    tn = _tile(math.gcd(diff_w, lat), 1024)
    n_main = 3 * diff_w + lat
    assert (3 * diff_w) % q_lora == 0 and (3 * diff_w + q_lora) % kv_lora == 0
    q_blk0 = 0
    k_blk0 = diff_w // HEAD_TILE
    g_blk0 = 2 * diff_w // HEAD_TILE
    cq_block = 3 * diff_w // q_lora
    ckv_block = (3 * diff_w + q_lora) // kv_lora
    colscale = jnp.ones((1, n_main), F32).at[:, :diff_w].set(DIFF_HEAD_DIM ** -0.5 * LOG2E)
    half = MLA_ROPE // 2
    kr0 = 4 * diff_w + lat
    w_krt = w_all[kr0:kr0 + MLA_ROPE]
    w_kr2 = jnp.concatenate([w_krt, -w_krt[half:], w_krt[:half]], axis=0)
    w_mg16 = w_all[kr0 + MLA_ROPE:kr0 + MLA_ROPE + mla_w]

    wq = w_uq[0].reshape(q_lora, hb, MLA_NOPE + MLA_ROPE)
    wq_r = wq[:, :, MLA_NOPE:]
    wq_full = jnp.concatenate(
        [wq[:, :, :MLA_NOPE], wq_r, jnp.concatenate([-wq_r[:, :, half:], wq_r[:, :, :half]], -1)],
        axis=-1).reshape(q_lora, hb * HEAD_TILE).astype(BF16)
    wkv = w_ukv[0].reshape(kv_lora, hb, MLA_NOPE + MLA_V_DIM)
    w_kn = wkv[:, :, :MLA_NOPE].reshape(kv_lora, hb * MLA_NOPE).astype(BF16)
    w_vt = wkv[:, :, MLA_NOPE:].reshape(kv_lora, hb * MLA_V_DIM).T.astype(BF16)
    wo = w_out[0].astype(BF16)

    pos = jnp.arange(n_meta + s, dtype=jnp.int32)
    rope_tab = _rope_table(pos)
    tab_meta, tab_seq = rope_tab[:n_meta], rope_tab[n_meta:]
    eq_tab, ek_tab, ekm_tab, core_tab = _diff_tables(ha, n_meta, s, t)
    qlane_tab, klane_tab, klane_meta = _mla_mask_lanes(n_meta, s)
    lane = jnp.arange(HEAD_TILE)
    row = jnp.arange(LANES)
    padk = jnp.where((lane[None, :] == MLA_NOPE + MLA_ROPE) & (row[:, None] >= n_meta),
                     MASK_VALUE, 0.0).astype(BF16)

    x2d = x.reshape(b * s, d)
    u = _prenorm(x2d, norm_pre)
    u_meta = _prenorm(meta_tokens.astype(x.dtype), norm_pre)
    main_tiles = (n_main // tn, 2 * diff_w // tn, diff_w // tn)
    z = _inproj(u, w_all, colscale, tn, *main_tiles)
    z_meta = _inproj(u_meta, w_all, colscale, tn, *main_tiles)
    z_gate, kr = _inproj_gate_kr(u, w_mg16, w_kr2)
    _, kr_meta = _inproj_gate_kr(u_meta, w_mg16, w_kr2)
    vt_a = _v_transposed(w_all, 2, diff_w, u, b, t)
    vt_a_meta = _v_transposed(w_all, 2, diff_w, u_meta, 1, n_meta)[0, 0]

    g_cq = mla_norm_q.astype(F32)
    g_ckv = mla_norm_kv.astype(F32)
    q_b = _mla_q(z, g_cq, wq_full, tab_seq, qlane_tab, hb, q_lora, cq_block, s)
    k_b, vt_b = _mla_kv(z, kr, g_ckv, w_kn, w_vt, tab_seq, klane_tab, hb, kv_lora, ckv_block,
                        b, s, t)
    k_b_meta, vt_b_meta = _mla_kv(z_meta, kr_meta, g_ckv, w_kn, w_vt, tab_meta, klane_meta, hb,
                                  kv_lora, ckv_block, 1, n_meta, n_meta)
    vt_b_meta = vt_b_meta[0, 0]

    z3 = z.reshape(b, s, n_main)
    lams = [v.astype(F32) for v in (diff_lambda_q1, diff_lambda_k1, diff_lambda_q2, diff_lambda_k2)]
    mix_a = _diff_attn(z3, vt_a, z_meta, vt_a_meta, eq_tab, ek_tab, ekm_tab, core_tab, lams,
                       diff_subln.astype(F32), ha, (q_blk0, k_blk0, g_blk0), lambda_init, t, tq)
    mix_b = _mla_attn(q_b.reshape(b, s, hb * HEAD_TILE), k_b.reshape(b, s, hb * HEAD_TILE), vt_b,
                      z_gate.reshape(b, s, mla_w), k_b_meta, vt_b_meta, padk, hb, 0, t, tq)

    y = _outproj(mix_a.reshape(b * s, diff_w), mix_b.reshape(b * s, mla_w), wo)
    out = _postnorm(x2d, y, norm_post.astype(F32))
    return out.reshape(b, s, d)
```

```python
import functools
import math

import jax
import jax.numpy as jnp
import numpy as np
from jax import lax
from jax.experimental import pallas as pl
from jax.experimental.pallas import tpu as pltpu

CHUNK = 64
RMS_EPS = 1e-6
MASK_VALUE = -1e30
DIFF_HEAD_DIM = 128
DIFF_V_DIM = 2 * DIFF_HEAD_DIM
MLA_NOPE = 128
MLA_ROPE = 64
MLA_V_DIM = 128
ROPE_THETA = 10000.0

LANES = 128
HEAD_TILE = 2 * LANES
VMEM_LIMIT = 56 * 1024 * 1024

KEY_TILE = 512
DIAG_BLOCKS = 2
QUERY_TILE = DIAG_BLOCKS * KEY_TILE
COL_TILE = 256
MLA_HEADS_PER_STEP = 2
M_INIT = -1e38
ONES_ROWS = 16
LOG2E = math.log2(math.e)

F32 = jnp.float32
BF16 = jnp.bfloat16
NP_BF16 = np.dtype(jnp.bfloat16)


def _params(*sem):
    return pltpu.CompilerParams(dimension_semantics=sem, vmem_limit_bytes=VMEM_LIMIT)


def _tile(dim, target, mult=LANES):
    if dim <= target:
        return dim
    t = (target // mult) * mult
    while t > mult and dim % t:
        t -= mult
    assert dim % t == 0, (dim, target)
    return t


def _dot(a, b):
    return jnp.dot(a, b, preferred_element_type=F32)


def _dot_nt(a, b):
    return lax.dot_general(a, b, (((1,), (1,)), ((), ())), preferred_element_type=F32)


def _prenorm_kernel(x_ref, g_ref, o_ref):
    x = x_ref[...]
    ms = jnp.mean(x * x, axis=-1, keepdims=True)
    o_ref[...] = (x * lax.rsqrt(ms + RMS_EPS) * g_ref[...]).astype(o_ref.dtype)


def _prenorm(x2d, g):
    m, d = x2d.shape
    tm = _tile(m, 256, 8)
    return pl.pallas_call(
        _prenorm_kernel,
        out_shape=jax.ShapeDtypeStruct((m, d), BF16),
        grid=(m // tm,),
        in_specs=[pl.BlockSpec((tm, d), lambda i: (i, 0)),
                  pl.BlockSpec((1, d), lambda i: (0, 0))],
        out_specs=pl.BlockSpec((tm, d), lambda i: (i, 0)),
        compiler_params=_params("arbitrary"),
        name="prenorm",
    )(x2d, g)


def _inproj_kernel(u_ref, w_ref, s_ref, z_ref):
    z_ref[...] = (_dot_nt(u_ref[...], w_ref[...]) * s_ref[...]).astype(z_ref.dtype)


def _inproj_kr_kernel(u_ref, w_ref, wkr_ref, z_ref, kr_ref):
    u = u_ref[...]
    z_ref[...] = _dot_nt(u, w_ref[...]).astype(z_ref.dtype)

    @pl.when(pl.program_id(1) == 0)
    def _():
        kr_ref[...] = _dot_nt(u, wkr_ref[...])


def _inproj(u, w_t, colscale, tn, n_tiles, skip_at, skip_tiles):
    m, d = u.shape
    tm = _tile(m, 1024, 16)

    def wrow(i, j):
        return jnp.where(j < skip_at, j, j + skip_tiles), 0

    return pl.pallas_call(
        _inproj_kernel,
        out_shape=jax.ShapeDtypeStruct((m, n_tiles * tn), BF16),
        grid=(m // tm, n_tiles),
        in_specs=[pl.BlockSpec((tm, d), lambda i, j: (i, 0)),
                  pl.BlockSpec((tn, d), wrow),
                  pl.BlockSpec((1, tn), lambda i, j: (0, j))],
        out_specs=pl.BlockSpec((tm, tn), lambda i, j: (i, j)),
        compiler_params=_params("arbitrary", "arbitrary"),
        name="inproj",
    )(u, w_t, colscale)


def _inproj_gate_kr(u, w_t, wkr_t):
    m, d = u.shape
    n = w_t.shape[0]
    tm = _tile(m, 1024, 16)
    tn = _tile(n, 1024)
    return pl.pallas_call(
        _inproj_kr_kernel,
        out_shape=(jax.ShapeDtypeStruct((m, n), BF16),
                   jax.ShapeDtypeStruct((m, LANES), F32)),
        grid=(m // tm, n // tn),
        in_specs=[pl.BlockSpec((tm, d), lambda i, j: (i, 0)),
                  pl.BlockSpec((tn, d), lambda i, j: (j, 0)),
                  pl.BlockSpec((LANES, d), lambda i, j: (0, 0))],
        out_specs=(pl.BlockSpec((tm, tn), lambda i, j: (i, j)),
                   pl.BlockSpec((tm, LANES), lambda i, j: (i, 0))),
        compiler_params=_params("arbitrary", "arbitrary"),
        name="inproj_gate",
    )(u, w_t, wkr_t)


def _vt_kernel(w_ref, u_ref, o_ref):
    o_ref[0, 0] = _dot_nt(w_ref[...], u_ref[...]).astype(o_ref.dtype)


def _v_transposed(w_t, row_block, n, u, batch, t):
    d = w_t.shape[1]
    m = u.shape[0]
    nt = m // (batch * t)
    return pl.pallas_call(
        _vt_kernel,
        out_shape=jax.ShapeDtypeStruct((batch, nt, n, t), BF16),
        grid=(m // t,),
        in_specs=[pl.BlockSpec((n, d), lambda i: (row_block, 0)),
                  pl.BlockSpec((t, d), lambda i: (i, 0))],
        out_specs=pl.BlockSpec((1, 1, n, t), lambda i: (i // nt, i % nt, 0, 0)),
        compiler_params=_params("arbitrary"),
        name="inproj_vt",
    )(w_t, u)


def _rope_combine(t):
    r = t + pltpu.roll(t, MLA_ROPE, axis=1)
    lane = lax.broadcasted_iota(jnp.int32, r.shape, 1)
    return jnp.where(lane < MLA_ROPE, r, 0.0), lane


def _mla_q_kernel(cq_ref, g_ref, w_ref, tab_ref, lanes_ref, o_ref, cqn_ref, *, heads_per_step, scale):
    @pl.when(pl.program_id(1) == 0)
    def _():
        c = cq_ref[...].astype(F32)
        ms = jnp.mean(c * c, axis=-1, keepdims=True)
        cqn_ref[...] = (c * lax.rsqrt(ms + RMS_EPS) * g_ref[...]).astype(cqn_ref.dtype)

    y = _dot(cqn_ref[...], w_ref[...])
    tab = tab_ref[...]
    mask_lanes = lanes_ref[...].astype(F32)
    for hh in range(heads_per_step):
        base = hh * HEAD_TILE
        nope = y[:, base:base + MLA_NOPE] * scale
        r, _ = _rope_combine(y[:, base + MLA_NOPE:base + HEAD_TILE] * tab)
        o_ref[:, base:base + MLA_NOPE] = nope.astype(o_ref.dtype)
        o_ref[:, base + MLA_NOPE:base + HEAD_TILE] = (r * scale + mask_lanes).astype(o_ref.dtype)


def _mla_q(z, g, w, tab, lanes_tab, n_heads, q_lora, cq_block, seq):
    m = z.shape[0]
    tm = _tile(m, 1024, 16)
    tm = math.gcd(tm, seq)
    hp = _tile(n_heads, 4, 1)
    nseq = seq // tm
    kern = functools.partial(_mla_q_kernel, heads_per_step=hp,
                             scale=float((MLA_NOPE + MLA_ROPE) ** -0.5 * LOG2E))
    return pl.pallas_call(
        kern,
        out_shape=jax.ShapeDtypeStruct((m, n_heads * HEAD_TILE), BF16),
        grid=(m // tm, n_heads // hp),
        in_specs=[pl.BlockSpec((tm, q_lora), lambda i, j: (i, cq_block)),
                  pl.BlockSpec((1, q_lora), lambda i, j: (0, 0)),
                  pl.BlockSpec((q_lora, hp * HEAD_TILE), lambda i, j: (0, j)),
                  pl.BlockSpec((tm, LANES), lambda i, j: (i % nseq, 0)),
                  pl.BlockSpec((tm, LANES), lambda i, j: (i % nseq, 0))],
        out_specs=pl.BlockSpec((tm, hp * HEAD_TILE), lambda i, j: (i, j)),
        scratch_shapes=[pltpu.VMEM((tm, q_lora), BF16)],
        compiler_params=_params("arbitrary", "arbitrary"),
        name="mla_q",
    )(z, g, w, tab, lanes_tab)


def _mla_kv_kernel(ckv_ref, kr_ref, g_ref, wk_ref, wvt_ref, tab_ref, lanes_ref, k_ref, vt_ref,
                   *, n_heads, t):
    c = ckv_ref[...].astype(F32)
    ms = jnp.mean(c * c, axis=-1, keepdims=True)
    cn = (c * lax.rsqrt(ms + RMS_EPS) * g_ref[...]).astype(BF16)
    kn = _dot(cn, wk_ref[...])
    kr, _ = _rope_combine(kr_ref[...] * tab_ref[...])
    kr = (kr + lanes_ref[...].astype(F32)).astype(k_ref.dtype)
    for h in range(n_heads):
        k_ref[:, h * HEAD_TILE:h * HEAD_TILE + MLA_NOPE] = (
            kn[:, h * MLA_NOPE:(h + 1) * MLA_NOPE].astype(k_ref.dtype))
        k_ref[:, h * HEAD_TILE + MLA_NOPE:(h + 1) * HEAD_TILE] = kr
    vt = _dot_nt(wvt_ref[...], cn).astype(vt_ref.dtype)
    for tt in range(vt_ref.shape[1]):
        vt_ref[0, tt] = vt[:, tt * t:(tt + 1) * t]


def _mla_kv(z, kr, g, wk, wvt, tab, lanes_tab, n_heads, kv_lora, ckv_block, batch, seq, t):
    m = z.shape[0]
    tm = math.gcd(_tile(m, 1024, 16), seq)
    t = min(t, tm)
    nseq = seq // tm
    kern = functools.partial(_mla_kv_kernel, n_heads=n_heads, t=t)
    return pl.pallas_call(
        kern,
        out_shape=(jax.ShapeDtypeStruct((m, n_heads * HEAD_TILE), BF16),
                   jax.ShapeDtypeStruct((batch, seq // t, n_heads * MLA_V_DIM, t), BF16)),
        grid=(m // tm,),
        in_specs=[pl.BlockSpec((tm, kv_lora), lambda i: (i, ckv_block)),
                  pl.BlockSpec((tm, LANES), lambda i: (i, 0)),
                  pl.BlockSpec((1, kv_lora), lambda i: (0, 0)),
                  pl.BlockSpec((kv_lora, n_heads * MLA_NOPE), lambda i: (0, 0)),
                  pl.BlockSpec((n_heads * MLA_V_DIM, kv_lora), lambda i: (0, 0)),
                  pl.BlockSpec((tm, LANES), lambda i: (i % nseq, 0)),
                  pl.BlockSpec((tm, LANES), lambda i: (i % nseq, 0))],
        out_specs=(pl.BlockSpec((tm, n_heads * HEAD_TILE), lambda i: (i, 0)),
                   pl.BlockSpec((1, tm // t, n_heads * MLA_V_DIM, t),
                                lambda i: (i // nseq, i % nseq, 0, 0))),
        compiler_params=_params("arbitrary"),
        name="mla_kv",
    )(z, kr, g, wk, wvt, tab, lanes_tab)


def _col_max(s_t):
    return jnp.max(s_t, axis=0, keepdims=True)


def _score_stage(s_ref, mx_ref, mm, cs, s_t):
    s_ref[mm, :, cs] = s_t
    mx_ref[mm, :, cs] = _col_max(s_t)


def _exp_stage(s_ref, mx_ref, p_ref, mm, cs, m_prev):
    m_new = jnp.maximum(m_prev, mx_ref[mm, :, cs])
    p_ref[mm, :, cs] = jnp.exp2(s_ref[mm, :, cs] - m_new).astype(BF16)
    return m_new, jnp.exp2(m_prev - m_new)


def _acc_stage(p_ref, acc_ref, mm, cs, vt_aug, alpha):
    acc_ref[mm, :, cs] = alpha * acc_ref[mm, :, cs] + _dot(vt_aug, p_ref[mm, :, cs])


def _silu(g):
    return g * jax.nn.sigmoid(g)


def _build_values(vaug_ref, vtm_ref, vt, vt_meta, dv, n_meta):
    vaug_ref[:, 0:dv, :] = vt
    vaug_ref[:, dv:, :] = jnp.ones((vaug_ref.shape[0], ONES_ROWS, vaug_ref.shape[2]), BF16)
    vtm_ref[0:dv, :] = jnp.zeros((dv, vtm_ref.shape[1]), BF16)
    vtm_ref[0:dv, 0:n_meta] = vt_meta
    vtm_ref[dv:, :] = jnp.ones((ONES_ROWS, vtm_ref.shape[1]), BF16)


def _attention_sweep(n_maps, qk, v_blk, meta_scores, vtm, diag_fix, s_refs, mx_refs, p_refs,
                     acc_ref, n_full):
    maps = range(n_maps)
    tq = acc_ref.shape[2]
    cols = [slice(c, c + COL_TILE) for c in range(0, tq, COL_TILE)]

    def score(slot, blk, c, fix=None):
        for mm in maps:
            s_t = qk(mm, blk, cols[c])
            if fix is not None:
                s_t = diag_fix(fix, c, s_t)
            _score_stage(s_refs[slot], mx_refs[slot], mm, cols[c], s_t)

    def expo(slot, c, m, alpha):
        for mm in maps:
            m[mm][c], alpha[mm][c] = _exp_stage(s_refs[slot], mx_refs[slot], p_refs[slot], mm,
                                                cols[c], m[mm][c])

    def accum(slot, vb, c, alpha):
        for mm in maps:
            _acc_stage(p_refs[slot], acc_ref, mm, cols[c], vb[mm], alpha[mm][c])

    def blank():
        return [[None] * len(cols) for _ in maps]

    all_cols = range(len(cols))
    d1_cols = range(s_refs[1].shape[1] // COL_TILE, len(cols))

    m, a0, a1 = blank(), blank(), blank()
    for c in all_cols:
        s_meta = [meta_scores(mm, cols[c]) for mm in maps]
        score(0, n_full, c, fix=0)
        if c in d1_cols:
            score(1, n_full + 1, c, fix=1)
        for mm in maps:
            mx = _col_max(s_meta[mm])
            acc_ref[mm, :, cols[c]] = _dot(vtm[mm], jnp.exp2(s_meta[mm] - mx).astype(BF16))
            m[mm][c] = mx
    for c in all_cols:
        expo(0, c, m, a0)

    def pair(m, a0, vb0, vb1, slot1_cols, next0, next1):
        a1 = blank()
        for c in all_cols:
            accum(0, vb0, c, a0)
            if c in slot1_cols:
                expo(1, c, m, a1)
            if next0 is not None:
                score(0, next0, c)
        for c in all_cols:
            if c in slot1_cols:
                accum(1, vb1, c, a1)
            if next0 is not None:
                expo(0, c, m, a0)
            if next1 is not None:
                score(1, next1, c)
        return m, a0

    if n_full == 0:
        pair(m, a0, v_blk(0), v_blk(1), d1_cols, None, None)
        return

    m, a0 = pair(m, a0, v_blk(n_full), v_blk(n_full + 1), d1_cols, 0, 1)

    def body(i, carry):
        m, a0 = [list(x) for x in carry[0]], [list(x) for x in carry[1]]
        return pair(m, a0, v_blk(2 * i - 2), v_blk(2 * i - 1), all_cols, 2 * i, 2 * i + 1)

    m, a0 = lax.fori_loop(1, n_full // 2, body, (m, a0))
    m, a0 = [list(x) for x in m], [list(x) for x in a0]
    pair(m, a0, v_blk(n_full - 2), v_blk(n_full - 1), all_cols, None, None)


def _diff_attn_kernel(q_ref, k_ref, vt_ref, g_ref, kmeta_ref, vtmeta_ref, eq_ref, ek_ref,
                      ekmeta_ref, core_ref, lq1_ref, lk1_ref, lq2_ref, lk2_ref, subln_ref,
                      o_ref, kaug_ref, kmaug_ref, vaug_ref, vtm_ref, s0_ref, s1_ref, mx0_ref, mx1_ref,
                      p0_ref, p1_ref, acc_ref, *, lambda_init, t, n_meta):
    d = DIFF_HEAD_DIM
    dv = DIFF_V_DIM
    tq = acc_ref.shape[2]

    kmaug_ref[...] = jnp.zeros(kmaug_ref.shape, BF16)
    _build_values(vaug_ref, vtm_ref, vt_ref[0], vtmeta_ref[...], dv, n_meta)
    for mm in range(2):
        kaug_ref[mm, :, 0:d] = k_ref[0, :, mm * d:(mm + 1) * d]
        kaug_ref[mm, :, d:2 * d] = ek_ref[0]
        kmaug_ref[mm, 0:n_meta, 0:d] = kmeta_ref[:, mm * d:(mm + 1) * d]
        kmaug_ref[mm, :, d:2 * d] = ekmeta_ref[0]

    def diag_fix(kd, c, s_t):
        lo = c * COL_TILE - kd * t
        if 0 <= lo < t:
            return s_t + core_ref[0, :, lo:lo + COL_TILE]
        return s_t

    def v_blk(blk):
        vb = vaug_ref[blk]
        return [vb, vb]

    lam = (jnp.exp(jnp.sum(lq1_ref[...] * lk1_ref[...], axis=-1, keepdims=True))
           - jnp.exp(jnp.sum(lq2_ref[...] * lk2_ref[...], axis=-1, keepdims=True))
           + lambda_init)

    for qi in range(q_ref.shape[1] // tq):
        rows = slice(qi * tq, (qi + 1) * tq)
        q = q_ref[0, rows, :]
        eq = eq_ref[0, rows, :]
        qa = [jnp.concatenate([q[:, mm * d:(mm + 1) * d], eq], axis=1) for mm in range(2)]

        def qk(mm, blk, cs, qa=qa):
            off = pl.multiple_of(blk * t, t)
            return _dot_nt(kaug_ref[mm, pl.ds(off, t), :], qa[mm][cs])

        _attention_sweep(2, qk, v_blk, lambda mm, cs, qa=qa: _dot_nt(kmaug_ref[mm], qa[mm][cs]),
                         [vtm_ref[...]] * 2, diag_fix,
                         (s0_ref, s1_ref), (mx0_ref, mx1_ref), (p0_ref, p1_ref), acc_ref,
                         qi * DIAG_BLOCKS)

        o_t = (acc_ref[0, 0:dv] / acc_ref[0, dv:dv + 1]
               - lam * (acc_ref[1, 0:dv] / acc_ref[1, dv:dv + 1]))
        ms = jnp.mean(o_t * o_t, axis=0, keepdims=True)
        o = (o_t * lax.rsqrt(ms + RMS_EPS)).T
        o = (o * subln_ref[...]) * (1.0 - lambda_init)
        o_ref[0, rows, :] = (o * _silu(g_ref[0, rows, :].astype(F32))).astype(o_ref.dtype)


def _diff_attn(z3, vt, z_meta, vt_meta, eq_tab, ek_tab, ekmeta_tab, core_tab, lams, subln,
               n_heads, blk, lambda_init, t, tq):
    b, s, _ = z3.shape
    n_meta = z_meta.shape[0]
    q0, k0, g0 = blk
    vec = pl.BlockSpec((1, DIFF_HEAD_DIM), lambda bb, h: (0, 0))
    kern = functools.partial(_diff_attn_kernel, lambda_init=lambda_init, t=t, n_meta=n_meta)
    return pl.pallas_call(
        kern,
        out_shape=jax.ShapeDtypeStruct((b, s, n_heads * DIFF_V_DIM), BF16),
        grid=(b, n_heads),
        in_specs=[
            pl.BlockSpec((1, s, HEAD_TILE), lambda bb, h: (bb, 0, q0 + h)),
            pl.BlockSpec((1, s, HEAD_TILE), lambda bb, h: (bb, 0, k0 + h)),
            pl.BlockSpec((1, s // t, DIFF_V_DIM, t), lambda bb, h: (bb, 0, h, 0)),
            pl.BlockSpec((1, s, DIFF_V_DIM), lambda bb, h: (bb, 0, g0 + h)),
            pl.BlockSpec((n_meta, HEAD_TILE), lambda bb, h: (0, k0 + h)),
            pl.BlockSpec((DIFF_V_DIM, n_meta), lambda bb, h: (h, 0)),
            pl.BlockSpec((1, s, LANES), lambda bb, h: (h, 0, 0)),
            pl.BlockSpec((1, s, LANES), lambda bb, h: (h, 0, 0)),
            pl.BlockSpec((1, LANES, LANES), lambda bb, h: (h, 0, 0)),
            pl.BlockSpec((1, t, t), lambda bb, h: (h, 0, 0)),
            vec, vec, vec, vec,
            pl.BlockSpec((1, DIFF_V_DIM), lambda bb, h: (0, 0)),
        ],
        out_specs=pl.BlockSpec((1, s, DIFF_V_DIM), lambda bb, h: (bb, 0, h)),
        scratch_shapes=[
            pltpu.VMEM((2, s, HEAD_TILE), BF16),
            pltpu.VMEM((2, LANES, HEAD_TILE), BF16),
            pltpu.VMEM((s // t, DIFF_V_DIM + ONES_ROWS, t), BF16),
            pltpu.VMEM((DIFF_V_DIM + ONES_ROWS, LANES), BF16),
            pltpu.VMEM((2, t, tq), F32),
            pltpu.VMEM((2, t, tq), F32),
            pltpu.VMEM((2, 1, tq), F32),
            pltpu.VMEM((2, 1, tq), F32),
            pltpu.VMEM((2, t, tq), BF16),
            pltpu.VMEM((2, t, tq), BF16),
            pltpu.VMEM((2, DIFF_V_DIM + ONES_ROWS, tq), F32),
        ],
        compiler_params=_params("arbitrary", "arbitrary"),
        name="diff_attn",
    )(z3, z3, vt, z3, z_meta, vt_meta, eq_tab, ek_tab, ekmeta_tab, core_tab, *lams, subln)


def _mla_attn_kernel(q_ref, k_ref, vt_ref, g_ref, kmeta_ref, vtmeta_ref, padk_ref,
                     o_ref, kmaug_ref, vaug_ref, vtm_ref, s0_ref, s1_ref, mx0_ref, mx1_ref,
                     p0_ref, p1_ref, acc_ref, *, t, n_meta, hp):
    dv = MLA_V_DIM
    heads = range(hp)
    tq = acc_ref.shape[2]

    for hh in heads:
        kmaug_ref[hh] = padk_ref[...]
        kmaug_ref[hh, 0:n_meta, :] = kmeta_ref[:, hh * HEAD_TILE:(hh + 1) * HEAD_TILE]
        _build_values(vaug_ref.at[hh], vtm_ref.at[hh], vt_ref[0, :, hh * dv:(hh + 1) * dv, :],
                      vtmeta_ref[hh * dv:(hh + 1) * dv, :], dv, n_meta)

    for qi in range(q_ref.shape[1] // tq):
        rows = slice(qi * tq, (qi + 1) * tq)
        q = [q_ref[0, rows, hh * HEAD_TILE:(hh + 1) * HEAD_TILE] for hh in heads]

        def qk(hh, blk, cs, q=q):
            off = pl.multiple_of(blk * t, t)
            return _dot_nt(k_ref[0, pl.ds(off, t), hh * HEAD_TILE:(hh + 1) * HEAD_TILE], q[hh][cs])

        _attention_sweep(hp, qk, lambda blk: [vaug_ref[hh, blk] for hh in heads],
                         lambda hh, cs, q=q: _dot_nt(kmaug_ref[hh], q[hh][cs]),
                         [vtm_ref[hh] for hh in heads], lambda kd, c, s_t: s_t,
                         (s0_ref, s1_ref), (mx0_ref, mx1_ref), (p0_ref, p1_ref), acc_ref,
                         qi * DIAG_BLOCKS)

        for hh in heads:
            o = (acc_ref[hh, 0:dv] / acc_ref[hh, dv:dv + 1]).T
            g = g_ref[0, rows, hh * dv:(hh + 1) * dv].astype(F32)
            o_ref[0, rows, hh * dv:(hh + 1) * dv] = (o * _silu(g)).astype(o_ref.dtype)


def _mla_attn(q3, k3, vt, z3, k_meta, vt_meta, padk, n_heads, g0, t, tq):
    b, s, _ = q3.shape
    n_meta = k_meta.shape[0]
    hp = _tile(n_heads, MLA_HEADS_PER_STEP, 1)
    kern = functools.partial(_mla_attn_kernel, t=t, n_meta=n_meta, hp=hp)
    return pl.pallas_call(
        kern,
        out_shape=jax.ShapeDtypeStruct((b, s, n_heads * MLA_V_DIM), BF16),
        grid=(b, n_heads // hp),
        in_specs=[
            pl.BlockSpec((1, s, hp * HEAD_TILE), lambda bb, h: (bb, 0, h)),
            pl.BlockSpec((1, s, hp * HEAD_TILE), lambda bb, h: (bb, 0, h)),
            pl.BlockSpec((1, s // t, hp * MLA_V_DIM, t), lambda bb, h: (bb, 0, h, 0)),
            pl.BlockSpec((1, s, hp * MLA_V_DIM), lambda bb, h: (bb, 0, g0 + h)),
            pl.BlockSpec((n_meta, hp * HEAD_TILE), lambda bb, h: (0, h)),
            pl.BlockSpec((hp * MLA_V_DIM, n_meta), lambda bb, h: (h, 0)),
            pl.BlockSpec((LANES, HEAD_TILE), lambda bb, h: (0, 0)),
        ],
        out_specs=pl.BlockSpec((1, s, hp * MLA_V_DIM), lambda bb, h: (bb, 0, h)),
        scratch_shapes=[
            pltpu.VMEM((hp, LANES, HEAD_TILE), BF16),
            pltpu.VMEM((hp, s // t, MLA_V_DIM + ONES_ROWS, t), BF16),
            pltpu.VMEM((hp, MLA_V_DIM + ONES_ROWS, LANES), BF16),
            pltpu.VMEM((hp, t, tq), F32),
            pltpu.VMEM((hp, t, tq), F32),
            pltpu.VMEM((hp, 1, tq), F32),
            pltpu.VMEM((hp, 1, tq), F32),
            pltpu.VMEM((hp, t, tq), BF16),
            pltpu.VMEM((hp, t, tq), BF16),
            pltpu.VMEM((hp, MLA_V_DIM + ONES_ROWS, tq), F32),
        ],
        compiler_params=_params("arbitrary", "arbitrary"),
        name="mla_attn",
    )(q3, k3, vt, z3, k_meta, vt_meta, padk)


def _outproj_kernel(a_ref, b_ref, wa_ref, wb_ref, y_ref):
    y = _dot(a_ref[...], wa_ref[...]) + _dot(b_ref[...], wb_ref[...])
    y_ref[...] = y.astype(y_ref.dtype)


def _outproj(mix_a, mix_b, w):
    m, ka = mix_a.shape
    n = w.shape[1]
    assert mix_b.shape[1] == ka and w.shape[0] == 2 * ka
    tm = _tile(m, 1024, 16)
    tn = _tile(n, 1024)
    return pl.pallas_call(
        _outproj_kernel,
        out_shape=jax.ShapeDtypeStruct((m, n), BF16),
        grid=(m // tm, n // tn),
        in_specs=[pl.BlockSpec((tm, ka), lambda i, j: (i, 0)),
                  pl.BlockSpec((tm, ka), lambda i, j: (i, 0)),
                  pl.BlockSpec((ka, tn), lambda i, j: (0, j)),
                  pl.BlockSpec((ka, tn), lambda i, j: (1, j))],
        out_specs=pl.BlockSpec((tm, tn), lambda i, j: (i, j)),
        compiler_params=_params("arbitrary", "arbitrary"),
        name="outproj",
    )(mix_a, mix_b, w, w)


def _postnorm_kernel(x_ref, y_ref, g_ref, o_ref):
    y = y_ref[...].astype(F32)
    ms = jnp.mean(y * y, axis=-1, keepdims=True)
    o_ref[...] = x_ref[...] + y * lax.rsqrt(ms + RMS_EPS) * g_ref[...]


def _postnorm(x2d, y, g):
    m, d = x2d.shape
    tm = _tile(m, 256, 16)
    return pl.pallas_call(
        _postnorm_kernel,
        out_shape=jax.ShapeDtypeStruct((m, d), F32),
        grid=(m // tm,),
        in_specs=[pl.BlockSpec((tm, d), lambda i: (i, 0)),
                  pl.BlockSpec((tm, d), lambda i: (i, 0)),
                  pl.BlockSpec((1, d), lambda i: (0, 0))],
        out_specs=pl.BlockSpec((tm, d), lambda i: (i, 0)),
        compiler_params=_params("arbitrary"),
        name="postnorm",
    )(x2d, y, g)


def _split3(v):
    v = np.asarray(v, np.float32)
    hi = v.astype(NP_BF16)
    r = v - hi.astype(np.float32)
    mid = r.astype(NP_BF16)
    lo = (r - mid.astype(np.float32)).astype(NP_BF16)
    return hi, mid, lo


def _rope_table(pos):
    inv_freq = 1.0 / (ROPE_THETA ** (jnp.arange(0, MLA_ROPE, 2, dtype=F32) / MLA_ROPE))
    ang = pos.astype(F32)[:, None] * inv_freq[None, :]
    cos, sin = jnp.cos(ang), jnp.sin(ang)
    return jnp.concatenate([cos, cos, sin, sin], axis=1)


def _chunk_mask_lanes(seq):
    chunk = np.arange(seq) // CHUNK
    c = np.arange(seq // CHUNK - 1)
    q_lanes = (chunk[:, None] == c[None, :]).astype(np.float32)
    k_lanes = np.where(chunk[:, None] > c[None, :], MASK_VALUE, 0.0).astype(np.float32)
    return q_lanes.astype(NP_BF16), k_lanes.astype(NP_BF16)


def _diff_tables(n_heads, n_meta, seq, t):
    f32 = np.float32
    slopes = (2.0 ** (-8.0 * np.arange(1, n_heads + 1, dtype=f32) / n_heads)).astype(f32)
    slopes = (slopes * f32(LOG2E)).astype(f32)
    pos_q = ((np.arange(seq, dtype=f32) + n_meta)[None, :] * slopes[:, None]).astype(f32)
    pos_m = (np.arange(LANES, dtype=f32)[None, :] * slopes[:, None]).astype(f32)
    qc, kc = _chunk_mask_lanes(seq)
    n_c = qc.shape[1]
    assert 8 + n_c <= LANES
    ones_q = np.ones((n_heads, seq), NP_BF16)
    zeros_q = np.zeros((n_heads, seq), NP_BF16)
    qh, qm, ql = _split3(-pos_q)
    kh, km, kl = _split3(pos_q)
    tail = np.zeros((n_heads, seq, LANES - 8 - n_c), NP_BF16)

    def lanes(cols, chunk_lanes):
        return np.concatenate([np.stack(cols + [zeros_q], -1),
                               np.broadcast_to(chunk_lanes, (n_heads,) + chunk_lanes.shape), tail], -1)

    eq = lanes([qh, qm, ql, ones_q, ones_q, ones_q, ones_q], qc)
    ek = lanes([ones_q, ones_q, ones_q, zeros_q, kh, km, kl], kc)
    mh, mmid, ml = _split3(pos_m)
    valid = (np.arange(LANES) < n_meta)[None, :]
    padmask = np.broadcast_to(np.where(valid, 0.0, MASK_VALUE).astype(f32),
                              (n_heads, LANES)).astype(NP_BF16)
    ones_m = np.ones((n_heads, LANES), NP_BF16)
    ekm = np.concatenate(
        [np.stack([ones_m, ones_m, ones_m, padmask, mh, mmid, ml], -1),
         np.zeros((n_heads, LANES, LANES - 7), NP_BF16)], -1)
    j = np.arange(t)[:, None]
    i = np.arange(t)[None, :]
    same = (j // CHUNK) == (i // CHUNK)
    fut = np.where(same, 2.0 * np.maximum(j - i, 0), 0.0).astype(f32)
    core = (-slopes[:, None, None] * fut[None]).astype(f32)
    return tuple(jnp.asarray(a) for a in (eq, ek, ekm, core))


def _mla_mask_lanes(n_meta, seq):
    qc, kc = _chunk_mask_lanes(seq)
    n_c = qc.shape[1]
    assert MLA_ROPE + 1 + n_c <= LANES
    q = np.zeros((seq, LANES), NP_BF16)
    k = np.zeros((seq, LANES), NP_BF16)
    q[:, MLA_ROPE] = 1.0
    q[:, MLA_ROPE + 1:MLA_ROPE + 1 + n_c] = qc
    k[:, MLA_ROPE + 1:MLA_ROPE + 1 + n_c] = kc
    return jnp.asarray(q), jnp.asarray(k), jnp.zeros((n_meta, LANES), BF16)


def kernel(x, meta_tokens, norm_pre, w_in, diff_lambda_q1, diff_lambda_k1, diff_lambda_q2,
           diff_lambda_k2, diff_subln, mla_norm_q, mla_norm_kv, w_uq, w_ukv, w_out, norm_post):
    assert norm_pre.shape[0] == 1, "single-layer block"
    b, s, d = x.shape
    n_meta = meta_tokens.shape[0]
    mix = w_out.shape[1]
    diff_w = mix // 2
    mla_w = mix - diff_w
    ha = diff_w // DIFF_V_DIM
    hb = mla_w // MLA_V_DIM
    q_lora = mla_norm_q.shape[-1]
    kv_lora = mla_norm_kv.shape[-1]
    t = min(KEY_TILE, s)
    tq = min(QUERY_TILE, s)
    assert s % tq == 0 and tq == DIAG_BLOCKS * t and t % CHUNK == 0 and n_meta <= 16
    lambda_init = 0.8 - 0.6 * math.exp(-0.3 * 0)

    w_all = jnp.transpose(w_in[0]).astype(BF16)
    lat = q_lora + kv_lora
    tn = _tile(math.gcd(diff_w, lat), 1024)
    n_main = 3 * diff_w + lat
    assert (3 * diff_w) % q_lora == 0 and (3 * diff_w + q_lora) % kv_lora == 0
    q_blk0 = 0
    k_blk0 = diff_w // HEAD_TILE
    g_blk0 = 2 * diff_w // HEAD_TILE
    cq_block = 3 * diff_w // q_lora
    ckv_block = (3 * diff_w + q_lora) // kv_lora
    colscale = jnp.ones((1, n_main), F32).at[:, :diff_w].set(DIFF_HEAD_DIM ** -0.5 * LOG2E)
    half = MLA_ROPE // 2
    kr0 = 4 * diff_w + lat
    w_krt = w_all[kr0:kr0 + MLA_ROPE]
    w_kr2 = jnp.concatenate([w_krt, -w_krt[half:], w_krt[:half]], axis=0)
    w_mg16 = w_all[kr0 + MLA_ROPE:kr0 + MLA_ROPE + mla_w]

    wq = w_uq[0].reshape(q_lora, hb, MLA_NOPE + MLA_ROPE)
    wq_r = wq[:, :, MLA_NOPE:]
    wq_full = jnp.concatenate(
        [wq[:, :, :MLA_NOPE], wq_r, jnp.concatenate([-wq_r[:, :, half:], wq_r[:, :, :half]], -1)],
        axis=-1).reshape(q_lora, hb * HEAD_TILE).astype(BF16)
    wkv = w_ukv[0].reshape(kv_lora, hb, MLA_NOPE + MLA_V_DIM)
    w_kn = wkv[:, :, :MLA_NOPE].reshape(kv_lora, hb * MLA_NOPE).astype(BF16)
    w_vt = wkv[:, :, MLA_NOPE:].reshape(kv_lora, hb * MLA_V_DIM).T.astype(BF16)
    wo = w_out[0].astype(BF16)

    pos = jnp.arange(n_meta + s, dtype=jnp.int32)
    rope_tab = _rope_table(pos)
    tab_meta, tab_seq = rope_tab[:n_meta], rope_tab[n_meta:]
    eq_tab, ek_tab, ekm_tab, core_tab = _diff_tables(ha, n_meta, s, t)
    qlane_tab, klane_tab, klane_meta = _mla_mask_lanes(n_meta, s)
    lane = jnp.arange(HEAD_TILE)
    row = jnp.arange(LANES)
    padk = jnp.where((lane[None, :] == MLA_NOPE + MLA_ROPE) & (row[:, None] >= n_meta),
                     MASK_VALUE, 0.0).astype(BF16)

    x2d = x.reshape(b * s, d)
    u = _prenorm(x2d, norm_pre)
    u_meta = _prenorm(meta_tokens.astype(x.dtype), norm_pre)
    main_tiles = (n_main // tn, 2 * diff_w // tn, diff_w // tn)
    z = _inproj(u, w_all, colscale, tn, *main_tiles)
    z_meta = _inproj(u_meta, w_all, colscale, tn, *main_tiles)
    z_gate, kr = _inproj_gate_kr(u, w_mg16, w_kr2)
    _, kr_meta = _inproj_gate_kr(u_meta, w_mg16, w_kr2)
    vt_a = _v_transposed(w_all, 2, diff_w, u, b, t)
    vt_a_meta = _v_transposed(w_all, 2, diff_w, u_meta, 1, n_meta)[0, 0]

    g_cq = mla_norm_q.astype(F32)
    g_ckv = mla_norm_kv.astype(F32)
    q_b = _mla_q(z, g_cq, wq_full, tab_seq, qlane_tab, hb, q_lora, cq_block, s)
    k_b, vt_b = _mla_kv(z, kr, g_ckv, w_kn, w_vt, tab_seq, klane_tab, hb, kv_lora, ckv_block,
                        b, s, t)
    k_b_meta, vt_b_meta = _mla_kv(z_meta, kr_meta, g_ckv, w_kn, w_vt, tab_meta, klane_meta, hb,
                                  kv_lora, ckv_block, 1, n_meta, n_meta)
    vt_b_meta = vt_b_meta[0, 0]

    z3 = z.reshape(b, s, n_main)
    lams = [v.astype(F32) for v in (diff_lambda_q1, diff_lambda_k1, diff_lambda_q2, diff_lambda_k2)]
    mix_a = _diff_attn(z3, vt_a, z_meta, vt_a_meta, eq_tab, ek_tab, ekm_tab, core_tab, lams,
                       diff_subln.astype(F32), ha, (q_blk0, k_blk0, g_blk0), lambda_init, t, tq)
    mix_b = _mla_attn(q_b.reshape(b, s, hb * HEAD_TILE), k_b.reshape(b, s, hb * HEAD_TILE), vt_b,
                      z_gate.reshape(b, s, mla_w), k_b_meta, vt_b_meta, padk, hb, 0, t, tq)

    y = _outproj(mix_a.reshape(b * s, diff_w), mix_b.reshape(b * s, mla_w), wo)
    out = _postnorm(x2d, y, norm_post.astype(F32))
    return out.reshape(b, s, d)
```

```python
import functools
import math

import jax
import jax.numpy as jnp
import numpy as np
from jax import lax
from jax.experimental import pallas as pl
from jax.experimental.pallas import tpu as pltpu

CHUNK = 64
RMS_EPS = 1e-6
MASK_VALUE = -1e30
DIFF_HEAD_DIM = 128
DIFF_V_DIM = 2 * DIFF_HEAD_DIM
MLA_NOPE = 128
MLA_ROPE = 64
MLA_V_DIM = 128
ROPE_THETA = 10000.0

LANES = 128
HEAD_TILE = 2 * LANES
VMEM_LIMIT = 56 * 1024 * 1024

KEY_TILE = 512
DIAG_BLOCKS = 2
QUERY_TILE = DIAG_BLOCKS * KEY_TILE
COL_TILE = 256
MLA_HEADS_PER_STEP = 2
M_INIT = -1e38
ONES_ROWS = 16
LOG2E = math.log2(math.e)

F32 = jnp.float32
BF16 = jnp.bfloat16
NP_BF16 = np.dtype(jnp.bfloat16)


def _params(*sem):
    return pltpu.CompilerParams(dimension_semantics=sem, vmem_limit_bytes=VMEM_LIMIT)


def _tile(dim, target, mult=LANES):
    if dim <= target:
        return dim
    t = (target // mult) * mult
    while t > mult and dim % t:
        t -= mult
    assert dim % t == 0, (dim, target)
    return t


def _dot(a, b):
    return jnp.dot(a, b, preferred_element_type=F32)


def _dot_nt(a, b):
    return lax.dot_general(a, b, (((1,), (1,)), ((), ())), preferred_element_type=F32)


def _prenorm_kernel(x_ref, g_ref, o_ref):
    x = x_ref[...]
    ms = jnp.mean(x * x, axis=-1, keepdims=True)
    o_ref[...] = (x * lax.rsqrt(ms + RMS_EPS) * g_ref[...]).astype(o_ref.dtype)


def _prenorm(x2d, g):
    m, d = x2d.shape
    tm = _tile(m, 256, 8)
    return pl.pallas_call(
        _prenorm_kernel,
        out_shape=jax.ShapeDtypeStruct((m, d), BF16),
        grid=(m // tm,),
        in_specs=[pl.BlockSpec((tm, d), lambda i: (i, 0)),
                  pl.BlockSpec((1, d), lambda i: (0, 0))],
        out_specs=pl.BlockSpec((tm, d), lambda i: (i, 0)),
        compiler_params=_params("arbitrary"),
        name="prenorm",
    )(x2d, g)


def _inproj_kernel(u_ref, w_ref, s_ref, z_ref):
    z_ref[...] = (_dot_nt(u_ref[...], w_ref[...]) * s_ref[...]).astype(z_ref.dtype)


def _inproj_kr_kernel(u_ref, w_ref, wkr_ref, z_ref, kr_ref):
    u = u_ref[...]
    z_ref[...] = _dot_nt(u, w_ref[...]).astype(z_ref.dtype)

    @pl.when(pl.program_id(1) == 0)
    def _():
        kr_ref[...] = _dot_nt(u, wkr_ref[...])


def _inproj(u, w_t, colscale, tn, n_tiles, skip_at, skip_tiles):
    m, d = u.shape
    tm = _tile(m, 1024, 16)

    def wrow(i, j):
        return jnp.where(j < skip_at, j, j + skip_tiles), 0

    return pl.pallas_call(
        _inproj_kernel,
        out_shape=jax.ShapeDtypeStruct((m, n_tiles * tn), BF16),
        grid=(m // tm, n_tiles),
        in_specs=[pl.BlockSpec((tm, d), lambda i, j: (i, 0)),
                  pl.BlockSpec((tn, d), wrow),
                  pl.BlockSpec((1, tn), lambda i, j: (0, j))],
        out_specs=pl.BlockSpec((tm, tn), lambda i, j: (i, j)),
        compiler_params=_params("arbitrary", "arbitrary"),
        name="inproj",
    )(u, w_t, colscale)


def _inproj_gate_kr(u, w_t, wkr_t):
    m, d = u.shape
    n = w_t.shape[0]
    tm = _tile(m, 1024, 16)
    tn = _tile(n, 1024)
    return pl.pallas_call(
        _inproj_kr_kernel,
        out_shape=(jax.ShapeDtypeStruct((m, n), BF16),
                   jax.ShapeDtypeStruct((m, LANES), F32)),
        grid=(m // tm, n // tn),
        in_specs=[pl.BlockSpec((tm, d), lambda i, j: (i, 0)),
                  pl.BlockSpec((tn, d), lambda i, j: (j, 0)),
                  pl.BlockSpec((LANES, d), lambda i, j: (0, 0))],
        out_specs=(pl.BlockSpec((tm, tn), lambda i, j: (i, j)),
                   pl.BlockSpec((tm, LANES), lambda i, j: (i, 0))),
        compiler_params=_params("arbitrary", "arbitrary"),
        name="inproj_gate",
    )(u, w_t, wkr_t)


def _vt_kernel(w_ref, u_ref, o_ref):
    o_ref[0, 0] = _dot_nt(w_ref[...], u_ref[...]).astype(o_ref.dtype)


def _v_transposed(w_t, row_block, n, u, batch, t):
    d = w_t.shape[1]
    m = u.shape[0]
    nt = m // (batch * t)
    return pl.pallas_call(
        _vt_kernel,
        out_shape=jax.ShapeDtypeStruct((batch, nt, n, t), BF16),
        grid=(m // t,),
        in_specs=[pl.BlockSpec((n, d), lambda i: (row_block, 0)),
                  pl.BlockSpec((t, d), lambda i: (i, 0))],
        out_specs=pl.BlockSpec((1, 1, n, t), lambda i: (i // nt, i % nt, 0, 0)),
        compiler_params=_params("arbitrary"),
        name="inproj_vt",
    )(w_t, u)


def _rope_combine(t):
    r = t + pltpu.roll(t, MLA_ROPE, axis=1)
    lane = lax.broadcasted_iota(jnp.int32, r.shape, 1)
    return jnp.where(lane < MLA_ROPE, r, 0.0), lane


def _mla_q_kernel(cq_ref, g_ref, w_ref, tab_ref, lanes_ref, o_ref, cqn_ref, *, heads_per_step, scale):
    @pl.when(pl.program_id(1) == 0)
    def _():
        c = cq_ref[...].astype(F32)
        ms = jnp.mean(c * c, axis=-1, keepdims=True)
        cqn_ref[...] = (c * lax.rsqrt(ms + RMS_EPS) * g_ref[...]).astype(cqn_ref.dtype)

    y = _dot(cqn_ref[...], w_ref[...])
    tab = tab_ref[...]
    mask_lanes = lanes_ref[...].astype(F32)
    for hh in range(heads_per_step):
        base = hh * HEAD_TILE
        nope = y[:, base:base + MLA_NOPE] * scale
        r, _ = _rope_combine(y[:, base + MLA_NOPE:base + HEAD_TILE] * tab)
        o_ref[:, base:base + MLA_NOPE] = nope.astype(o_ref.dtype)
        o_ref[:, base + MLA_NOPE:base + HEAD_TILE] = (r * scale + mask_lanes).astype(o_ref.dtype)


def _mla_q(z, g, w, tab, lanes_tab, n_heads, q_lora, cq_block, seq):
    m = z.shape[0]
    tm = _tile(m, 1024, 16)
    tm = math.gcd(tm, seq)
    hp = _tile(n_heads, 8, 1)
    nseq = seq // tm
    kern = functools.partial(_mla_q_kernel, heads_per_step=hp,
                             scale=float((MLA_NOPE + MLA_ROPE) ** -0.5 * LOG2E))
    return pl.pallas_call(
        kern,
        out_shape=jax.ShapeDtypeStruct((m, n_heads * HEAD_TILE), BF16),
        grid=(m // tm, n_heads // hp),
        in_specs=[pl.BlockSpec((tm, q_lora), lambda i, j: (i, cq_block)),
                  pl.BlockSpec((1, q_lora), lambda i, j: (0, 0)),
                  pl.BlockSpec((q_lora, hp * HEAD_TILE), lambda i, j: (0, j)),
                  pl.BlockSpec((tm, LANES), lambda i, j: (i % nseq, 0)),
                  pl.BlockSpec((tm, LANES), lambda i, j: (i % nseq, 0))],
        out_specs=pl.BlockSpec((tm, hp * HEAD_TILE), lambda i, j: (i, j)),
        scratch_shapes=[pltpu.VMEM((tm, q_lora), BF16)],
        compiler_params=_params("arbitrary", "arbitrary"),
        name="mla_q",
    )(z, g, w, tab, lanes_tab)


def _mla_kv_kernel(ckv_ref, kr_ref, g_ref, wk_ref, wvt_ref, tab_ref, lanes_ref, k_ref, vt_ref,
                   *, n_heads, t):
    c = ckv_ref[...].astype(F32)
    ms = jnp.mean(c * c, axis=-1, keepdims=True)
    cn = (c * lax.rsqrt(ms + RMS_EPS) * g_ref[...]).astype(BF16)
    kn = _dot(cn, wk_ref[...])
    kr, _ = _rope_combine(kr_ref[...] * tab_ref[...])
    kr = (kr + lanes_ref[...].astype(F32)).astype(k_ref.dtype)
    for h in range(n_heads):
        k_ref[:, h * HEAD_TILE:h * HEAD_TILE + MLA_NOPE] = (
            kn[:, h * MLA_NOPE:(h + 1) * MLA_NOPE].astype(k_ref.dtype))
        k_ref[:, h * HEAD_TILE + MLA_NOPE:(h + 1) * HEAD_TILE] = kr
    vt = _dot_nt(wvt_ref[...], cn).astype(vt_ref.dtype)
    for tt in range(vt_ref.shape[1]):
        vt_ref[0, tt] = vt[:, tt * t:(tt + 1) * t]


def _mla_kv(z, kr, g, wk, wvt, tab, lanes_tab, n_heads, kv_lora, ckv_block, batch, seq, t):
    m = z.shape[0]
    tm = math.gcd(_tile(m, 1024, 16), seq)
    t = min(t, tm)
    nseq = seq // tm
    kern = functools.partial(_mla_kv_kernel, n_heads=n_heads, t=t)
    return pl.pallas_call(
        kern,
        out_shape=(jax.ShapeDtypeStruct((m, n_heads * HEAD_TILE), BF16),
                   jax.ShapeDtypeStruct((batch, seq // t, n_heads * MLA_V_DIM, t), BF16)),
        grid=(m // tm,),
        in_specs=[pl.BlockSpec((tm, kv_lora), lambda i: (i, ckv_block)),
                  pl.BlockSpec((tm, LANES), lambda i: (i, 0)),
                  pl.BlockSpec((1, kv_lora), lambda i: (0, 0)),
                  pl.BlockSpec((kv_lora, n_heads * MLA_NOPE), lambda i: (0, 0)),
                  pl.BlockSpec((n_heads * MLA_V_DIM, kv_lora), lambda i: (0, 0)),
                  pl.BlockSpec((tm, LANES), lambda i: (i % nseq, 0)),
                  pl.BlockSpec((tm, LANES), lambda i: (i % nseq, 0))],
        out_specs=(pl.BlockSpec((tm, n_heads * HEAD_TILE), lambda i: (i, 0)),
                   pl.BlockSpec((1, tm // t, n_heads * MLA_V_DIM, t),
                                lambda i: (i // nseq, i % nseq, 0, 0))),
        compiler_params=_params("arbitrary"),
        name="mla_kv",
    )(z, kr, g, wk, wvt, tab, lanes_tab)


def _col_max(s_t):
    return jnp.max(s_t, axis=0, keepdims=True)


def _score_stage(s_ref, mx_ref, mm, cs, s_t):
    s_ref[mm, :, cs] = s_t
    mx_ref[mm, :, cs] = _col_max(s_t)


def _exp_stage(s_ref, mx_ref, p_ref, mm, cs, m_prev):
    m_new = jnp.maximum(m_prev, mx_ref[mm, :, cs])
    p_ref[mm, :, cs] = jnp.exp2(s_ref[mm, :, cs] - m_new).astype(BF16)
    return m_new, jnp.exp2(m_prev - m_new)


def _acc_stage(p_ref, acc_ref, mm, cs, vt_aug, alpha):
    acc_ref[mm, :, cs] = alpha * acc_ref[mm, :, cs] + _dot(vt_aug, p_ref[mm, :, cs])


def _silu(g):
    return g * jax.nn.sigmoid(g)


def _build_values(vaug_ref, vtm_ref, vt, vt_meta, dv, n_meta):
    vaug_ref[:, 0:dv, :] = vt
    vaug_ref[:, dv:, :] = jnp.ones((vaug_ref.shape[0], ONES_ROWS, vaug_ref.shape[2]), BF16)
    vtm_ref[0:dv, :] = jnp.zeros((dv, vtm_ref.shape[1]), BF16)
    vtm_ref[0:dv, 0:n_meta] = vt_meta
    vtm_ref[dv:, :] = jnp.ones((ONES_ROWS, vtm_ref.shape[1]), BF16)


def _attention_sweep(n_maps, qk, v_blk, meta_scores, vtm, diag_fix, s_refs, mx_refs, p_refs,
                     acc_ref, n_full):
    maps = range(n_maps)
    tq = acc_ref.shape[2]
    cols = [slice(c, c + COL_TILE) for c in range(0, tq, COL_TILE)]

    def score(slot, blk, c, fix=None):
        for mm in maps:
            s_t = qk(mm, blk, cols[c])
            if fix is not None:
                s_t = diag_fix(fix, c, s_t)
            _score_stage(s_refs[slot], mx_refs[slot], mm, cols[c], s_t)

    def expo(slot, c, m, alpha):
        for mm in maps:
            m[mm][c], alpha[mm][c] = _exp_stage(s_refs[slot], mx_refs[slot], p_refs[slot], mm,
                                                cols[c], m[mm][c])

    def accum(slot, vb, c, alpha):
        for mm in maps:
            _acc_stage(p_refs[slot], acc_ref, mm, cols[c], vb[mm], alpha[mm][c])

    def blank():
        return [[None] * len(cols) for _ in maps]

    all_cols = range(len(cols))
    d1_cols = range(s_refs[1].shape[1] // COL_TILE, len(cols))

    m, a0, a1 = blank(), blank(), blank()
    for c in all_cols:
        s_meta = [meta_scores(mm, cols[c]) for mm in maps]
        score(0, n_full, c, fix=0)
        if c in d1_cols:
            score(1, n_full + 1, c, fix=1)
        for mm in maps:
            mx = _col_max(s_meta[mm])
            acc_ref[mm, :, cols[c]] = _dot(vtm[mm], jnp.exp2(s_meta[mm] - mx).astype(BF16))
            m[mm][c] = mx
    for c in all_cols:
        expo(0, c, m, a0)

    def pair(m, a0, vb0, vb1, slot1_cols, next0, next1):
        a1 = blank()
        for c in all_cols:
            accum(0, vb0, c, a0)
            if c in slot1_cols:
                expo(1, c, m, a1)
            if next0 is not None:
                score(0, next0, c)
        for c in all_cols:
            if c in slot1_cols:
                accum(1, vb1, c, a1)
            if next0 is not None:
                expo(0, c, m, a0)
            if next1 is not None:
                score(1, next1, c)
        return m, a0

    if n_full == 0:
        pair(m, a0, v_blk(0), v_blk(1), d1_cols, None, None)
        return

    m, a0 = pair(m, a0, v_blk(n_full), v_blk(n_full + 1), d1_cols, 0, 1)

    def body(i, carry):
        m, a0 = [list(x) for x in carry[0]], [list(x) for x in carry[1]]
        return pair(m, a0, v_blk(2 * i - 2), v_blk(2 * i - 1), all_cols, 2 * i, 2 * i + 1)

    m, a0 = lax.fori_loop(1, n_full // 2, body, (m, a0))
    m, a0 = [list(x) for x in m], [list(x) for x in a0]
    pair(m, a0, v_blk(n_full - 2), v_blk(n_full - 1), all_cols, None, None)


def _diff_attn_kernel(q_ref, k_ref, vt_ref, g_ref, kmeta_ref, vtmeta_ref, eq_ref, ek_ref,
                      ekmeta_ref, core_ref, lq1_ref, lk1_ref, lq2_ref, lk2_ref, subln_ref,
                      o_ref, kaug_ref, kmaug_ref, vaug_ref, vtm_ref, s0_ref, s1_ref, mx0_ref, mx1_ref,
                      p0_ref, p1_ref, acc_ref, *, lambda_init, t, n_meta):
    d = DIFF_HEAD_DIM
    dv = DIFF_V_DIM
    tq = acc_ref.shape[2]

    kmaug_ref[...] = jnp.zeros(kmaug_ref.shape, BF16)
    _build_values(vaug_ref, vtm_ref, vt_ref[0], vtmeta_ref[...], dv, n_meta)
    for mm in range(2):
        kaug_ref[mm, :, 0:d] = k_ref[0, :, mm * d:(mm + 1) * d]
        kaug_ref[mm, :, d:2 * d] = ek_ref[0]
        kmaug_ref[mm, 0:n_meta, 0:d] = kmeta_ref[:, mm * d:(mm + 1) * d]
        kmaug_ref[mm, :, d:2 * d] = ekmeta_ref[0]

    def diag_fix(kd, c, s_t):
        lo = c * COL_TILE - kd * t
        if 0 <= lo < t:
            return s_t + core_ref[0, :, lo:lo + COL_TILE]
        return s_t

    def v_blk(blk):
        vb = vaug_ref[blk]
        return [vb, vb]

    lam = (jnp.exp(jnp.sum(lq1_ref[...] * lk1_ref[...], axis=-1, keepdims=True))
           - jnp.exp(jnp.sum(lq2_ref[...] * lk2_ref[...], axis=-1, keepdims=True))
           + lambda_init)

    for qi in range(q_ref.shape[1] // tq):
        rows = slice(qi * tq, (qi + 1) * tq)
        q = q_ref[0, rows, :]
        eq = eq_ref[0, rows, :]
        qa = [jnp.concatenate([q[:, mm * d:(mm + 1) * d], eq], axis=1) for mm in range(2)]

        def qk(mm, blk, cs, qa=qa):
            off = pl.multiple_of(blk * t, t)
            return _dot_nt(kaug_ref[mm, pl.ds(off, t), :], qa[mm][cs])

        _attention_sweep(2, qk, v_blk, lambda mm, cs, qa=qa: _dot_nt(kmaug_ref[mm], qa[mm][cs]),
                         [vtm_ref[...]] * 2, diag_fix,
                         (s0_ref, s1_ref), (mx0_ref, mx1_ref), (p0_ref, p1_ref), acc_ref,
                         qi * DIAG_BLOCKS)

        o_t = (acc_ref[0, 0:dv] / acc_ref[0, dv:dv + 1]
               - lam * (acc_ref[1, 0:dv] / acc_ref[1, dv:dv + 1]))
        ms = jnp.mean(o_t * o_t, axis=0, keepdims=True)
        o = (o_t * lax.rsqrt(ms + RMS_EPS)).T
        o = (o * subln_ref[...]) * (1.0 - lambda_init)
        o_ref[0, rows, :] = (o * _silu(g_ref[0, rows, :].astype(F32))).astype(o_ref.dtype)


def _diff_attn(z3, vt, z_meta, vt_meta, eq_tab, ek_tab, ekmeta_tab, core_tab, lams, subln,
               n_heads, blk, lambda_init, t, tq):
    b, s, _ = z3.shape
    n_meta = z_meta.shape[0]
    q0, k0, g0 = blk
    vec = pl.BlockSpec((1, DIFF_HEAD_DIM), lambda bb, h: (0, 0))
    kern = functools.partial(_diff_attn_kernel, lambda_init=lambda_init, t=t, n_meta=n_meta)
    return pl.pallas_call(
        kern,
        out_shape=jax.ShapeDtypeStruct((b, s, n_heads * DIFF_V_DIM), BF16),
        grid=(b, n_heads),
        in_specs=[
            pl.BlockSpec((1, s, HEAD_TILE), lambda bb, h: (bb, 0, q0 + h)),
            pl.BlockSpec((1, s, HEAD_TILE), lambda bb, h: (bb, 0, k0 + h)),
            pl.BlockSpec((1, s // t, DIFF_V_DIM, t), lambda bb, h: (bb, 0, h, 0)),
            pl.BlockSpec((1, s, DIFF_V_DIM), lambda bb, h: (bb, 0, g0 + h)),
            pl.BlockSpec((n_meta, HEAD_TILE), lambda bb, h: (0, k0 + h)),
            pl.BlockSpec((DIFF_V_DIM, n_meta), lambda bb, h: (h, 0)),
            pl.BlockSpec((1, s, LANES), lambda bb, h: (h, 0, 0)),
            pl.BlockSpec((1, s, LANES), lambda bb, h: (h, 0, 0)),
            pl.BlockSpec((1, LANES, LANES), lambda bb, h: (h, 0, 0)),
            pl.BlockSpec((1, t, t), lambda bb, h: (h, 0, 0)),
            vec, vec, vec, vec,
            pl.BlockSpec((1, DIFF_V_DIM), lambda bb, h: (0, 0)),
        ],
        out_specs=pl.BlockSpec((1, s, DIFF_V_DIM), lambda bb, h: (bb, 0, h)),
        scratch_shapes=[
            pltpu.VMEM((2, s, HEAD_TILE), BF16),
            pltpu.VMEM((2, LANES, HEAD_TILE), BF16),
            pltpu.VMEM((s // t, DIFF_V_DIM + ONES_ROWS, t), BF16),
            pltpu.VMEM((DIFF_V_DIM + ONES_ROWS, LANES), BF16),
            pltpu.VMEM((2, t, tq), F32),
            pltpu.VMEM((2, t, tq), F32),
            pltpu.VMEM((2, 1, tq), F32),
            pltpu.VMEM((2, 1, tq), F32),
            pltpu.VMEM((2, t, tq), BF16),
            pltpu.VMEM((2, t, tq), BF16),
            pltpu.VMEM((2, DIFF_V_DIM + ONES_ROWS, tq), F32),
        ],
        compiler_params=_params("arbitrary", "arbitrary"),
        name="diff_attn",
    )(z3, z3, vt, z3, z_meta, vt_meta, eq_tab, ek_tab, ekmeta_tab, core_tab, *lams, subln)


def _mla_attn_kernel(q_ref, k_ref, vt_ref, g_ref, kmeta_ref, vtmeta_ref, padk_ref,
                     o_ref, kmaug_ref, vaug_ref, vtm_ref, s0_ref, s1_ref, mx0_ref, mx1_ref,
                     p0_ref, p1_ref, acc_ref, *, t, n_meta, hp):
    dv = MLA_V_DIM
    heads = range(hp)
    tq = acc_ref.shape[2]

    for hh in heads:
        kmaug_ref[hh] = padk_ref[...]
        kmaug_ref[hh, 0:n_meta, :] = kmeta_ref[:, hh * HEAD_TILE:(hh + 1) * HEAD_TILE]
        _build_values(vaug_ref.at[hh], vtm_ref.at[hh], vt_ref[0, :, hh * dv:(hh + 1) * dv, :],
                      vtmeta_ref[hh * dv:(hh + 1) * dv, :], dv, n_meta)

    for qi in range(q_ref.shape[1] // tq):
        rows = slice(qi * tq, (qi + 1) * tq)
        q = [q_ref[0, rows, hh * HEAD_TILE:(hh + 1) * HEAD_TILE] for hh in heads]

        def qk(hh, blk, cs, q=q):
            off = pl.multiple_of(blk * t, t)
            return _dot_nt(k_ref[0, pl.ds(off, t), hh * HEAD_TILE:(hh + 1) * HEAD_TILE], q[hh][cs])

        _attention_sweep(hp, qk, lambda blk: [vaug_ref[hh, blk] for hh in heads],
                         lambda hh, cs, q=q: _dot_nt(kmaug_ref[hh], q[hh][cs]),
                         [vtm_ref[hh] for hh in heads], lambda kd, c, s_t: s_t,
                         (s0_ref, s1_ref), (mx0_ref, mx1_ref), (p0_ref, p1_ref), acc_ref,
                         qi * DIAG_BLOCKS)

        for hh in heads:
            o = (acc_ref[hh, 0:dv] / acc_ref[hh, dv:dv + 1]).T
            g = g_ref[0, rows, hh * dv:(hh + 1) * dv].astype(F32)
            o_ref[0, rows, hh * dv:(hh + 1) * dv] = (o * _silu(g)).astype(o_ref.dtype)


def _mla_attn(q3, k3, vt, z3, k_meta, vt_meta, padk, n_heads, g0, t, tq):
    b, s, _ = q3.shape
    n_meta = k_meta.shape[0]
    hp = _tile(n_heads, MLA_HEADS_PER_STEP, 1)
    kern = functools.partial(_mla_attn_kernel, t=t, n_meta=n_meta, hp=hp)
    return pl.pallas_call(
        kern,
        out_shape=jax.ShapeDtypeStruct((b, s, n_heads * MLA_V_DIM), BF16),
        grid=(b, n_heads // hp),
        in_specs=[
            pl.BlockSpec((1, s, hp * HEAD_TILE), lambda bb, h: (bb, 0, h)),
            pl.BlockSpec((1, s, hp * HEAD_TILE), lambda bb, h: (bb, 0, h)),
            pl.BlockSpec((1, s // t, hp * MLA_V_DIM, t), lambda bb, h: (bb, 0, h, 0)),
            pl.BlockSpec((1, s, hp * MLA_V_DIM), lambda bb, h: (bb, 0, g0 + h)),
            pl.BlockSpec((n_meta, hp * HEAD_TILE), lambda bb, h: (0, h)),
            pl.BlockSpec((hp * MLA_V_DIM, n_meta), lambda bb, h: (h, 0)),
            pl.BlockSpec((LANES, HEAD_TILE), lambda bb, h: (0, 0)),
        ],
        out_specs=pl.BlockSpec((1, s, hp * MLA_V_DIM), lambda bb, h: (bb, 0, h)),
        scratch_shapes=[
            pltpu.VMEM((hp, LANES, HEAD_TILE), BF16),
            pltpu.VMEM((hp, s // t, MLA_V_DIM + ONES_ROWS, t), BF16),
            pltpu.VMEM((hp, MLA_V_DIM + ONES_ROWS, LANES), BF16),
            pltpu.VMEM((hp, t, tq), F32),
            pltpu.VMEM((hp, t, tq), F32),
            pltpu.VMEM((hp, 1, tq), F32),
            pltpu.VMEM((hp, 1, tq), F32),
            pltpu.VMEM((hp, t, tq), BF16),
            pltpu.VMEM((hp, t, tq), BF16),
            pltpu.VMEM((hp, MLA_V_DIM + ONES_ROWS, tq), F32),
        ],
        compiler_params=_params("arbitrary", "arbitrary"),
        name="mla_attn",
    )(q3, k3, vt, z3, k_meta, vt_meta, padk)


def _outproj_kernel(a_ref, b_ref, wa_ref, wb_ref, y_ref):
    y = _dot(a_ref[...], wa_ref[...]) + _dot(b_ref[...], wb_ref[...])
    y_ref[...] = y.astype(y_ref.dtype)


def _outproj(mix_a, mix_b, w):
    m, ka = mix_a.shape
    n = w.shape[1]
    assert mix_b.shape[1] == ka and w.shape[0] == 2 * ka
    tm = _tile(m, 1024, 16)
    tn = _tile(n, 1024)
    return pl.pallas_call(
        _outproj_kernel,
        out_shape=jax.ShapeDtypeStruct((m, n), BF16),
        grid=(m // tm, n // tn),
        in_specs=[pl.BlockSpec((tm, ka), lambda i, j: (i, 0)),
                  pl.BlockSpec((tm, ka), lambda i, j: (i, 0)),
                  pl.BlockSpec((ka, tn), lambda i, j: (0, j)),
                  pl.BlockSpec((ka, tn), lambda i, j: (1, j))],
        out_specs=pl.BlockSpec((tm, tn), lambda i, j: (i, j)),
        compiler_params=_params("arbitrary", "arbitrary"),
        name="outproj",
    )(mix_a, mix_b, w, w)


def _postnorm_kernel(x_ref, y_ref, g_ref, o_ref):
    y = y_ref[...].astype(F32)
    ms = jnp.mean(y * y, axis=-1, keepdims=True)
    o_ref[...] = x_ref[...] + y * lax.rsqrt(ms + RMS_EPS) * g_ref[...]


def _postnorm(x2d, y, g):
    m, d = x2d.shape
    tm = _tile(m, 256, 16)
    return pl.pallas_call(
        _postnorm_kernel,
        out_shape=jax.ShapeDtypeStruct((m, d), F32),
        grid=(m // tm,),
        in_specs=[pl.BlockSpec((tm, d), lambda i: (i, 0)),
                  pl.BlockSpec((tm, d), lambda i: (i, 0)),
                  pl.BlockSpec((1, d), lambda i: (0, 0))],
        out_specs=pl.BlockSpec((tm, d), lambda i: (i, 0)),
        compiler_params=_params("arbitrary"),
        name="postnorm",
    )(x2d, y, g)


def _split3(v):
    v = np.asarray(v, np.float32)
    hi = v.astype(NP_BF16)
    r = v - hi.astype(np.float32)
    mid = r.astype(NP_BF16)
    lo = (r - mid.astype(np.float32)).astype(NP_BF16)
    return hi, mid, lo


def _rope_table(pos):
    inv_freq = 1.0 / (ROPE_THETA ** (jnp.arange(0, MLA_ROPE, 2, dtype=F32) / MLA_ROPE))
    ang = pos.astype(F32)[:, None] * inv_freq[None, :]
    cos, sin = jnp.cos(ang), jnp.sin(ang)
    return jnp.concatenate([cos, cos, sin, sin], axis=1)


def _chunk_mask_lanes(seq):
    chunk = np.arange(seq) // CHUNK
    c = np.arange(seq // CHUNK - 1)
    q_lanes = (chunk[:, None] == c[None, :]).astype(np.float32)
    k_lanes = np.where(chunk[:, None] > c[None, :], MASK_VALUE, 0.0).astype(np.float32)
    return q_lanes.astype(NP_BF16), k_lanes.astype(NP_BF16)


def _diff_tables(n_heads, n_meta, seq, t):
    f32 = np.float32
    slopes = (2.0 ** (-8.0 * np.arange(1, n_heads + 1, dtype=f32) / n_heads)).astype(f32)
    slopes = (slopes * f32(LOG2E)).astype(f32)
    pos_q = ((np.arange(seq, dtype=f32) + n_meta)[None, :] * slopes[:, None]).astype(f32)
    pos_m = (np.arange(LANES, dtype=f32)[None, :] * slopes[:, None]).astype(f32)
    qc, kc = _chunk_mask_lanes(seq)
    n_c = qc.shape[1]
    assert 8 + n_c <= LANES
    ones_q = np.ones((n_heads, seq), NP_BF16)
    zeros_q = np.zeros((n_heads, seq), NP_BF16)
    qh, qm, ql = _split3(-pos_q)
    kh, km, kl = _split3(pos_q)
    tail = np.zeros((n_heads, seq, LANES - 8 - n_c), NP_BF16)

    def lanes(cols, chunk_lanes):
        return np.concatenate([np.stack(cols + [zeros_q], -1),
                               np.broadcast_to(chunk_lanes, (n_heads,) + chunk_lanes.shape), tail], -1)

    eq = lanes([qh, qm, ql, ones_q, ones_q, ones_q, ones_q], qc)
    ek = lanes([ones_q, ones_q, ones_q, zeros_q, kh, km, kl], kc)
    mh, mmid, ml = _split3(pos_m)
    valid = (np.arange(LANES) < n_meta)[None, :]
    padmask = np.broadcast_to(np.where(valid, 0.0, MASK_VALUE).astype(f32),
                              (n_heads, LANES)).astype(NP_BF16)
    ones_m = np.ones((n_heads, LANES), NP_BF16)
    ekm = np.concatenate(
        [np.stack([ones_m, ones_m, ones_m, padmask, mh, mmid, ml], -1),
         np.zeros((n_heads, LANES, LANES - 7), NP_BF16)], -1)
    j = np.arange(t)[:, None]
    i = np.arange(t)[None, :]
    same = (j // CHUNK) == (i // CHUNK)
    fut = np.where(same, 2.0 * np.maximum(j - i, 0), 0.0).astype(f32)
    core = (-slopes[:, None, None] * fut[None]).astype(f32)
    return tuple(jnp.asarray(a) for a in (eq, ek, ekm, core))


def _mla_mask_lanes(n_meta, seq):
    qc, kc = _chunk_mask_lanes(seq)
    n_c = qc.shape[1]
    assert MLA_ROPE + 1 + n_c <= LANES
    q = np.zeros((seq, LANES), NP_BF16)
    k = np.zeros((seq, LANES), NP_BF16)
    q[:, MLA_ROPE] = 1.0
    q[:, MLA_ROPE + 1:MLA_ROPE + 1 + n_c] = qc
    k[:, MLA_ROPE + 1:MLA_ROPE + 1 + n_c] = kc
    return jnp.asarray(q), jnp.asarray(k), jnp.zeros((n_meta, LANES), BF16)


def kernel(x, meta_tokens, norm_pre, w_in, diff_lambda_q1, diff_lambda_k1, diff_lambda_q2,
           diff_lambda_k2, diff_subln, mla_norm_q, mla_norm_kv, w_uq, w_ukv, w_out, norm_post):
    assert norm_pre.shape[0] == 1, "single-layer block"
    b, s, d = x.shape
    n_meta = meta_tokens.shape[0]
    mix = w_out.shape[1]
    diff_w = mix // 2
    mla_w = mix - diff_w
    ha = diff_w // DIFF_V_DIM
    hb = mla_w // MLA_V_DIM
    q_lora = mla_norm_q.shape[-1]
    kv_lora = mla_norm_kv.shape[-1]
    t = min(KEY_TILE, s)
    tq = min(QUERY_TILE, s)
    assert s % tq == 0 and tq == DIAG_BLOCKS * t and t % CHUNK == 0 and n_meta <= 16
    lambda_init = 0.8 - 0.6 * math.exp(-0.3 * 0)

    w_all = jnp.transpose(w_in[0]).astype(BF16)
    lat = q_lora + kv_lora
    tn = _tile(math.gcd(diff_w, lat), 1024)
    n_main = 3 * diff_w + lat
    assert (3 * diff_w) % q_lora == 0 and (3 * diff_w + q_lora) % kv_lora == 0
    q_blk0 = 0
    k_blk0 = diff_w // HEAD_TILE
    g_blk0 = 2 * diff_w // HEAD_TILE
    cq_block = 3 * diff_w // q_lora
    ckv_block = (3 * diff_w + q_lora) // kv_lora
    colscale = jnp.ones((1, n_main), F32).at[:, :diff_w].set(DIFF_HEAD_DIM ** -0.5 * LOG2E)
    half = MLA_ROPE // 2
    kr0 = 4 * diff_w + lat
    w_krt = w_all[kr0:kr0 + MLA_ROPE]
    w_kr2 = jnp.concatenate([w_krt, -w_krt[half:], w_krt[:half]], axis=0)
    w_mg16 = w_all[kr0 + MLA_ROPE:kr0 + MLA_ROPE + mla_w]

    wq = w_uq[0].reshape(q_lora, hb, MLA_NOPE + MLA_ROPE)
    wq_r = wq[:, :, MLA_NOPE:]
    wq_full = jnp.concatenate(
        [wq[:, :, :MLA_NOPE], wq_r, jnp.concatenate([-wq_r[:, :, half:], wq_r[:, :, :half]], -1)],
        axis=-1).reshape(q_lora, hb * HEAD_TILE).astype(BF16)
    wkv = w_ukv[0].reshape(kv_lora, hb, MLA_NOPE + MLA_V_DIM)
    w_kn = wkv[:, :, :MLA_NOPE].reshape(kv_lora, hb * MLA_NOPE).astype(BF16)
    w_vt = wkv[:, :, MLA_NOPE:].reshape(kv_lora, hb * MLA_V_DIM).T.astype(BF16)
    wo = w_out[0].astype(BF16)

    pos = jnp.arange(n_meta + s, dtype=jnp.int32)
    rope_tab = _rope_table(pos)
    tab_meta, tab_seq = rope_tab[:n_meta], rope_tab[n_meta:]
    eq_tab, ek_tab, ekm_tab, core_tab = _diff_tables(ha, n_meta, s, t)
    qlane_tab, klane_tab, klane_meta = _mla_mask_lanes(n_meta, s)
    lane = jnp.arange(HEAD_TILE)
    row = jnp.arange(LANES)
    padk = jnp.where((lane[None, :] == MLA_NOPE + MLA_ROPE) & (row[:, None] >= n_meta),
                     MASK_VALUE, 0.0).astype(BF16)

    x2d = x.reshape(b * s, d)
    u = _prenorm(x2d, norm_pre)
    u_meta = _prenorm(meta_tokens.astype(x.dtype), norm_pre)
    main_tiles = (n_main // tn, 2 * diff_w // tn, diff_w // tn)
    z = _inproj(u, w_all, colscale, tn, *main_tiles)
    z_meta = _inproj(u_meta, w_all, colscale, tn, *main_tiles)
    z_gate, kr = _inproj_gate_kr(u, w_mg16, w_kr2)
    _, kr_meta = _inproj_gate_kr(u_meta, w_mg16, w_kr2)
    vt_a = _v_transposed(w_all, 2, diff_w, u, b, t)
    vt_a_meta = _v_transposed(w_all, 2, diff_w, u_meta, 1, n_meta)[0, 0]

    g_cq = mla_norm_q.astype(F32)
    g_ckv = mla_norm_kv.astype(F32)
    q_b = _mla_q(z, g_cq, wq_full, tab_seq, qlane_tab, hb, q_lora, cq_block, s)
    k_b, vt_b = _mla_kv(z, kr, g_ckv, w_kn, w_vt, tab_seq, klane_tab, hb, kv_lora, ckv_block,
                        b, s, t)
    k_b_meta, vt_b_meta = _mla_kv(z_meta, kr_meta, g_ckv, w_kn, w_vt, tab_meta, klane_meta, hb,
                                  kv_lora, ckv_block, 1, n_meta, n_meta)
    vt_b_meta = vt_b_meta[0, 0]

    z3 = z.reshape(b, s, n_main)
    lams = [v.astype(F32) for v in (diff_lambda_q1, diff_lambda_k1, diff_lambda_q2, diff_lambda_k2)]
    mix_a = _diff_attn(z3, vt_a, z_meta, vt_a_meta, eq_tab, ek_tab, ekm_tab, core_tab, lams,
                       diff_subln.astype(F32), ha, (q_blk0, k_blk0, g_blk0), lambda_init, t, tq)
    mix_b = _mla_attn(q_b.reshape(b, s, hb * HEAD_TILE), k_b.reshape(b, s, hb * HEAD_TILE), vt_b,
                      z_gate.reshape(b, s, mla_w), k_b_meta, vt_b_meta, padk, hb, 0, t, tq)

    y = _outproj(mix_a.reshape(b * s, diff_w), mix_b.reshape(b * s, mla_w), wo)
    out = _postnorm(x2d, y, norm_post.astype(F32))
    return out.reshape(b, s, d)
```
